```python
import jax
import jax.numpy as jnp
from jax import lax
import numpy as np

D_MODEL = 2048
BATCH = 4
SEQ = 2048
DEPTH = 2
DEC_BATCH = 128
DEC_SEQ = 1
PAST_LEN = 16384
PAGE_SIZE = 128

N_MIXERS = 2
N_CONV_LAYERS = (DEPTH + 1) // 2
N_GLA_LAYERS = DEPTH // 2
CONV_W = 3
GLA_HEADS = 4
GLA_DK = D_MODEL // 2 // GLA_HEADS
GLA_DV = D_MODEL // GLA_HEADS
GLA_HK = GLA_HEADS * GLA_DK
GLA_HV = GLA_HEADS * GLA_DV
GATE_RANK = 16
GATE_TAU = 16.0
GLA_PROJ = 2 * GLA_HK + 2 * GLA_HV + GATE_RANK
CHUNK = 64
D_FF = ((8 * D_MODEL // 3 + 255) // 256) * 256
N_EXPERTS = 8
TOP_K = 2
D_FF_EXPERT = 7 * D_MODEL // 2
EPS = 1e-6

kernel_name = 'hybrid_shortconv_gla_moe_step'


def rmsnorm(x, g):
    xf = x.astype(jnp.float32)
    y = xf * lax.rsqrt(jnp.mean(xf * xf, axis=-1, keepdims=True) + EPS)
    return (y * g.astype(jnp.float32)).astype(x.dtype)


def swiglu(x, w_gate, w_up, w_down):
    return (jax.nn.silu(x @ w_gate) * (x @ w_up)) @ w_down


def conv_mixer(x, buf, w_in, w_conv, w_out):
    T = x.shape[1]
    h, b, c = jnp.split(x @ w_in, 3, axis=-1)
    u = c * h
    ext = jnp.concatenate([buf.astype(u.dtype), u], axis=1)
    z = w_conv[0] * ext[:, 0:T]
    for j in range(1, CONV_W):
        z = z + w_conv[j] * ext[:, j:j + T]
    y = (b * z) @ w_out
    return y, ext[:, -(CONV_W - 1):]


def _pad_time(a, pad):
    return jnp.pad(a, ((0, 0), (0, pad)) + ((0, 0),) * (a.ndim - 2))


def gla_chunked(q, k, v, g, s0):
    N, T = q.shape[0], q.shape[1]
    c = min(CHUNK, T)
    nc = -(-T // c)
    pad = nc * c - T
    if pad:
        q, k, v, g = (_pad_time(a, pad) for a in (q, k, v, g))

    def to_chunks(a):
        return a.reshape((N, nc, c) + a.shape[2:]).swapaxes(0, 1)

    mask = jnp.tril(jnp.ones((c, c), dtype=bool))

    def step(S, inp):
        qc, kc, vc, gc = inp
        b = jnp.cumsum(gc, axis=1)
        b_last = b[:, -1]
        q_dec = qc * jnp.exp(b)
        k_inv = kc * jnp.exp(-b)
        k_end = kc * jnp.exp(b_last[:, None] - b)
        scores = jnp.einsum('nthk,nshk->nhts', q_dec, k_inv)
        scores = jnp.where(mask, scores, 0.0)
        o = (jnp.einsum('nhts,nshv->nthv', scores, vc)
             + jnp.einsum('nthk,nhkv->nthv', q_dec, S))
        S = S * jnp.exp(b_last)[..., None] + jnp.einsum('nshk,nshv->nhkv', k_end, vc)
        return S, o

    S, o = lax.scan(step, s0, (to_chunks(q), to_chunks(k), to_chunks(v), to_chunks(g)))
    o = o.swapaxes(0, 1).reshape((N, nc * c) + o.shape[3:])[:, :T]
    return o, S


def gla_mixer(x, s0, w_in, w_gate_up, b_gate, g_norm, w_out):
    N, T = x.shape[0], x.shape[1]
    f32 = jnp.float32
    q, k, v, r, a = jnp.split(x @ w_in, [GLA_HK, 2 * GLA_HK, 2 * GLA_HK + GLA_HV, 2 * GLA_HK + 2 * GLA_HV], axis=-1)
    q = q.astype(f32).reshape(N, T, GLA_HEADS, GLA_DK) * (GLA_DK ** -0.5)
    k = k.astype(f32).reshape(N, T, GLA_HEADS, GLA_DK)
    v = v.astype(f32).reshape(N, T, GLA_HEADS, GLA_DV)
    g = (jax.nn.log_sigmoid((a @ w_gate_up + b_gate).astype(f32)) / GATE_TAU).reshape(N, T, GLA_HEADS, GLA_DK)
    o, S = gla_chunked(q, k, v, g, s0.astype(f32))
    o = o * lax.rsqrt(jnp.mean(o * o, axis=-1, keepdims=True) + EPS) * g_norm.astype(f32)
    o = o.reshape(N, T, GLA_HV).astype(x.dtype)
    y = (jax.nn.silu(r) * o) @ w_out
    return y, S.astype(s0.dtype)


def moe(x, w_router, w_gate, w_up, w_down):
    N, T, D = x.shape
    xt = x.reshape(N * T, D)
    logits = (xt @ w_router).astype(jnp.float32)
    top_v, top_i = lax.top_k(logits, TOP_K)
    top_w = jax.nn.softmax(top_v, axis=-1)
    gates = jnp.sum(jax.nn.one_hot(top_i, N_EXPERTS, dtype=jnp.float32) * top_w[..., None], axis=1)
    out = jnp.zeros((N * T, D), jnp.float32)
    for e in range(N_EXPERTS):
        out = out + gates[:, e:e + 1] * swiglu(xt, w_gate[e], w_up[e], w_down[e]).astype(jnp.float32)
    return out.astype(x.dtype).reshape(N, T, D)


def setup_inputs(seed: int = 0):
    key = jax.random.key(seed)
    ks = list(jax.random.split(key, 32))
    cnt = [0]

    def nrm(shape, scale):
        kk = ks[cnt[0]]
        cnt[0] += 1
        return jax.random.normal(kk, shape, jnp.float32) * scale

    NA, NB, D = N_CONV_LAYERS, N_GLA_LAYERS, D_MODEL
    return {
        'x_prompt': nrm((BATCH, SEQ, D), 1.0),
        'x_sample': nrm((DEC_BATCH, DEC_SEQ, D), 1.0),
        'state_conv': nrm((NA, DEC_BATCH, CONV_W - 1, D), 1.0),
        'state_gla': nrm((NB, DEC_BATCH, GLA_HEADS, GLA_DK, GLA_DV), 1.0),
        'norm_mix': 1.0 + nrm((DEPTH, D), 0.02),
        'norm_ffn': 1.0 + nrm((DEPTH, D), 0.02),
        'norm_final': 1.0 + nrm((D,), 0.02),
        'conv_w_in': nrm((NA, D, 3 * D), D ** -0.5),
        'conv_w': nrm((NA, CONV_W, D), CONV_W ** -0.5),
        'conv_w_out': nrm((NA, D, D), D ** -0.5),
        'gla_w_in': nrm((NB, D, GLA_PROJ), D ** -0.5),
        'gla_w_gate': nrm((NB, GATE_RANK, GLA_HK), GATE_RANK ** -0.5),
        'gla_b_gate': nrm((NB, GLA_HK), 0.1),
        'gla_norm': 1.0 + nrm((NB, GLA_DV), 0.02),
        'gla_w_out': nrm((NB, GLA_HV, D), GLA_HV ** -0.5),
        'ffn_w_gate': nrm((NA, D, D_FF), D ** -0.5),
        'ffn_w_up': nrm((NA, D, D_FF), D ** -0.5),
        'ffn_w_down': nrm((NA, D_FF, D), D_FF ** -0.5),
        'moe_w_router': nrm((NB, D, N_EXPERTS), D ** -0.5),
        'moe_w_gate': nrm((NB, N_EXPERTS, D, D_FF_EXPERT), D ** -0.5),
        'moe_w_up': nrm((NB, N_EXPERTS, D, D_FF_EXPERT), D ** -0.5),
        'moe_w_down': nrm((NB, N_EXPERTS, D_FF_EXPERT, D), D_FF_EXPERT ** -0.5),
    }


def reference(x_prompt, x_sample, state_conv, state_gla, norm_mix, norm_ffn, norm_final,
              conv_w_in, conv_w, conv_w_out, gla_w_in, gla_w_gate, gla_b_gate, gla_norm, gla_w_out,
              ffn_w_gate, ffn_w_up, ffn_w_down, moe_w_router, moe_w_gate, moe_w_up, moe_w_down):

    def run(x, conv_st, gla_st):
        new_conv, new_gla = [], []
        for i in range(DEPTH):
            j = i // 2
            h = rmsnorm(x, norm_mix[i])
            if i % N_MIXERS == 0:
                y, st = conv_mixer(h, conv_st[j], conv_w_in[j], conv_w[j], conv_w_out[j])
                new_conv.append(st)
            else:
                y, st = gla_mixer(h, gla_st[j], gla_w_in[j], gla_w_gate[j], gla_b_gate[j], gla_norm[j], gla_w_out[j])
                new_gla.append(st)
            x = x + y
            h = rmsnorm(x, norm_ffn[i])
            if i % 2 == 0:
                x = x + swiglu(h, ffn_w_gate[j], ffn_w_up[j], ffn_w_down[j])
            else:
                x = x + moe(h, moe_w_router[j], moe_w_gate[j], moe_w_up[j], moe_w_down[j])
        return rmsnorm(x, norm_final), jnp.stack(new_conv), jnp.stack(new_gla)

    conv0 = jnp.zeros((N_CONV_LAYERS, x_prompt.shape[0], CONV_W - 1, D_MODEL), state_conv.dtype)
    gla0 = jnp.zeros((N_GLA_LAYERS, x_prompt.shape[0], GLA_HEADS, GLA_DK, GLA_DV), state_gla.dtype)
    y_prompt, new_conv_prompt, new_gla_prompt = run(x_prompt, conv0, gla0)
    y_sample, new_conv_sample, new_gla_sample = run(x_sample, state_conv, state_gla)
    return (y_prompt, y_sample, new_conv_prompt, new_conv_sample, new_gla_prompt, new_gla_sample)
```

```python
import functools

import jax
import jax.numpy as jnp
from jax import lax
from jax.experimental import pallas as pl
from jax.experimental.pallas import tpu as pltpu

F32 = jnp.float32
BF16 = jnp.bfloat16

D = 2048
N_PROMPT_SEQ = 4
SEQ = 2048
M_PROMPT = N_PROMPT_SEQ * SEQ
M_SAMPLE = 128
M_TOK = M_PROMPT + M_SAMPLE
HEADS = 4
DK = 256
DV = 512
HK = HEADS * DK
HV = HEADS * DV
GATE_RANK = 16
GATE_TAU = 16.0
CHUNK = 64
N_CHUNKS = SEQ // CHUNK
D_FF = 5632
N_EXPERTS = 8
D_FF_EXPERT = 7168
EPS = 1e-6
LANES = 128

TM_DENSE = 1040
GROUP_ROWS = 512
N_ASSIGN = 2 * M_TOK
N_GROUP_TILES = N_ASSIGN // GROUP_ROWS + N_EXPERTS
R_PAD = N_GROUP_TILES * GROUP_ROWS
TM_DOWN = 256
TM_COMBINE = 416


def _params(sem, vmem_mb):
    return pltpu.CompilerParams(dimension_semantics=sem,
                                vmem_limit_bytes=vmem_mb * 1024 * 1024)


def _rmsnorm(x, g):
    return x * lax.rsqrt(jnp.mean(x * x, axis=-1, keepdims=True) + EPS) * g


def _silu(x):
    return x * jax.nn.sigmoid(x)


def _norm_kernel(x_ref, g_ref, o_ref):
    o_ref[...] = _rmsnorm(x_ref[...], g_ref[...]).astype(o_ref.dtype)


def rmsnorm_bf16(x, g, tr=832):
    m = x.shape[0]
    return pl.pallas_call(
        _norm_kernel,
        grid=(m // tr,),
        in_specs=[pl.BlockSpec((tr, D), lambda i: (i, 0)),
                  pl.BlockSpec((1, D), lambda i: (0, 0))],
        out_specs=pl.BlockSpec((tr, D), lambda i: (i, 0)),
        out_shape=jax.ShapeDtypeStruct((m, D), BF16),
        compiler_params=_params(("arbitrary",), 40),
        name="rmsnorm",
    )(x, g.reshape(1, D))


def _linear_kernel(*refs, has_res):
    if has_res:
        x_ref, w_ref, r_ref, o_ref, wb_ref = refs
    else:
        x_ref, w_ref, o_ref, wb_ref = refs

    @pl.when(pl.program_id(1) == 0)
    def _():
        wb_ref[...] = w_ref[...].astype(BF16)

    acc = jnp.dot(x_ref[...], wb_ref[...], preferred_element_type=F32)
    if has_res:
        acc = acc + r_ref[...]
    o_ref[...] = acc.astype(o_ref.dtype)


def linear(x, w, layer, n_out, tn, tm, res=None, out_dtype=F32, vmem_mb=56, name="linear"):
    m, k = x.shape
    in_specs = [pl.BlockSpec((tm, k), lambda j, i: (i, 0)),
                pl.BlockSpec((None, k, tn), lambda j, i: (layer, 0, j))]
    args = [x, w]
    if res is not None:
        in_specs.append(pl.BlockSpec((tm, tn), lambda j, i: (i, j)))
        args.append(res)
    return pl.pallas_call(
        functools.partial(_linear_kernel, has_res=res is not None),
        grid=(n_out // tn, m // tm),
        in_specs=in_specs,
        out_specs=pl.BlockSpec((tm, tn), lambda j, i: (i, j)),
        out_shape=jax.ShapeDtypeStruct((m, n_out), out_dtype),
        scratch_shapes=[pltpu.VMEM((k, tn), BF16)],
        compiler_params=_params(("arbitrary", "arbitrary"), vmem_mb),
        name=name,
    )(*args)


def _swiglu_kernel(x_ref, wg_ref, wu_ref, o_ref, wgb_ref, wub_ref):
    @pl.when(pl.program_id(1) == 0)
    def _():
        wgb_ref[...] = wg_ref[...].astype(BF16)
        wub_ref[...] = wu_ref[...].astype(BF16)

    x = x_ref[...]
    g = jnp.dot(x, wgb_ref[...], preferred_element_type=F32)
    u = jnp.dot(x, wub_ref[...], preferred_element_type=F32)
    o_ref[...] = (_silu(g) * u).astype(o_ref.dtype)


def swiglu_up(x, wg, wu, layer, tn=512, tm=TM_DENSE):
    m, k = x.shape
    f = wg.shape[-1]
    wspec = pl.BlockSpec((None, k, tn), lambda j, i: (layer, 0, j))
    return pl.pallas_call(
        _swiglu_kernel,
        grid=(f // tn, m // tm),
        in_specs=[pl.BlockSpec((tm, k), lambda j, i: (i, 0)), wspec, wspec],
        out_specs=pl.BlockSpec((tm, tn), lambda j, i: (i, j)),
        out_shape=jax.ShapeDtypeStruct((m, f), BF16),
        scratch_shapes=[pltpu.VMEM((k, tn), BF16), pltpu.VMEM((k, tn), BF16)],
        compiler_params=_params(("arbitrary", "arbitrary"), 56),
        name="ffn_up",
    )(x, wg, wu)


CONV_ROWS = 128
CONV_BLOCKS_PER_SEQ = SEQ // CONV_ROWS
CONV_PROMPT_BLOCKS = M_PROMPT // CONV_ROWS


def _conv_kernel(h_ref, b_ref, c_ref, w_ref, buf_ref, bz_ref, stp_ref, sts_ref, carry_ref):
    i = pl.program_id(0)
    u = c_ref[...] * h_ref[...]
    w = w_ref[...]

    @pl.when(i < CONV_PROMPT_BLOCKS)
    def _():
        @pl.when(i % CONV_BLOCKS_PER_SEQ == 0)
        def _():
            carry_ref[...] = jnp.zeros_like(carry_ref)

        c2 = carry_ref[0:1, :]
        c1 = carry_ref[1:2, :]
        row = lax.broadcasted_iota(jnp.int32, u.shape, 0)
        u1 = jnp.where(row == 0, c1, pltpu.roll(u, 1, 0))
        u2 = jnp.where(row == 0, c2, jnp.where(row == 1, c1, pltpu.roll(u, 2, 0)))
        z = w[0:1, :] * u2
        z = z + w[1:2, :] * u1
        z = z + w[2:3, :] * u
        bz_ref[...] = (b_ref[...] * z).astype(bz_ref.dtype)
        tail = u[CONV_ROWS - 2:CONV_ROWS, :]
        carry_ref[0:2, :] = tail
        stp_ref[...] = tail

    @pl.when(i == CONV_PROMPT_BLOCKS)
    def _():
        b0 = buf_ref[:, :D]
        b1 = buf_ref[:, D:]
        z = w[0:1, :] * b0
        z = z + w[1:2, :] * b1
        z = z + w[2:3, :] * u
        bz_ref[...] = (b_ref[...] * z).astype(bz_ref.dtype)
        sts_ref[:, :D] = b1
        sts_ref[:, D:] = u


def conv_mix(p, conv_w, buf):
    last_seq = N_PROMPT_SEQ - 1
    return pl.pallas_call(
        _conv_kernel,
        grid=(M_TOK // CONV_ROWS,),
        in_specs=[pl.BlockSpec((CONV_ROWS, D), lambda i: (i, 0)),
                  pl.BlockSpec((CONV_ROWS, D), lambda i: (i, 1)),
                  pl.BlockSpec((CONV_ROWS, D), lambda i: (i, 2)),
                  pl.BlockSpec((3, D), lambda i: (0, 0)),
                  pl.BlockSpec((M_SAMPLE, 2 * D), lambda i: (0, 0))],
        out_specs=[pl.BlockSpec((CONV_ROWS, D), lambda i: (i, 0)),
                   pl.BlockSpec((None, 2, D),
                                lambda i: (jnp.minimum(i // CONV_BLOCKS_PER_SEQ, last_seq), 0, 0)),
                   pl.BlockSpec((M_SAMPLE, 2 * D), lambda i: (0, 0))],
        out_shape=[jax.ShapeDtypeStruct((M_TOK, D), BF16),
                   jax.ShapeDtypeStruct((N_PROMPT_SEQ, 2, D), F32),
                   jax.ShapeDtypeStruct((M_SAMPLE, 2 * D), F32)],
        scratch_shapes=[pltpu.VMEM((8, D), F32)],
        compiler_params=_params(("arbitrary",), 32),
        name="conv_mix",
    )(p, p, p, conv_w, buf)


def _log_sigmoid(x):
    return jnp.minimum(x, 0.0) - jnp.log1p(jnp.exp(-jnp.abs(x)))


def _gla_gate_kernel(h_ref, wa_ref, wg_ref, bg_ref, g_ref):
    a = jnp.dot(h_ref[...], wa_ref[...].astype(BF16), preferred_element_type=F32)
    lane = lax.broadcasted_iota(jnp.int32, a.shape, 1)
    a = jnp.where(lane < GATE_RANK, a, 0.0)
    z = jnp.dot(a.astype(BF16), wg_ref[...].astype(BF16), preferred_element_type=F32) + bg_ref[...]
    g_ref[...] = _log_sigmoid(z) * (1.0 / GATE_TAU)


def gla_gate(h, w_in, w_gate_pad, b_gate, tm=TM_DENSE):
    a_blk = (2 * HK + 2 * HV) // LANES
    return pl.pallas_call(
        _gla_gate_kernel,
        grid=(M_TOK // tm,),
        in_specs=[pl.BlockSpec((tm, D), lambda i: (i, 0)),
                  pl.BlockSpec((None, D, LANES), lambda i: (0, 0, a_blk)),
                  pl.BlockSpec((LANES, HK), lambda i: (0, 0)),
                  pl.BlockSpec((1, HK), lambda i: (0, 0))],
        out_specs=pl.BlockSpec((tm, HK), lambda i: (i, 0)),
        out_shape=jax.ShapeDtypeStruct((M_TOK, HK), F32),
        compiler_params=_params(("arbitrary",), 40),
        name="gla_gate",
    )(h, w_in, w_gate_pad, b_gate)


def _row_to_cols(row):
    return jnp.transpose(jnp.broadcast_to(row, (LANES, row.shape[1])))


def _split3_bf16(x):
    x1 = x.astype(BF16)
    r1 = x - x1.astype(F32)
    x2 = r1.astype(BF16)
    x3 = (r1 - x2.astype(F32)).astype(BF16)
    return x1, x2, x3


def _gla_prompt_kernel(o_init, q_ref, k_ref, v_ref, r_ref, g_ref, gn_ref, o_ref, sout_ref, s_ref):
    del o_init
    c = pl.program_id(1)

    @pl.when(c == 0)
    def _():
        s_ref[...] = jnp.zeros_like(s_ref)

    row = lax.broadcasted_iota(jnp.int32, (CHUNK, CHUNK), 0)
    col = lax.broadcasted_iota(jnp.int32, (CHUNK, CHUNK), 1)
    tri = row >= col
    trib = tri.astype(BF16)

    g1, g2, g3 = _split3_bf16(g_ref[...])
    b = (jnp.dot(trib, g1, preferred_element_type=F32)
         + jnp.dot(trib, g2, preferred_element_type=F32)
         + jnp.dot(trib, g3, preferred_element_type=F32))
    b_last = b[CHUNK - 1:CHUNK, :]
    q = q_ref[...] * (DK ** -0.5)
    k = k_ref[...]
    q_dec = (q * jnp.exp(b)).astype(BF16)
    k_inv = (k * jnp.exp(-b)).astype(BF16)
    k_end = (k * jnp.exp(b_last - b)).astype(BF16)
    decay = jnp.exp(b_last)
    gn = gn_ref[...]

    for h in range(HEADS):
        ks = slice(h * DK, (h + 1) * DK)
        vs = slice(h * DV, (h + 1) * DV)
        vb = v_ref[:, vs].astype(BF16)
        s_old = s_ref[h]
        scores = lax.dot_general(q_dec[:, ks], k_inv[:, ks], (((1,), (1,)), ((), ())),
                                 preferred_element_type=F32)
        scores = jnp.where(tri, scores, 0.0).astype(BF16)
        o = (jnp.dot(scores, vb, preferred_element_type=F32)
             + jnp.dot(q_dec[:, ks], s_old.astype(BF16), preferred_element_type=F32))
        kv = lax.dot_general(k_end[:, ks], vb, (((0,), (0,)), ((), ())),
                             preferred_element_type=F32)
        dcol = _row_to_cols(decay[:, ks])
        s_ref[h] = jnp.concatenate(
            [s_old[:, j * LANES:(j + 1) * LANES] * dcol for j in range(DV // LANES)], axis=1) + kv
        on = _rmsnorm(o, gn)
        o_ref[:, vs] = (_silu(r_ref[:, vs]) * on).astype(o_ref.dtype)

    @pl.when(c == N_CHUNKS - 1)
    def _():
        sout_ref[...] = s_ref[...]


def gla_prompt(p, g, g_norm):
    rows = lambda n, c: n * N_CHUNKS + c
    return pl.pallas_call(
        _gla_prompt_kernel,
        grid=(N_PROMPT_SEQ, N_CHUNKS),
        in_specs=[pl.BlockSpec(memory_space=pl.ANY),
                  pl.BlockSpec((CHUNK, HK), lambda n, c: (rows(n, c), 0)),
                  pl.BlockSpec((CHUNK, HK), lambda n, c: (rows(n, c), 1)),
                  pl.BlockSpec((CHUNK, HV), lambda n, c: (rows(n, c), 1)),
                  pl.BlockSpec((CHUNK, HV), lambda n, c: (rows(n, c), 2)),
                  pl.BlockSpec((CHUNK, HK), lambda n, c: (rows(n, c), 0)),
                  pl.BlockSpec((1, DV), lambda n, c: (0, 0))],
        out_specs=[pl.BlockSpec((CHUNK, HV), lambda n, c: (rows(n, c), 0)),
                   pl.BlockSpec((None, HEADS, DK, DV), lambda n, c: (n, 0, 0, 0))],
        out_shape=[jax.ShapeDtypeStruct((M_TOK, HV), BF16),
                   jax.ShapeDtypeStruct((N_PROMPT_SEQ, HEADS, DK, DV), F32)],
        scratch_shapes=[pltpu.VMEM((HEADS, DK, DV), F32)],
        input_output_aliases={0: 0},
        compiler_params=_params(("arbitrary", "arbitrary"), 40),
        name="gla_prompt",
    )(jnp.zeros((M_TOK, HV), BF16), p, p, p, p, g, g_norm)


GLA_SAMPLE_SEQS = 2


def _gla_sample_kernel(gated_any, q_ref, k_ref, v_ref, r_ref, g_ref, gn_ref, s_ref,
                       gated_ref, sout_ref, qt_ref, kt_ref, et_ref, o_scr):
    del gated_any
    i = pl.program_id(0)

    @pl.when(i == 0)
    def _():
        q = q_ref[...] * (DK ** -0.5)
        k = k_ref[...]
        e = jnp.exp(g_ref[...])
        for h in range(HEADS):
            ks = slice(h * DK, (h + 1) * DK)
            qt_ref[h] = jnp.transpose(q[:, ks])
            kt_ref[h] = jnp.transpose(k[:, ks])
            et_ref[h] = jnp.transpose(e[:, ks])

    lane = lax.broadcasted_iota(jnp.int32, (DK, M_SAMPLE), 1)
    for s in range(GLA_SAMPLE_SEQS):
        n = i * GLA_SAMPLE_SEQS + s
        pick = lane == n

        def column(t):
            return jnp.sum(jnp.where(pick, t, 0.0), axis=1, keepdims=True)

        for h in range(HEADS):
            vs = slice(h * DV, (h + 1) * DV)
            v_row = v_ref[pl.ds(n, 1), vs]
            s_new = s_ref[s, h] * column(et_ref[h]) + column(kt_ref[h]) * v_row
            sout_ref[s, h] = s_new
            o_scr[pl.ds(n, 1), vs] = jnp.sum(column(qt_ref[h]) * s_new, axis=0, keepdims=True)

    @pl.when(i == pl.num_programs(0) - 1)
    def _():
        gn = gn_ref[...]
        for h in range(HEADS):
            vs = slice(h * DV, (h + 1) * DV)
            on = _rmsnorm(o_scr[:, vs], gn)
            gated_ref[:, vs] = (_silu(r_ref[:, vs]) * on).astype(gated_ref.dtype)


def gla_sample(gated, p, g, g_norm, state):
    rb = M_PROMPT // M_SAMPLE
    bs = GLA_SAMPLE_SEQS
    return pl.pallas_call(
        _gla_sample_kernel,
        grid=(M_SAMPLE // bs,),
        in_specs=[pl.BlockSpec(memory_space=pl.ANY),
                  pl.BlockSpec((M_SAMPLE, HK), lambda i: (rb, 0)),
                  pl.BlockSpec((M_SAMPLE, HK), lambda i: (rb, 1)),
                  pl.BlockSpec((M_SAMPLE, HV), lambda i: (rb, 1)),
                  pl.BlockSpec((M_SAMPLE, HV), lambda i: (rb, 2)),
                  pl.BlockSpec((M_SAMPLE, HK), lambda i: (rb, 0)),
                  pl.BlockSpec((1, DV), lambda i: (0, 0)),
                  pl.BlockSpec((bs, HEADS, DK, DV), lambda i: (i, 0, 0, 0))],
        out_specs=[pl.BlockSpec((M_SAMPLE, HV), lambda i: (rb, 0)),
                   pl.BlockSpec((bs, HEADS, DK, DV), lambda i: (i, 0, 0, 0))],
        out_shape=[jax.ShapeDtypeStruct((M_TOK, HV), BF16),
                   jax.ShapeDtypeStruct((M_SAMPLE, HEADS, DK, DV), F32)],
        scratch_shapes=[pltpu.VMEM((HEADS, DK, M_SAMPLE), F32),
                        pltpu.VMEM((HEADS, DK, M_SAMPLE), F32),
                        pltpu.VMEM((HEADS, DK, M_SAMPLE), F32),
                        pltpu.VMEM((M_SAMPLE, HV), F32)],
        input_output_aliases={0: 0},
        compiler_params=_params(("arbitrary",), 40),
        name="gla_sample",
    )(gated, p, p, p, p, g, g_norm, state)


def _router_kernel(x_ref, g_ref, wr_ref, h_ref, route_ref, idx_ref):
    h = _rmsnorm(x_ref[...], g_ref[...])
    h_ref[...] = h
    logits = jnp.dot(h, wr_ref[...], preferred_element_type=F32, precision=lax.Precision.HIGHEST)
    lane = lax.broadcasted_iota(jnp.int32, logits.shape, 1)
    lane_f = lane.astype(F32)
    neg = jnp.float32(-jnp.inf)
    logits = jnp.where(lane < N_EXPERTS, logits, neg)
    m1 = jnp.max(logits, axis=1, keepdims=True)
    i1 = jnp.min(jnp.where(logits == m1, lane_f, float(LANES)), axis=1, keepdims=True)
    rest = jnp.where(lane_f == i1, neg, logits)
    m2 = jnp.max(rest, axis=1, keepdims=True)
    i2 = jnp.min(jnp.where(rest == m2, lane_f, float(LANES)), axis=1, keepdims=True)
    e2 = jnp.exp(m2 - m1)
    den = 1.0 + e2
    w1 = 1.0 / den
    w2 = e2 / den
    route_ref[...] = jnp.where(lane == 0, w1, jnp.where(lane == 1, w2, 0.0))
    idx_ref[...] = jnp.where(lane == 0, i1, jnp.where(lane == 1, i2, 0.0)).astype(jnp.int32)


def router(x, g, w_router_pad, tm=416):
    return pl.pallas_call(
        _router_kernel,
        grid=(M_TOK // tm,),
        in_specs=[pl.BlockSpec((tm, D), lambda i: (i, 0)),
                  pl.BlockSpec((1, D), lambda i: (0, 0)),
                  pl.BlockSpec((D, LANES), lambda i: (0, 0))],
        out_specs=[pl.BlockSpec((tm, D), lambda i: (i, 0)),
                   pl.BlockSpec((tm, LANES), lambda i: (i, 0)),
                   pl.BlockSpec((tm, LANES), lambda i: (i, 0))],
        out_shape=[jax.ShapeDtypeStruct((M_TOK, D), F32),
                   jax.ShapeDtypeStruct((M_TOK, LANES), F32),
                   jax.ShapeDtypeStruct((M_TOK, LANES), jnp.int32)],
        compiler_params=_params(("arbitrary",), 40),
        name="router",
    )(x, g.reshape(1, D), w_router_pad)


def _gather_kernel(tok_ref, nvalid_ref, src_hbm, o_ref, buf, sem):
    i = pl.program_id(0)
    tm = buf.shape[0]

    @pl.when(i < nvalid_ref[0])
    def _():
        base = i * tm

        def issue(r, carry):
            t = tok_ref[base + r]
            pltpu.make_async_copy(src_hbm.at[pl.ds(t, 1)], buf.at[pl.ds(r, 1)], sem).start()
            return carry

        lax.fori_loop(0, tm, issue, 0, unroll=8)

        def drain(r, carry):
            pltpu.make_async_copy(src_hbm.at[pl.ds(0, 1)], buf.at[pl.ds(r, 1)], sem).wait()
            return carry

        lax.fori_loop(0, tm, drain, 0, unroll=8)
        o_ref[...] = buf[...].astype(o_ref.dtype)

    @pl.when(i >= nvalid_ref[0])
    def _():
        o_ref[...] = jnp.zeros_like(o_ref)


def gather_rows(row_token, n_valid_tiles, src, tm=GROUP_ROWS):
    return pl.pallas_call(
        _gather_kernel,
        grid_spec=pltpu.PrefetchScalarGridSpec(
            num_scalar_prefetch=2,
            grid=(R_PAD // tm,),
            in_specs=[pl.BlockSpec(memory_space=pl.ANY)],
            out_specs=pl.BlockSpec((tm, D), lambda i, tok, nv: (i, 0)),
            scratch_shapes=[pltpu.VMEM((tm, D), F32), pltpu.SemaphoreType.DMA(())]),
        out_shape=jax.ShapeDtypeStruct((R_PAD, D), BF16),
        compiler_params=_params(("arbitrary",), 32),
        name="moe_gather",
    )(row_token, n_valid_tiles, src)


def _gswiglu_kernel(te, tf, tv, tb, x_ref, wg_ref, wu_ref, o_ref, wgb_ref, wub_ref):
    i = pl.program_id(1)

    @pl.when(tf[i] == 1)
    def _():
        wgb_ref[...] = wg_ref[...].astype(BF16)
        wub_ref[...] = wu_ref[...].astype(BF16)

    @pl.when(tv[i] == 1)
    def _():
        x = x_ref[...]
        g = jnp.dot(x, wgb_ref[...], preferred_element_type=F32)
        u = jnp.dot(x, wub_ref[...], preferred_element_type=F32)
        o_ref[...] = (_silu(g) * u).astype(o_ref.dtype)

    @pl.when(tv[i] == 0)
    def _():
        o_ref[...] = jnp.zeros_like(o_ref)


def moe_up(xs, wg, wu, tables, tm=GROUP_ROWS, tn=512):
    te, tf, tv, tb = tables
    k, f = D, D_FF_EXPERT
    wspec = pl.BlockSpec((None, None, k, tn), lambda j, i, te, tf, tv, tb: (0, te[i], 0, j))
    return pl.pallas_call(
        _gswiglu_kernel,
        grid_spec=pltpu.PrefetchScalarGridSpec(
            num_scalar_prefetch=4,
            grid=(f // tn, R_PAD // tm),
            in_specs=[pl.BlockSpec((tm, k), lambda j, i, te, tf, tv, tb: (tb[i], 0)), wspec, wspec],
            out_specs=pl.BlockSpec((tm, tn), lambda j, i, te, tf, tv, tb: (i, j)),
            scratch_shapes=[pltpu.VMEM((k, tn), BF16), pltpu.VMEM((k, tn), BF16)]),
        out_shape=jax.ShapeDtypeStruct((R_PAD, f), BF16),
        compiler_params=_params(("arbitrary", "arbitrary"), 56),
        name="moe_up",
    )(te, tf, tv, tb, xs, wg, wu)


def _gdown_kernel(te, tf, tv, tb, a_ref, w_ref, o_ref, wb_ref):
    i = pl.program_id(1)

    @pl.when(tf[i] == 1)
    def _():
        wb_ref[...] = w_ref[...].astype(BF16)

    @pl.when(tv[i] == 1)
    def _():
        o_ref[...] = jnp.dot(a_ref[...], wb_ref[...], preferred_element_type=F32)

    @pl.when(tv[i] == 0)
    def _():
        o_ref[...] = jnp.zeros_like(o_ref)


def moe_down(a, wd, tables, tm=TM_DOWN, tn=512):
    te, tf, tv, tb = tables
    k, n = D_FF_EXPERT, D
    return pl.pallas_call(
        _gdown_kernel,
        grid_spec=pltpu.PrefetchScalarGridSpec(
            num_scalar_prefetch=4,
            grid=(n // tn, R_PAD // tm),
            in_specs=[pl.BlockSpec((tm, k), lambda j, i, te, tf, tv, tb: (tb[i], 0)),
                      pl.BlockSpec((None, None, k, tn), lambda j, i, te, tf, tv, tb: (0, te[i], 0, j))],
            out_specs=pl.BlockSpec((tm, tn), lambda j, i, te, tf, tv, tb: (i, j)),
            scratch_shapes=[pltpu.VMEM((k, tn), BF16)]),
        out_shape=jax.ShapeDtypeStruct((R_PAD, n), F32),
        compiler_params=_params(("arbitrary", "arbitrary"), 56),
        name="moe_down",
    )(te, tf, tv, tb, a, wd)


def _combine_kernel(p1_ref, p2_ref, x_ref, route_ref, g_ref, y_hbm, o_ref, b1, b2, sem1, sem2):
    i = pl.program_id(0)
    tm = b1.shape[0]
    base = i * tm

    def issue(r, carry):
        pltpu.make_async_copy(y_hbm.at[pl.ds(p1_ref[base + r], 1)], b1.at[pl.ds(r, 1)], sem1).start()
        pltpu.make_async_copy(y_hbm.at[pl.ds(p2_ref[base + r], 1)], b2.at[pl.ds(r, 1)], sem2).start()
        return carry

    lax.fori_loop(0, tm, issue, 0, unroll=8)

    def drain(r, carry):
        pltpu.make_async_copy(y_hbm.at[pl.ds(0, 1)], b1.at[pl.ds(r, 1)], sem1).wait()
        pltpu.make_async_copy(y_hbm.at[pl.ds(0, 1)], b2.at[pl.ds(r, 1)], sem2).wait()
        return carry

    lax.fori_loop(0, tm, drain, 0, unroll=8)
    route = route_ref[...]
    w1 = route[:, 0:1]
    w2 = route[:, 1:2]
    x = x_ref[...] + (w1 * b1[...] + w2 * b2[...])
    o_ref[...] = _rmsnorm(x, g_ref[...])


def moe_combine(pos1, pos2, x, route, g_final, y, tm=TM_COMBINE):
    return pl.pallas_call(
        _combine_kernel,
        grid_spec=pltpu.PrefetchScalarGridSpec(
            num_scalar_prefetch=2,
            grid=(M_TOK // tm,),
            in_specs=[pl.BlockSpec((tm, D), lambda i, p1, p2: (i, 0)),
                      pl.BlockSpec((tm, LANES), lambda i, p1, p2: (i, 0)),
                      pl.BlockSpec((1, D), lambda i, p1, p2: (0, 0)),
                      pl.BlockSpec(memory_space=pl.ANY)],
            out_specs=pl.BlockSpec((tm, D), lambda i, p1, p2: (i, 0)),
            scratch_shapes=[pltpu.VMEM((tm, D), F32), pltpu.VMEM((tm, D), F32),
                            pltpu.SemaphoreType.DMA(()), pltpu.SemaphoreType.DMA(())]),
        out_shape=jax.ShapeDtypeStruct((M_TOK, D), F32),
        compiler_params=_params(("arbitrary",), 40),
        name="moe_combine",
    )(pos1, pos2, x, route, g_final.reshape(1, D), y)


def _group_tables(idx):
    e_flat = jnp.concatenate([idx[:, 0], idx[:, 1]])
    onehot = (e_flat[:, None] == jnp.arange(N_EXPERTS, dtype=jnp.int32)[None, :]).astype(jnp.int32)
    csum = jnp.cumsum(onehot, axis=0)
    counts = csum[-1]
    rank = jnp.sum(csum * onehot, axis=1) - 1
    tiles_per = (counts + GROUP_ROWS - 1) // GROUP_ROWS
    tile_end = jnp.cumsum(tiles_per)
    starts = (tile_end - tiles_per) * GROUP_ROWS
    dest = jnp.sum(onehot * starts[None, :], axis=1) + rank
    token = jnp.concatenate([jnp.arange(M_TOK, dtype=jnp.int32)] * 2)
    row_token = jnp.zeros((R_PAD,), jnp.int32).at[dest].set(token)
    n_valid = tile_end[-1]

    def tables(tile_rows):
        sub = GROUP_ROWS // tile_rows
        t = jnp.arange(N_GROUP_TILES * sub, dtype=jnp.int32)
        nv = n_valid * sub
        tb = jnp.minimum(t, nv - 1)
        te = jnp.sum((tb[:, None] // sub >= tile_end[None, :]).astype(jnp.int32), axis=1)
        tv = (t < nv).astype(jnp.int32)
        prev = jnp.concatenate([jnp.full((1,), -1, jnp.int32), te[:-1]])
        tf = tv * (te != prev).astype(jnp.int32)
        return te, tf, tv, tb

    return row_token, n_valid.reshape(1), dest[:M_TOK], dest[M_TOK:], tables


def kernel(x_prompt, x_sample, state_conv, state_gla, norm_mix, norm_ffn, norm_final,
           conv_w_in, conv_w, conv_w_out, gla_w_in, gla_w_gate, gla_b_gate, gla_norm, gla_w_out,
           ffn_w_gate, ffn_w_up, ffn_w_down, moe_w_router, moe_w_gate, moe_w_up, moe_w_down):
    x0 = jnp.concatenate([x_prompt.reshape(M_PROMPT, D), x_sample.reshape(M_SAMPLE, D)], axis=0)

    h = rmsnorm_bf16(x0, norm_mix[0])
    p = linear(h, conv_w_in, 0, 3 * D, tn=1024, tm=TM_DENSE, name="conv_in")
    bz, conv_prompt_state, conv_sample_state = conv_mix(
        p, conv_w[0], state_conv[0].reshape(M_SAMPLE, 2 * D))
    conv_sample_state = conv_sample_state.reshape(1, M_SAMPLE, 2, D)
    x1 = linear(bz, conv_w_out, 0, D, tn=1024, tm=TM_DENSE, res=x0, name="conv_out")

    h = rmsnorm_bf16(x1, norm_ffn[0])
    a = swiglu_up(h, ffn_w_gate, ffn_w_up, 0)
    x2 = linear(a, ffn_w_down, 0, D, tn=512, tm=640, res=x1, name="ffn_down")

    h = rmsnorm_bf16(x2, norm_mix[1])
    p = linear(h, gla_w_in, 0, 2 * HK + 2 * HV, tn=1024, tm=TM_DENSE, name="gla_in")
    w_gate_pad = jnp.pad(gla_w_gate[0], ((0, LANES - GATE_RANK), (0, 0)))
    g = gla_gate(h, gla_w_in, w_gate_pad, gla_b_gate[0].reshape(1, HK))
    gn = gla_norm[0].reshape(1, DV)
    gated, gla_prompt_state = gla_prompt(p, g, gn)
    gated, gla_sample_state = gla_sample(gated, p, g, gn, state_gla[0])
    x3 = linear(gated, gla_w_out, 0, D, tn=1024, tm=TM_DENSE, res=x2, name="gla_out")

    w_router_pad = jnp.pad(moe_w_router[0], ((0, 0), (0, LANES - N_EXPERTS)))
    hn, route, idx = router(x3, norm_ffn[1], w_router_pad)
    row_token, n_valid, pos1, pos2, tables = _group_tables(idx)
    xs = gather_rows(row_token, n_valid, hn)
    act = moe_up(xs, moe_w_gate, moe_w_up, tables(GROUP_ROWS))
    y = moe_down(act, moe_w_down, tables(TM_DOWN))
    out = moe_combine(pos1, pos2, x3, route, norm_final, y)

    y_prompt = out[:M_PROMPT].reshape(N_PROMPT_SEQ, SEQ, D)
    y_sample = out[M_PROMPT:].reshape(M_SAMPLE, 1, D)
    return (y_prompt, y_sample,
            conv_prompt_state.reshape(1, N_PROMPT_SEQ, 2, D), conv_sample_state,
            gla_prompt_state.reshape(1, N_PROMPT_SEQ, HEADS, DK, DV),
            gla_sample_state.reshape(1, M_SAMPLE, HEADS, DK, DV))
```

```python
import functools

import jax
import jax.numpy as jnp
from jax import lax
from jax.experimental import pallas as pl
from jax.experimental.pallas import tpu as pltpu

F32 = jnp.float32
BF16 = jnp.bfloat16

D = 2048
N_PROMPT_SEQ = 4
SEQ = 2048
M_PROMPT = N_PROMPT_SEQ * SEQ
M_SAMPLE = 128
M_TOK = M_PROMPT + M_SAMPLE
HEADS = 4
DK = 256
DV = 512
HK = HEADS * DK
HV = HEADS * DV
GATE_RANK = 16
GATE_TAU = 16.0
CHUNK = 64
N_CHUNKS = SEQ // CHUNK
D_FF = 5632
N_EXPERTS = 8
D_FF_EXPERT = 7168
EPS = 1e-6
LANES = 128

TM_DENSE = 1040
GROUP_ROWS = 512
N_ASSIGN = 2 * M_TOK
N_GROUP_TILES = N_ASSIGN // GROUP_ROWS + N_EXPERTS
R_PAD = N_GROUP_TILES * GROUP_ROWS
TM_DOWN = 256


def _params(sem, vmem_mb):
    return pltpu.CompilerParams(dimension_semantics=sem,
                                vmem_limit_bytes=vmem_mb * 1024 * 1024)


def _rmsnorm(x, g):
    return x * lax.rsqrt(jnp.mean(x * x, axis=-1, keepdims=True) + EPS) * g


def _silu(x):
    return x * jax.nn.sigmoid(x)


def _norm_kernel(x_ref, g_ref, o_ref):
    o_ref[...] = _rmsnorm(x_ref[...], g_ref[...]).astype(o_ref.dtype)


def rmsnorm_bf16(x, g, tr=832):
    m = x.shape[0]
    return pl.pallas_call(
        _norm_kernel,
        grid=(m // tr,),
        in_specs=[pl.BlockSpec((tr, D), lambda i: (i, 0)),
                  pl.BlockSpec((1, D), lambda i: (0, 0))],
        out_specs=pl.BlockSpec((tr, D), lambda i: (i, 0)),
        out_shape=jax.ShapeDtypeStruct((m, D), BF16),
        compiler_params=_params(("arbitrary",), 40),
        name="rmsnorm",
    )(x, g.reshape(1, D))


STACK_ROWS = 512
STACK_PROMPT_STEPS = M_PROMPT // STACK_ROWS


def _stack_norm_kernel(xp_ref, xs_ref, g_ref, x_ref, h_ref):
    i = pl.program_id(0)
    g = g_ref[...]

    @pl.when(i < STACK_PROMPT_STEPS)
    def _():
        x = xp_ref[...]
        x_ref[...] = x
        h_ref[...] = _rmsnorm(x, g).astype(h_ref.dtype)

    @pl.when(i == STACK_PROMPT_STEPS)
    def _():
        x = xs_ref[...]
        x_ref[0:M_SAMPLE, :] = x
        h_ref[0:M_SAMPLE, :] = _rmsnorm(x, g).astype(h_ref.dtype)


def stack_norm(xp, xs, g):
    last_prompt = STACK_PROMPT_STEPS - 1
    return pl.pallas_call(
        _stack_norm_kernel,
        grid=(STACK_PROMPT_STEPS + 1,),
        in_specs=[pl.BlockSpec((STACK_ROWS, D), lambda i: (jnp.minimum(i, last_prompt), 0)),
                  pl.BlockSpec((M_SAMPLE, D), lambda i: (0, 0)),
                  pl.BlockSpec((1, D), lambda i: (0, 0))],
        out_specs=[pl.BlockSpec((STACK_ROWS, D), lambda i: (i, 0)),
                   pl.BlockSpec((STACK_ROWS, D), lambda i: (i, 0))],
        out_shape=[jax.ShapeDtypeStruct((M_TOK, D), F32),
                   jax.ShapeDtypeStruct((M_TOK, D), BF16)],
        compiler_params=_params(("arbitrary",), 40),
        name="stack_norm",
    )(xp, xs, g.reshape(1, D))


def _linear_kernel(*refs, has_res):
    if has_res:
        x_ref, w_ref, r_ref, o_ref, wb_ref = refs
    else:
        x_ref, w_ref, o_ref, wb_ref = refs

    @pl.when(pl.program_id(1) == 0)
    def _():
        wb_ref[...] = w_ref[...].astype(BF16)

    acc = jnp.dot(x_ref[...], wb_ref[...], preferred_element_type=F32)
    if has_res:
        acc = acc + r_ref[...]
    o_ref[...] = acc.astype(o_ref.dtype)


def linear(x, w, layer, n_out, tn, tm, res=None, out_dtype=F32, vmem_mb=56, name="linear"):
    m, k = x.shape
    in_specs = [pl.BlockSpec((tm, k), lambda j, i: (i, 0)),
                pl.BlockSpec((None, k, tn), lambda j, i: (layer, 0, j))]
    args = [x, w]
    if res is not None:
        in_specs.append(pl.BlockSpec((tm, tn), lambda j, i: (i, j)))
        args.append(res)
    return pl.pallas_call(
        functools.partial(_linear_kernel, has_res=res is not None),
        grid=(n_out // tn, m // tm),
        in_specs=in_specs,
        out_specs=pl.BlockSpec((tm, tn), lambda j, i: (i, j)),
        out_shape=jax.ShapeDtypeStruct((m, n_out), out_dtype),
        scratch_shapes=[pltpu.VMEM((k, tn), BF16)],
        compiler_params=_params(("arbitrary", "arbitrary"), vmem_mb),
        name=name,
    )(*args)


def _swiglu_kernel(x_ref, wg_ref, wu_ref, o_ref, wgb_ref, wub_ref):
    @pl.when(pl.program_id(1) == 0)
    def _():
        wgb_ref[...] = wg_ref[...].astype(BF16)
        wub_ref[...] = wu_ref[...].astype(BF16)

    x = x_ref[...]
    g = jnp.dot(x, wgb_ref[...], preferred_element_type=F32)
    u = jnp.dot(x, wub_ref[...], preferred_element_type=F32)
    o_ref[...] = (_silu(g) * u).astype(o_ref.dtype)


def swiglu_up(x, wg, wu, layer, tn=512, tm=TM_DENSE):
    m, k = x.shape
    f = wg.shape[-1]
    wspec = pl.BlockSpec((None, k, tn), lambda j, i: (layer, 0, j))
    return pl.pallas_call(
        _swiglu_kernel,
        grid=(f // tn, m // tm),
        in_specs=[pl.BlockSpec((tm, k), lambda j, i: (i, 0)), wspec, wspec],
        out_specs=pl.BlockSpec((tm, tn), lambda j, i: (i, j)),
        out_shape=jax.ShapeDtypeStruct((m, f), BF16),
        scratch_shapes=[pltpu.VMEM((k, tn), BF16), pltpu.VMEM((k, tn), BF16)],
        compiler_params=_params(("arbitrary", "arbitrary"), 56),
        name="ffn_up",
    )(x, wg, wu)


CONV_ROWS = 128
CONV_BLOCKS_PER_SEQ = SEQ // CONV_ROWS
CONV_PROMPT_BLOCKS = M_PROMPT // CONV_ROWS


def _conv_kernel(h_ref, b_ref, c_ref, w_ref, buf_ref, bz_ref, stp_ref, sts_ref, carry_ref):
    i = pl.program_id(0)
    u = c_ref[...] * h_ref[...]
    w = w_ref[...]

    @pl.when(i < CONV_PROMPT_BLOCKS)
    def _():
        @pl.when(i % CONV_BLOCKS_PER_SEQ == 0)
        def _():
            carry_ref[...] = jnp.zeros_like(carry_ref)

        c2 = carry_ref[0:1, :]
        c1 = carry_ref[1:2, :]
        row = lax.broadcasted_iota(jnp.int32, u.shape, 0)
        u1 = jnp.where(row == 0, c1, pltpu.roll(u, 1, 0))
        u2 = jnp.where(row == 0, c2, jnp.where(row == 1, c1, pltpu.roll(u, 2, 0)))
        z = w[0:1, :] * u2
        z = z + w[1:2, :] * u1
        z = z + w[2:3, :] * u
        bz_ref[...] = (b_ref[...] * z).astype(bz_ref.dtype)
        tail = u[CONV_ROWS - 2:CONV_ROWS, :]
        carry_ref[0:2, :] = tail
        stp_ref[...] = tail

    @pl.when(i == CONV_PROMPT_BLOCKS)
    def _():
        b0 = buf_ref[:, :D]
        b1 = buf_ref[:, D:]
        z = w[0:1, :] * b0
        z = z + w[1:2, :] * b1
        z = z + w[2:3, :] * u
        bz_ref[...] = (b_ref[...] * z).astype(bz_ref.dtype)
        sts_ref[:, :D] = b1
        sts_ref[:, D:] = u


def conv_mix(p, conv_w, buf):
    last_seq = N_PROMPT_SEQ - 1
    return pl.pallas_call(
        _conv_kernel,
        grid=(M_TOK // CONV_ROWS,),
        in_specs=[pl.BlockSpec((CONV_ROWS, D), lambda i: (i, 0)),
                  pl.BlockSpec((CONV_ROWS, D), lambda i: (i, 1)),
                  pl.BlockSpec((CONV_ROWS, D), lambda i: (i, 2)),
                  pl.BlockSpec((3, D), lambda i: (0, 0)),
                  pl.BlockSpec((M_SAMPLE, 2 * D), lambda i: (0, 0))],
        out_specs=[pl.BlockSpec((CONV_ROWS, D), lambda i: (i, 0)),
                   pl.BlockSpec((None, 2, D),
                                lambda i: (jnp.minimum(i // CONV_BLOCKS_PER_SEQ, last_seq), 0, 0)),
                   pl.BlockSpec((M_SAMPLE, 2 * D), lambda i: (0, 0))],
        out_shape=[jax.ShapeDtypeStruct((M_TOK, D), BF16),
                   jax.ShapeDtypeStruct((N_PROMPT_SEQ, 2, D), F32),
                   jax.ShapeDtypeStruct((M_SAMPLE, 2 * D), F32)],
        scratch_shapes=[pltpu.VMEM((8, D), F32)],
        compiler_params=_params(("arbitrary",), 32),
        name="conv_mix",
    )(p, p, p, conv_w, buf)


def _log_sigmoid(x):
    return jnp.minimum(x, 0.0) - jnp.log1p(jnp.exp(-jnp.abs(x)))


def _gla_gate_kernel(h_ref, wa_ref, wg_ref, bg_ref, g_ref):
    a = jnp.dot(h_ref[...], wa_ref[...].astype(BF16), preferred_element_type=F32)
    lane = lax.broadcasted_iota(jnp.int32, a.shape, 1)
    a = jnp.where(lane < GATE_RANK, a, 0.0)
    z = jnp.dot(a.astype(BF16), wg_ref[...].astype(BF16), preferred_element_type=F32) + bg_ref[...]
    g_ref[...] = _log_sigmoid(z) * (1.0 / GATE_TAU)


def gla_gate(h, w_in, w_gate_pad, b_gate, tm=TM_DENSE):
    a_blk = (2 * HK + 2 * HV) // LANES
    return pl.pallas_call(
        _gla_gate_kernel,
        grid=(M_TOK // tm,),
        in_specs=[pl.BlockSpec((tm, D), lambda i: (i, 0)),
                  pl.BlockSpec((None, D, LANES), lambda i: (0, 0, a_blk)),
                  pl.BlockSpec((LANES, HK), lambda i: (0, 0)),
                  pl.BlockSpec((1, HK), lambda i: (0, 0))],
        out_specs=pl.BlockSpec((tm, HK), lambda i: (i, 0)),
        out_shape=jax.ShapeDtypeStruct((M_TOK, HK), F32),
        compiler_params=_params(("arbitrary",), 40),
        name="gla_gate",
    )(h, w_in, w_gate_pad, b_gate)


def _row_to_cols(row):
    return jnp.transpose(jnp.broadcast_to(row, (LANES, row.shape[1])))


def _split3_bf16(x):
    x1 = x.astype(BF16)
    r1 = x - x1.astype(F32)
    x2 = r1.astype(BF16)
    x3 = (r1 - x2.astype(F32)).astype(BF16)
    return x1, x2, x3


def _gla_prompt_kernel(o_init, q_ref, k_ref, v_ref, r_ref, g_ref, gn_ref, o_ref, sout_ref, s_ref):
    del o_init
    c = pl.program_id(1)

    @pl.when(c == 0)
    def _():
        s_ref[...] = jnp.zeros_like(s_ref)

    row = lax.broadcasted_iota(jnp.int32, (CHUNK, CHUNK), 0)
    col = lax.broadcasted_iota(jnp.int32, (CHUNK, CHUNK), 1)
    tri = row >= col
    trib = tri.astype(BF16)

    g1, g2, g3 = _split3_bf16(g_ref[...])
    b = (jnp.dot(trib, g1, preferred_element_type=F32)
         + jnp.dot(trib, g2, preferred_element_type=F32)
         + jnp.dot(trib, g3, preferred_element_type=F32))
    b_last = b[CHUNK - 1:CHUNK, :]
    q = q_ref[...] * (DK ** -0.5)
    k = k_ref[...]
    q_dec = (q * jnp.exp(b)).astype(BF16)
    k_inv = (k * jnp.exp(-b)).astype(BF16)
    k_end = (k * jnp.exp(b_last - b)).astype(BF16)
    decay = jnp.exp(b_last)
    gn = gn_ref[...]

    for h in range(HEADS):
        ks = slice(h * DK, (h + 1) * DK)
        vs = slice(h * DV, (h + 1) * DV)
        vb = v_ref[:, vs].astype(BF16)
        s_old = s_ref[h]
        scores = lax.dot_general(q_dec[:, ks], k_inv[:, ks], (((1,), (1,)), ((), ())),
                                 preferred_element_type=F32)
        scores = jnp.where(tri, scores, 0.0).astype(BF16)
        o = (jnp.dot(scores, vb, preferred_element_type=F32)
             + jnp.dot(q_dec[:, ks], s_old.astype(BF16), preferred_element_type=F32))
        kv = lax.dot_general(k_end[:, ks], vb, (((0,), (0,)), ((), ())),
                             preferred_element_type=F32)
        dcol = _row_to_cols(decay[:, ks])
        s_ref[h] = jnp.concatenate(
            [s_old[:, j * LANES:(j + 1) * LANES] * dcol for j in range(DV // LANES)], axis=1) + kv
        on = _rmsnorm(o, gn)
        o_ref[:, vs] = (_silu(r_ref[:, vs]) * on).astype(o_ref.dtype)

    @pl.when(c == N_CHUNKS - 1)
    def _():
        sout_ref[...] = s_ref[...]


def gla_prompt(p, g, g_norm):
    rows = lambda n, c: n * N_CHUNKS + c
    return pl.pallas_call(
        _gla_prompt_kernel,
        grid=(N_PROMPT_SEQ, N_CHUNKS),
        in_specs=[pl.BlockSpec(memory_space=pl.ANY),
                  pl.BlockSpec((CHUNK, HK), lambda n, c: (rows(n, c), 0)),
                  pl.BlockSpec((CHUNK, HK), lambda n, c: (rows(n, c), 1)),
                  pl.BlockSpec((CHUNK, HV), lambda n, c: (rows(n, c), 1)),
                  pl.BlockSpec((CHUNK, HV), lambda n, c: (rows(n, c), 2)),
                  pl.BlockSpec((CHUNK, HK), lambda n, c: (rows(n, c), 0)),
                  pl.BlockSpec((1, DV), lambda n, c: (0, 0))],
        out_specs=[pl.BlockSpec((CHUNK, HV), lambda n, c: (rows(n, c), 0)),
                   pl.BlockSpec((None, HEADS, DK, DV), lambda n, c: (n, 0, 0, 0))],
        out_shape=[jax.ShapeDtypeStruct((M_TOK, HV), BF16),
                   jax.ShapeDtypeStruct((N_PROMPT_SEQ, HEADS, DK, DV), F32)],
        scratch_shapes=[pltpu.VMEM((HEADS, DK, DV), F32)],
        input_output_aliases={0: 0},
        compiler_params=_params(("arbitrary", "arbitrary"), 40),
        name="gla_prompt",
    )(jnp.zeros((M_TOK, HV), BF16), p, p, p, p, g, g_norm)


GLA_SAMPLE_SEQS = 2


def _gla_sample_kernel(gated_any, q_ref, k_ref, v_ref, r_ref, g_ref, gn_ref, s_ref,
                       gated_ref, sout_ref, qt_ref, kt_ref, et_ref, o_scr):
    del gated_any
    i = pl.program_id(0)

    @pl.when(i == 0)
    def _():
        q = q_ref[...] * (DK ** -0.5)
        k = k_ref[...]
        e = jnp.exp(g_ref[...])
        for h in range(HEADS):
            ks = slice(h * DK, (h + 1) * DK)
            qt_ref[h] = jnp.transpose(q[:, ks])
            kt_ref[h] = jnp.transpose(k[:, ks])
            et_ref[h] = jnp.transpose(e[:, ks])

    lane = lax.broadcasted_iota(jnp.int32, (DK, M_SAMPLE), 1)
    for s in range(GLA_SAMPLE_SEQS):
        n = i * GLA_SAMPLE_SEQS + s
        pick = lane == n

        def column(t):
            return jnp.sum(jnp.where(pick, t, 0.0), axis=1, keepdims=True)

        for h in range(HEADS):
            vs = slice(h * DV, (h + 1) * DV)
            v_row = v_ref[pl.ds(n, 1), vs]
            s_new = s_ref[s, h] * column(et_ref[h]) + column(kt_ref[h]) * v_row
            sout_ref[s, h] = s_new
            o_scr[pl.ds(n, 1), vs] = jnp.sum(column(qt_ref[h]) * s_new, axis=0, keepdims=True)

    @pl.when(i == pl.num_programs(0) - 1)
    def _():
        gn = gn_ref[...]
        for h in range(HEADS):
            vs = slice(h * DV, (h + 1) * DV)
            on = _rmsnorm(o_scr[:, vs], gn)
            gated_ref[:, vs] = (_silu(r_ref[:, vs]) * on).astype(gated_ref.dtype)


def gla_sample(gated, p, g, g_norm, state):
    rb = M_PROMPT // M_SAMPLE
    bs = GLA_SAMPLE_SEQS
    return pl.pallas_call(
        _gla_sample_kernel,
        grid=(M_SAMPLE // bs,),
        in_specs=[pl.BlockSpec(memory_space=pl.ANY),
                  pl.BlockSpec((M_SAMPLE, HK), lambda i: (rb, 0)),
                  pl.BlockSpec((M_SAMPLE, HK), lambda i: (rb, 1)),
                  pl.BlockSpec((M_SAMPLE, HV), lambda i: (rb, 1)),
                  pl.BlockSpec((M_SAMPLE, HV), lambda i: (rb, 2)),
                  pl.BlockSpec((M_SAMPLE, HK), lambda i: (rb, 0)),
                  pl.BlockSpec((1, DV), lambda i: (0, 0)),
                  pl.BlockSpec((bs, HEADS, DK, DV), lambda i: (i, 0, 0, 0))],
        out_specs=[pl.BlockSpec((M_SAMPLE, HV), lambda i: (rb, 0)),
                   pl.BlockSpec((bs, HEADS, DK, DV), lambda i: (i, 0, 0, 0))],
        out_shape=[jax.ShapeDtypeStruct((M_TOK, HV), BF16),
                   jax.ShapeDtypeStruct((M_SAMPLE, HEADS, DK, DV), F32)],
        scratch_shapes=[pltpu.VMEM((HEADS, DK, M_SAMPLE), F32),
                        pltpu.VMEM((HEADS, DK, M_SAMPLE), F32),
                        pltpu.VMEM((HEADS, DK, M_SAMPLE), F32),
                        pltpu.VMEM((M_SAMPLE, HV), F32)],
        input_output_aliases={0: 0},
        compiler_params=_params(("arbitrary",), 40),
        name="gla_sample",
    )(gated, p, p, p, p, g, g_norm, state)


def _router_kernel(x_ref, g_ref, wr_ref, route_ref, idx_ref):
    h = _rmsnorm(x_ref[...], g_ref[...])
    logits = jnp.dot(h, wr_ref[...], preferred_element_type=F32, precision=lax.Precision.HIGHEST)
    lane = lax.broadcasted_iota(jnp.int32, logits.shape, 1)
    lane_f = lane.astype(F32)
    neg = jnp.float32(-jnp.inf)
    logits = jnp.where(lane < N_EXPERTS, logits, neg)
    m1 = jnp.max(logits, axis=1, keepdims=True)
    i1 = jnp.min(jnp.where(logits == m1, lane_f, float(LANES)), axis=1, keepdims=True)
    rest = jnp.where(lane_f == i1, neg, logits)
    m2 = jnp.max(rest, axis=1, keepdims=True)
    i2 = jnp.min(jnp.where(rest == m2, lane_f, float(LANES)), axis=1, keepdims=True)
    e2 = jnp.exp(m2 - m1)
    den = 1.0 + e2
    w1 = 1.0 / den
    w2 = e2 / den
    route_ref[...] = jnp.where(lane == 0, w1, jnp.where(lane == 1, w2, 0.0))
    idx_ref[...] = jnp.where(lane == 0, i1, jnp.where(lane == 1, i2, 0.0)).astype(jnp.int32)


def router(x, g, w_router_pad, tm=416):
    return pl.pallas_call(
        _router_kernel,
        grid=(M_TOK // tm,),
        in_specs=[pl.BlockSpec((tm, D), lambda i: (i, 0)),
                  pl.BlockSpec((1, D), lambda i: (0, 0)),
                  pl.BlockSpec((D, LANES), lambda i: (0, 0))],
        out_specs=[pl.BlockSpec((tm, LANES), lambda i: (i, 0)),
                   pl.BlockSpec((tm, LANES), lambda i: (i, 0))],
        out_shape=[jax.ShapeDtypeStruct((M_TOK, LANES), F32),
                   jax.ShapeDtypeStruct((M_TOK, LANES), jnp.int32)],
        compiler_params=_params(("arbitrary",), 40),
        name="router",
    )(x, g.reshape(1, D), w_router_pad)


def _gather_kernel(tok_ref, nvalid_ref, src_hbm, g_ref, o_ref, buf, sem):
    i = pl.program_id(0)
    nv = nvalid_ref[0]
    tm = buf.shape[1]
    slot = lax.rem(i, 2)

    def issue(tile, dst_slot):
        base = tile * tm

        def body(r, carry):
            t = tok_ref[base + r]
            pltpu.make_async_copy(src_hbm.at[pl.ds(t, 1)], buf.at[dst_slot, pl.ds(r, 1)],
                                  sem.at[dst_slot]).start()
            return carry

        lax.fori_loop(0, tm, body, 0, unroll=8)

    @pl.when(i == 0)
    def _():
        issue(0, 0)

    @pl.when(i + 1 < nv)
    def _():
        issue(i + 1, 1 - slot)

    @pl.when(i < nv)
    def _():
        pltpu.make_async_copy(src_hbm.at[pl.ds(0, tm)], buf.at[slot], sem.at[slot]).wait()
        o_ref[...] = _rmsnorm(buf[slot], g_ref[...]).astype(o_ref.dtype)

    @pl.when(i >= nv)
    def _():
        o_ref[...] = jnp.zeros_like(o_ref)


def gather_rows(row_token, n_valid_tiles, src, g, tm=GROUP_ROWS):
    return pl.pallas_call(
        _gather_kernel,
        grid_spec=pltpu.PrefetchScalarGridSpec(
            num_scalar_prefetch=2,
            grid=(R_PAD // tm,),
            in_specs=[pl.BlockSpec(memory_space=pl.ANY),
                      pl.BlockSpec((1, D), lambda i, tok, nv: (0, 0))],
            out_specs=pl.BlockSpec((tm, D), lambda i, tok, nv: (i, 0)),
            scratch_shapes=[pltpu.VMEM((2, tm, D), F32), pltpu.SemaphoreType.DMA((2,))]),
        out_shape=jax.ShapeDtypeStruct((R_PAD, D), BF16),
        compiler_params=_params(("arbitrary",), 32),
        name="moe_gather",
    )(row_token, n_valid_tiles, src, g.reshape(1, D))


def _grouped_kernel(tstart, ntiles, x_hbm, *refs, n_w, total_tiles):
    w_refs = refs[:n_w]
    o_hbm = refs[n_w]
    wb_refs = refs[n_w + 1:2 * n_w + 1]
    xbuf, obuf, xsem, osem = refs[2 * n_w + 1:]
    e = pl.program_id(0)
    j = pl.program_id(1)
    tm = xbuf.shape[1]
    tn = obuf.shape[2]
    nt = ntiles[e]
    t0 = tstart[e]
    col = pl.multiple_of(j * tn, tn)

    for w_ref, wb_ref in zip(w_refs, wb_refs):
        wb_ref[...] = w_ref[...].astype(BF16)

    def x_copy(tile, slot):
        row = pl.multiple_of(tile * tm, tm)
        return pltpu.make_async_copy(x_hbm.at[pl.ds(row, tm)], xbuf.at[slot], xsem.at[slot])

    def o_copy(tile, slot):
        row = pl.multiple_of(tile * tm, tm)
        return pltpu.make_async_copy(obuf.at[slot], o_hbm.at[pl.ds(row, tm), pl.ds(col, tn)],
                                     osem.at[slot])

    @pl.when(nt > 0)
    def _():
        x_copy(t0, 0).start()

    def body(t, carry):
        slot = lax.rem(t, 2)
        x_copy(t0 + t, slot).wait()

        @pl.when(t + 1 < nt)
        def _():
            x_copy(t0 + t + 1, 1 - slot).start()

        @pl.when(t >= 2)
        def _():
            o_copy(t0 + t - 2, slot).wait()

        x = xbuf[slot]
        if n_w == 2:
            g = jnp.dot(x, wb_refs[0][...], preferred_element_type=F32)
            u = jnp.dot(x, wb_refs[1][...], preferred_element_type=F32)
            obuf[slot] = (_silu(g) * u).astype(obuf.dtype)
        else:
            obuf[slot] = jnp.dot(x, wb_refs[0][...], preferred_element_type=F32).astype(obuf.dtype)
        o_copy(t0 + t, slot).start()
        return carry

    lax.fori_loop(0, nt, body, 0)

    @pl.when(nt >= 2)
    def _():
        o_copy(t0 + nt - 2, lax.rem(nt, 2)).wait()

    @pl.when(nt >= 1)
    def _():
        o_copy(t0 + nt - 1, lax.rem(nt + 1, 2)).wait()

    @pl.when(e == pl.num_programs(0) - 1)
    def _():
        obuf[0] = jnp.zeros(obuf.shape[1:], obuf.dtype)

        def zero_tile(tile, carry):
            cp = o_copy(tile, 0)
            cp.start()
            cp.wait()
            return carry

        lax.fori_loop(t0 + nt, total_tiles, zero_tile, 0)


def grouped_matmul(x, ws, tstart, ntiles, tm, tn, out_dtype, vmem_mb, name):
    k = x.shape[1]
    n = ws[0].shape[-1]
    n_w = len(ws)
    wspec = pl.BlockSpec((None, None, k, tn), lambda e, j, ts, nt: (0, e, 0, j))
    any_spec = pl.BlockSpec(memory_space=pl.ANY)
    return pl.pallas_call(
        functools.partial(_grouped_kernel, n_w=n_w, total_tiles=R_PAD // tm),
        grid_spec=pltpu.PrefetchScalarGridSpec(
            num_scalar_prefetch=2,
            grid=(N_EXPERTS, n // tn),
            in_specs=[any_spec] + [wspec] * n_w,
            out_specs=any_spec,
            scratch_shapes=([pltpu.VMEM((k, tn), BF16)] * n_w
                            + [pltpu.VMEM((2, tm, k), BF16), pltpu.VMEM((2, tm, tn), out_dtype),
                               pltpu.SemaphoreType.DMA((2,)), pltpu.SemaphoreType.DMA((2,))])),
        out_shape=jax.ShapeDtypeStruct((R_PAD, n), out_dtype),
        compiler_params=_params(("arbitrary", "arbitrary"), vmem_mb),
        name=name,
    )(tstart, ntiles, x, *ws)


COMBINE_ROWS = 128
COMBINE_PROMPT_STEPS = M_PROMPT // COMBINE_ROWS


def _combine_kernel(p1_ref, p2_ref, x_ref, route_ref, g_ref, y_hbm, op_ref, os_ref, b1, b2, sem):
    i = pl.program_id(0)
    tm = b1.shape[1]
    slot = lax.rem(i, 2)

    def issue(tile, dst_slot):
        base = tile * tm

        def body(r, carry):
            pltpu.make_async_copy(y_hbm.at[pl.ds(p1_ref[base + r], 1)], b1.at[dst_slot, pl.ds(r, 1)],
                                  sem.at[0, dst_slot]).start()
            pltpu.make_async_copy(y_hbm.at[pl.ds(p2_ref[base + r], 1)], b2.at[dst_slot, pl.ds(r, 1)],
                                  sem.at[1, dst_slot]).start()
            return carry

        lax.fori_loop(0, tm, body, 0, unroll=8)

    @pl.when(i == 0)
    def _():
        issue(0, 0)

    @pl.when(i + 1 < pl.num_programs(0))
    def _():
        issue(i + 1, 1 - slot)

    pltpu.make_async_copy(y_hbm.at[pl.ds(0, tm)], b1.at[slot], sem.at[0, slot]).wait()
    pltpu.make_async_copy(y_hbm.at[pl.ds(0, tm)], b2.at[slot], sem.at[1, slot]).wait()
    route = route_ref[...]
    w1 = route[:, 0:1]
    w2 = route[:, 1:2]
    x = x_ref[...] + (w1 * b1[slot] + w2 * b2[slot])
    out = _rmsnorm(x, g_ref[...])

    @pl.when(i < COMBINE_PROMPT_STEPS)
    def _():
        op_ref[...] = out

    @pl.when(i == COMBINE_PROMPT_STEPS)
    def _():
        os_ref[...] = out


def moe_combine(pos1, pos2, x, route, g_final, y):
    tm = COMBINE_ROWS
    last_prompt = COMBINE_PROMPT_STEPS - 1
    return pl.pallas_call(
        _combine_kernel,
        grid_spec=pltpu.PrefetchScalarGridSpec(
            num_scalar_prefetch=2,
            grid=(M_TOK // tm,),
            in_specs=[pl.BlockSpec((tm, D), lambda i, p1, p2: (i, 0)),
                      pl.BlockSpec((tm, LANES), lambda i, p1, p2: (i, 0)),
                      pl.BlockSpec((1, D), lambda i, p1, p2: (0, 0)),
                      pl.BlockSpec(memory_space=pl.ANY)],
            out_specs=[pl.BlockSpec((tm, D), lambda i, p1, p2: (jnp.minimum(i, last_prompt), 0)),
                       pl.BlockSpec((M_SAMPLE, D), lambda i, p1, p2: (0, 0))],
            scratch_shapes=[pltpu.VMEM((2, tm, D), F32), pltpu.VMEM((2, tm, D), F32),
                            pltpu.SemaphoreType.DMA((2, 2))]),
        out_shape=[jax.ShapeDtypeStruct((M_PROMPT, D), F32),
                   jax.ShapeDtypeStruct((M_SAMPLE, D), F32)],
        compiler_params=_params(("arbitrary",), 40),
        name="moe_combine",
    )(pos1, pos2, x, route, g_final.reshape(1, D), y)


def _group_tables(idx):
    e_flat = jnp.concatenate([idx[:, 0], idx[:, 1]])
    onehot = (e_flat[:, None] == jnp.arange(N_EXPERTS, dtype=jnp.int32)[None, :]).astype(jnp.int32)
    csum = jnp.cumsum(onehot, axis=0)
    counts = csum[-1]
    rank = jnp.sum(csum * onehot, axis=1) - 1
    ntiles = (counts + GROUP_ROWS - 1) // GROUP_ROWS
    tile_end = jnp.cumsum(ntiles)
    tstart = tile_end - ntiles
    dest = jnp.sum(onehot * (tstart * GROUP_ROWS)[None, :], axis=1) + rank
    token = jnp.concatenate([jnp.arange(M_TOK, dtype=jnp.int32)] * 2)
    row_token = jnp.zeros((R_PAD,), jnp.int32).at[dest].set(token)
    return row_token, tile_end[-1:], dest[:M_TOK], dest[M_TOK:], tstart, ntiles


def kernel(x_prompt, x_sample, state_conv, state_gla, norm_mix, norm_ffn, norm_final,
           conv_w_in, conv_w, conv_w_out, gla_w_in, gla_w_gate, gla_b_gate, gla_norm, gla_w_out,
           ffn_w_gate, ffn_w_up, ffn_w_down, moe_w_router, moe_w_gate, moe_w_up, moe_w_down):
    x0, h = stack_norm(x_prompt.reshape(M_PROMPT, D), x_sample.reshape(M_SAMPLE, D), norm_mix[0])
    p = linear(h, conv_w_in, 0, 3 * D, tn=1024, tm=TM_DENSE, name="conv_in")
    bz, conv_prompt_state, conv_sample_state = conv_mix(
        p, conv_w[0], state_conv[0].reshape(M_SAMPLE, 2 * D))
    conv_sample_state = conv_sample_state.reshape(1, M_SAMPLE, 2, D)
    x1 = linear(bz, conv_w_out, 0, D, tn=1024, tm=TM_DENSE, res=x0, name="conv_out")

    h = rmsnorm_bf16(x1, norm_ffn[0])
    a = swiglu_up(h, ffn_w_gate, ffn_w_up, 0)
    x2 = linear(a, ffn_w_down, 0, D, tn=512, tm=640, res=x1, name="ffn_down")

    h = rmsnorm_bf16(x2, norm_mix[1])
    p = linear(h, gla_w_in, 0, 2 * HK + 2 * HV, tn=1024, tm=TM_DENSE, name="gla_in")
    w_gate_pad = jnp.pad(gla_w_gate[0], ((0, LANES - GATE_RANK), (0, 0)))
    g = gla_gate(h, gla_w_in, w_gate_pad, gla_b_gate[0].reshape(1, HK))
    gn = gla_norm[0].reshape(1, DV)
    gated, gla_prompt_state = gla_prompt(p, g, gn)
    gated, gla_sample_state = gla_sample(gated, p, g, gn, state_gla[0])
    x3 = linear(gated, gla_w_out, 0, D, tn=1024, tm=TM_DENSE, res=x2, name="gla_out")

    w_router_pad = jnp.pad(moe_w_router[0], ((0, 0), (0, LANES - N_EXPERTS)))
    route, idx = router(x3, norm_ffn[1], w_router_pad)
    row_token, n_valid, pos1, pos2, tstart, ntiles = _group_tables(idx)
    xs = gather_rows(row_token, n_valid, x3, norm_ffn[1])
    act = grouped_matmul(xs, (moe_w_gate, moe_w_up), tstart, ntiles, tm=GROUP_ROWS, tn=512,
                         out_dtype=BF16, vmem_mb=48, name="moe_up")
    sub = GROUP_ROWS // TM_DOWN
    y = grouped_matmul(act, (moe_w_down,), tstart * sub, ntiles * sub, tm=TM_DOWN, tn=512,
                       out_dtype=F32, vmem_mb=56, name="moe_down")
    y_prompt, y_sample = moe_combine(pos1, pos2, x3, route, norm_final, y)

    y_prompt = y_prompt.reshape(N_PROMPT_SEQ, SEQ, D)
    y_sample = y_sample.reshape(M_SAMPLE, 1, D)
    return (y_prompt, y_sample,
            conv_prompt_state.reshape(1, N_PROMPT_SEQ, 2, D), conv_sample_state,
            gla_prompt_state.reshape(1, N_PROMPT_SEQ, HEADS, DK, DV),
            gla_sample_state.reshape(1, M_SAMPLE, HEADS, DK, DV))
```

```python
import functools

import jax
import jax.numpy as jnp
from jax import lax
from jax.experimental import pallas as pl
from jax.experimental.pallas import tpu as pltpu

F32 = jnp.float32
BF16 = jnp.bfloat16

D = 2048
N_PROMPT_SEQ = 4
SEQ = 2048
M_PROMPT = N_PROMPT_SEQ * SEQ
M_SAMPLE = 128
M_TOK = M_PROMPT + M_SAMPLE
HEADS = 4
DK = 256
DV = 512
HK = HEADS * DK
HV = HEADS * DV
GATE_RANK = 16
GATE_TAU = 16.0
CHUNK = 64
N_CHUNKS = SEQ // CHUNK
D_FF = 5632
N_EXPERTS = 8
D_FF_EXPERT = 7168
EPS = 1e-6
LANES = 128

TM_DENSE = 1040
GROUP_ROWS = 512
N_ASSIGN = 2 * M_TOK
N_GROUP_TILES = N_ASSIGN // GROUP_ROWS + N_EXPERTS
R_PAD = N_GROUP_TILES * GROUP_ROWS
TM_DOWN = 256


def _params(sem, vmem_mb):
    return pltpu.CompilerParams(dimension_semantics=sem,
                                vmem_limit_bytes=vmem_mb * 1024 * 1024)


def _rmsnorm(x, g):
    return x * lax.rsqrt(jnp.mean(x * x, axis=-1, keepdims=True) + EPS) * g


def _silu(x):
    return x * jax.nn.sigmoid(x)


def _norm_kernel(x_ref, g_ref, o_ref):
    o_ref[...] = _rmsnorm(x_ref[...], g_ref[...]).astype(o_ref.dtype)


def rmsnorm_bf16(x, g, tr=832):
    m = x.shape[0]
    return pl.pallas_call(
        _norm_kernel,
        grid=(m // tr,),
        in_specs=[pl.BlockSpec((tr, D), lambda i: (i, 0)),
                  pl.BlockSpec((1, D), lambda i: (0, 0))],
        out_specs=pl.BlockSpec((tr, D), lambda i: (i, 0)),
        out_shape=jax.ShapeDtypeStruct((m, D), BF16),
        compiler_params=_params(("arbitrary",), 40),
        name="rmsnorm",
    )(x, g.reshape(1, D))


STACK_ROWS = 512
STACK_PROMPT_STEPS = M_PROMPT // STACK_ROWS


def _stack_norm_kernel(xp_ref, xs_ref, g_ref, x_ref, h_ref):
    i = pl.program_id(0)
    g = g_ref[...]

    @pl.when(i < STACK_PROMPT_STEPS)
    def _():
        x = xp_ref[...]
        x_ref[...] = x
        h_ref[...] = _rmsnorm(x, g).astype(h_ref.dtype)

    @pl.when(i == STACK_PROMPT_STEPS)
    def _():
        x = xs_ref[...]
        x_ref[0:M_SAMPLE, :] = x
        h_ref[0:M_SAMPLE, :] = _rmsnorm(x, g).astype(h_ref.dtype)


def stack_norm(xp, xs, g):
    last_prompt = STACK_PROMPT_STEPS - 1
    return pl.pallas_call(
        _stack_norm_kernel,
        grid=(STACK_PROMPT_STEPS + 1,),
        in_specs=[pl.BlockSpec((STACK_ROWS, D), lambda i: (jnp.minimum(i, last_prompt), 0)),
                  pl.BlockSpec((M_SAMPLE, D), lambda i: (0, 0)),
                  pl.BlockSpec((1, D), lambda i: (0, 0))],
        out_specs=[pl.BlockSpec((STACK_ROWS, D), lambda i: (i, 0)),
                   pl.BlockSpec((STACK_ROWS, D), lambda i: (i, 0))],
        out_shape=[jax.ShapeDtypeStruct((M_TOK, D), F32),
                   jax.ShapeDtypeStruct((M_TOK, D), BF16)],
        compiler_params=_params(("arbitrary",), 40),
        name="stack_norm",
    )(xp, xs, g.reshape(1, D))


def _linear_kernel(*refs, has_res):
    if has_res:
        x_ref, w_ref, r_ref, o_ref, wb_ref = refs
    else:
        x_ref, w_ref, o_ref, wb_ref = refs

    @pl.when(pl.program_id(1) == 0)
    def _():
        wb_ref[...] = w_ref[...].astype(BF16)

    acc = jnp.dot(x_ref[...], wb_ref[...], preferred_element_type=F32)
    if has_res:
        acc = acc + r_ref[...]
    o_ref[...] = acc.astype(o_ref.dtype)


def linear(x, w, layer, n_out, tn, tm, res=None, out_dtype=F32, vmem_mb=56, name="linear"):
    m, k = x.shape
    in_specs = [pl.BlockSpec((tm, k), lambda j, i: (i, 0)),
                pl.BlockSpec((None, k, tn), lambda j, i: (layer, 0, j))]
    args = [x, w]
    if res is not None:
        in_specs.append(pl.BlockSpec((tm, tn), lambda j, i: (i, j)))
        args.append(res)
    return pl.pallas_call(
        functools.partial(_linear_kernel, has_res=res is not None),
        grid=(n_out // tn, m // tm),
        in_specs=in_specs,
        out_specs=pl.BlockSpec((tm, tn), lambda j, i: (i, j)),
        out_shape=jax.ShapeDtypeStruct((m, n_out), out_dtype),
        scratch_shapes=[pltpu.VMEM((k, tn), BF16)],
        compiler_params=_params(("arbitrary", "arbitrary"), vmem_mb),
        name=name,
    )(*args)


def _swiglu_kernel(x_ref, wg_ref, wu_ref, o_ref, wgb_ref, wub_ref):
    @pl.when(pl.program_id(1) == 0)
    def _():
        wgb_ref[...] = wg_ref[...].astype(BF16)
        wub_ref[...] = wu_ref[...].astype(BF16)

    x = x_ref[...]
    g = jnp.dot(x, wgb_ref[...], preferred_element_type=F32)
    u = jnp.dot(x, wub_ref[...], preferred_element_type=F32)
    o_ref[...] = (_silu(g) * u).astype(o_ref.dtype)


def swiglu_up(x, wg, wu, layer, tn=512, tm=TM_DENSE):
    m, k = x.shape
    f = wg.shape[-1]
    wspec = pl.BlockSpec((None, k, tn), lambda j, i: (layer, 0, j))
    return pl.pallas_call(
        _swiglu_kernel,
        grid=(f // tn, m // tm),
        in_specs=[pl.BlockSpec((tm, k), lambda j, i: (i, 0)), wspec, wspec],
        out_specs=pl.BlockSpec((tm, tn), lambda j, i: (i, j)),
        out_shape=jax.ShapeDtypeStruct((m, f), BF16),
        scratch_shapes=[pltpu.VMEM((k, tn), BF16), pltpu.VMEM((k, tn), BF16)],
        compiler_params=_params(("arbitrary", "arbitrary"), 56),
        name="ffn_up",
    )(x, wg, wu)


CONV_ROWS = 128
CONV_BLOCKS_PER_SEQ = SEQ // CONV_ROWS
CONV_PROMPT_BLOCKS = M_PROMPT // CONV_ROWS


def _conv_kernel(h_ref, b_ref, c_ref, w_ref, buf_ref, bz_ref, stp_ref, sts_ref, carry_ref):
    i = pl.program_id(0)
    u = c_ref[...] * h_ref[...]
    w = w_ref[...]

    @pl.when(i < CONV_PROMPT_BLOCKS)
    def _():
        @pl.when(i % CONV_BLOCKS_PER_SEQ == 0)
        def _():
            carry_ref[...] = jnp.zeros_like(carry_ref)

        c2 = carry_ref[0:1, :]
        c1 = carry_ref[1:2, :]
        row = lax.broadcasted_iota(jnp.int32, u.shape, 0)
        u1 = jnp.where(row == 0, c1, pltpu.roll(u, 1, 0))
        u2 = jnp.where(row == 0, c2, jnp.where(row == 1, c1, pltpu.roll(u, 2, 0)))
        z = w[0:1, :] * u2
        z = z + w[1:2, :] * u1
        z = z + w[2:3, :] * u
        bz_ref[...] = (b_ref[...] * z).astype(bz_ref.dtype)
        tail = u[CONV_ROWS - 2:CONV_ROWS, :]
        carry_ref[0:2, :] = tail
        stp_ref[...] = tail

    @pl.when(i == CONV_PROMPT_BLOCKS)
    def _():
        b0 = buf_ref[:, :D]
        b1 = buf_ref[:, D:]
        z = w[0:1, :] * b0
        z = z + w[1:2, :] * b1
        z = z + w[2:3, :] * u
        bz_ref[...] = (b_ref[...] * z).astype(bz_ref.dtype)
        sts_ref[:, :D] = b1
        sts_ref[:, D:] = u


def conv_mix(p, conv_w, buf):
    last_seq = N_PROMPT_SEQ - 1
    return pl.pallas_call(
        _conv_kernel,
        grid=(M_TOK // CONV_ROWS,),
        in_specs=[pl.BlockSpec((CONV_ROWS, D), lambda i: (i, 0)),
                  pl.BlockSpec((CONV_ROWS, D), lambda i: (i, 1)),
                  pl.BlockSpec((CONV_ROWS, D), lambda i: (i, 2)),
                  pl.BlockSpec((3, D), lambda i: (0, 0)),
                  pl.BlockSpec((M_SAMPLE, 2 * D), lambda i: (0, 0))],
        out_specs=[pl.BlockSpec((CONV_ROWS, D), lambda i: (i, 0)),
                   pl.BlockSpec((None, 2, D),
                                lambda i: (jnp.minimum(i // CONV_BLOCKS_PER_SEQ, last_seq), 0, 0)),
                   pl.BlockSpec((M_SAMPLE, 2 * D), lambda i: (0, 0))],
        out_shape=[jax.ShapeDtypeStruct((M_TOK, D), BF16),
                   jax.ShapeDtypeStruct((N_PROMPT_SEQ, 2, D), F32),
                   jax.ShapeDtypeStruct((M_SAMPLE, 2 * D), F32)],
        scratch_shapes=[pltpu.VMEM((8, D), F32)],
        compiler_params=_params(("arbitrary",), 32),
        name="conv_mix",
    )(p, p, p, conv_w, buf)


def _log_sigmoid(x):
    return jnp.minimum(x, 0.0) - jnp.log1p(jnp.exp(-jnp.abs(x)))


def _gla_gate_kernel(h_ref, wa_ref, wg_ref, bg_ref, g_ref):
    a = jnp.dot(h_ref[...], wa_ref[...].astype(BF16), preferred_element_type=F32)
    lane = lax.broadcasted_iota(jnp.int32, a.shape, 1)
    a = jnp.where(lane < GATE_RANK, a, 0.0)
    z = jnp.dot(a.astype(BF16), wg_ref[...].astype(BF16), preferred_element_type=F32) + bg_ref[...]
    g_ref[...] = _log_sigmoid(z) * (1.0 / GATE_TAU)


def gla_gate(h, w_in, w_gate_pad, b_gate, tm=TM_DENSE):
    a_blk = (2 * HK + 2 * HV) // LANES
    return pl.pallas_call(
        _gla_gate_kernel,
        grid=(M_TOK // tm,),
        in_specs=[pl.BlockSpec((tm, D), lambda i: (i, 0)),
                  pl.BlockSpec((None, D, LANES), lambda i: (0, 0, a_blk)),
                  pl.BlockSpec((LANES, HK), lambda i: (0, 0)),
                  pl.BlockSpec((1, HK), lambda i: (0, 0))],
        out_specs=pl.BlockSpec((tm, HK), lambda i: (i, 0)),
        out_shape=jax.ShapeDtypeStruct((M_TOK, HK), F32),
        compiler_params=_params(("arbitrary",), 40),
        name="gla_gate",
    )(h, w_in, w_gate_pad, b_gate)


def _row_to_cols(row):
    return jnp.transpose(jnp.broadcast_to(row, (LANES, row.shape[1])))


def _split3_bf16(x):
    x1 = x.astype(BF16)
    r1 = x - x1.astype(F32)
    x2 = r1.astype(BF16)
    x3 = (r1 - x2.astype(F32)).astype(BF16)
    return x1, x2, x3


def _gla_prompt_kernel(o_init, q_ref, k_ref, v_ref, r_ref, g_ref, gn_ref, o_ref, sout_ref, s_ref):
    del o_init
    c = pl.program_id(1)

    @pl.when(c == 0)
    def _():
        s_ref[...] = jnp.zeros_like(s_ref)

    row = lax.broadcasted_iota(jnp.int32, (CHUNK, CHUNK), 0)
    col = lax.broadcasted_iota(jnp.int32, (CHUNK, CHUNK), 1)
    tri = row >= col
    trib = tri.astype(BF16)

    g1, g2, g3 = _split3_bf16(g_ref[...])
    b = (jnp.dot(trib, g1, preferred_element_type=F32)
         + jnp.dot(trib, g2, preferred_element_type=F32)
         + jnp.dot(trib, g3, preferred_element_type=F32))
    b_last = b[CHUNK - 1:CHUNK, :]
    q = q_ref[...] * (DK ** -0.5)
    k = k_ref[...]
    q_dec = (q * jnp.exp(b)).astype(BF16)
    k_inv = (k * jnp.exp(-b)).astype(BF16)
    k_end = (k * jnp.exp(b_last - b)).astype(BF16)
    decay = jnp.exp(b_last)
    gn = gn_ref[...]

    for h in range(HEADS):
        ks = slice(h * DK, (h + 1) * DK)
        vs = slice(h * DV, (h + 1) * DV)
        vb = v_ref[:, vs].astype(BF16)
        s_old = s_ref[h]
        scores = lax.dot_general(q_dec[:, ks], k_inv[:, ks], (((1,), (1,)), ((), ())),
                                 preferred_element_type=F32)
        scores = jnp.where(tri, scores, 0.0).astype(BF16)
        o = (jnp.dot(scores, vb, preferred_element_type=F32)
             + jnp.dot(q_dec[:, ks], s_old.astype(BF16), preferred_element_type=F32))
        kv = lax.dot_general(k_end[:, ks], vb, (((0,), (0,)), ((), ())),
                             preferred_element_type=F32)
        dcol = _row_to_cols(decay[:, ks])
        s_ref[h] = jnp.concatenate(
            [s_old[:, j * LANES:(j + 1) * LANES] * dcol for j in range(DV // LANES)], axis=1) + kv
        on = _rmsnorm(o, gn)
        o_ref[:, vs] = (_silu(r_ref[:, vs]) * on).astype(o_ref.dtype)

    @pl.when(c == N_CHUNKS - 1)
    def _():
        sout_ref[...] = s_ref[...]


def gla_prompt(p, g, g_norm):
    rows = lambda n, c: n * N_CHUNKS + c
    return pl.pallas_call(
        _gla_prompt_kernel,
        grid=(N_PROMPT_SEQ, N_CHUNKS),
        in_specs=[pl.BlockSpec(memory_space=pl.ANY),
                  pl.BlockSpec((CHUNK, HK), lambda n, c: (rows(n, c), 0)),
                  pl.BlockSpec((CHUNK, HK), lambda n, c: (rows(n, c), 1)),
                  pl.BlockSpec((CHUNK, HV), lambda n, c: (rows(n, c), 1)),
                  pl.BlockSpec((CHUNK, HV), lambda n, c: (rows(n, c), 2)),
                  pl.BlockSpec((CHUNK, HK), lambda n, c: (rows(n, c), 0)),
                  pl.BlockSpec((1, DV), lambda n, c: (0, 0))],
        out_specs=[pl.BlockSpec((CHUNK, HV), lambda n, c: (rows(n, c), 0)),
                   pl.BlockSpec((None, HEADS, DK, DV), lambda n, c: (n, 0, 0, 0))],
        out_shape=[jax.ShapeDtypeStruct((M_TOK, HV), BF16),
                   jax.ShapeDtypeStruct((N_PROMPT_SEQ, HEADS, DK, DV), F32)],
        scratch_shapes=[pltpu.VMEM((HEADS, DK, DV), F32)],
        input_output_aliases={0: 0},
        compiler_params=_params(("arbitrary", "arbitrary"), 40),
        name="gla_prompt",
    )(jnp.zeros((M_TOK, HV), BF16), p, p, p, p, g, g_norm)


GLA_SAMPLE_SEQS = 2


def _gla_sample_kernel(gated_any, q_ref, k_ref, v_ref, r_ref, g_ref, gn_ref, s_ref,
                       gated_ref, sout_ref, qt_ref, kt_ref, et_ref, o_scr):
    del gated_any
    i = pl.program_id(0)

    @pl.when(i == 0)
    def _():
        q = q_ref[...] * (DK ** -0.5)
        k = k_ref[...]
        e = jnp.exp(g_ref[...])
        for h in range(HEADS):
            ks = slice(h * DK, (h + 1) * DK)
            qt_ref[h] = jnp.transpose(q[:, ks])
            kt_ref[h] = jnp.transpose(k[:, ks])
            et_ref[h] = jnp.transpose(e[:, ks])

    lane = lax.broadcasted_iota(jnp.int32, (DK, M_SAMPLE), 1)
    for s in range(GLA_SAMPLE_SEQS):
        n = i * GLA_SAMPLE_SEQS + s
        pick = lane == n

        def column(t):
            return jnp.sum(jnp.where(pick, t, 0.0), axis=1, keepdims=True)

        for h in range(HEADS):
            vs = slice(h * DV, (h + 1) * DV)
            v_row = v_ref[pl.ds(n, 1), vs]
            s_new = s_ref[s, h] * column(et_ref[h]) + column(kt_ref[h]) * v_row
            sout_ref[s, h] = s_new
            o_scr[pl.ds(n, 1), vs] = jnp.sum(column(qt_ref[h]) * s_new, axis=0, keepdims=True)

    @pl.when(i == pl.num_programs(0) - 1)
    def _():
        gn = gn_ref[...]
        for h in range(HEADS):
            vs = slice(h * DV, (h + 1) * DV)
            on = _rmsnorm(o_scr[:, vs], gn)
            gated_ref[:, vs] = (_silu(r_ref[:, vs]) * on).astype(gated_ref.dtype)


def gla_sample(gated, p, g, g_norm, state):
    rb = M_PROMPT // M_SAMPLE
    bs = GLA_SAMPLE_SEQS
    return pl.pallas_call(
        _gla_sample_kernel,
        grid=(M_SAMPLE // bs,),
        in_specs=[pl.BlockSpec(memory_space=pl.ANY),
                  pl.BlockSpec((M_SAMPLE, HK), lambda i: (rb, 0)),
                  pl.BlockSpec((M_SAMPLE, HK), lambda i: (rb, 1)),
                  pl.BlockSpec((M_SAMPLE, HV), lambda i: (rb, 1)),
                  pl.BlockSpec((M_SAMPLE, HV), lambda i: (rb, 2)),
                  pl.BlockSpec((M_SAMPLE, HK), lambda i: (rb, 0)),
                  pl.BlockSpec((1, DV), lambda i: (0, 0)),
                  pl.BlockSpec((bs, HEADS, DK, DV), lambda i: (i, 0, 0, 0))],
        out_specs=[pl.BlockSpec((M_SAMPLE, HV), lambda i: (rb, 0)),
                   pl.BlockSpec((bs, HEADS, DK, DV), lambda i: (i, 0, 0, 0))],
        out_shape=[jax.ShapeDtypeStruct((M_TOK, HV), BF16),
                   jax.ShapeDtypeStruct((M_SAMPLE, HEADS, DK, DV), F32)],
        scratch_shapes=[pltpu.VMEM((HEADS, DK, M_SAMPLE), F32),
                        pltpu.VMEM((HEADS, DK, M_SAMPLE), F32),
                        pltpu.VMEM((HEADS, DK, M_SAMPLE), F32),
                        pltpu.VMEM((M_SAMPLE, HV), F32)],
        input_output_aliases={0: 0},
        compiler_params=_params(("arbitrary",), 40),
        name="gla_sample",
    )(gated, p, p, p, p, g, g_norm, state)


def _router_kernel(x_ref, g_ref, wr_ref, route_ref, idx_ref):
    h = _rmsnorm(x_ref[...], g_ref[...])
    logits = jnp.dot(h, wr_ref[...], preferred_element_type=F32, precision=lax.Precision.HIGHEST)
    lane = lax.broadcasted_iota(jnp.int32, logits.shape, 1)
    lane_f = lane.astype(F32)
    neg = jnp.float32(-jnp.inf)
    logits = jnp.where(lane < N_EXPERTS, logits, neg)
    m1 = jnp.max(logits, axis=1, keepdims=True)
    i1 = jnp.min(jnp.where(logits == m1, lane_f, float(LANES)), axis=1, keepdims=True)
    rest = jnp.where(lane_f == i1, neg, logits)
    m2 = jnp.max(rest, axis=1, keepdims=True)
    i2 = jnp.min(jnp.where(rest == m2, lane_f, float(LANES)), axis=1, keepdims=True)
    e2 = jnp.exp(m2 - m1)
    den = 1.0 + e2
    w1 = 1.0 / den
    w2 = e2 / den
    route_ref[...] = jnp.where(lane == 0, w1, jnp.where(lane == 1, w2, 0.0))
    idx_ref[...] = jnp.where(lane == 0, i1, jnp.where(lane == 1, i2, 0.0)).astype(jnp.int32)


def router(x, g, w_router_pad, tm=416):
    return pl.pallas_call(
        _router_kernel,
        grid=(M_TOK // tm,),
        in_specs=[pl.BlockSpec((tm, D), lambda i: (i, 0)),
                  pl.BlockSpec((1, D), lambda i: (0, 0)),
                  pl.BlockSpec((D, LANES), lambda i: (0, 0))],
        out_specs=[pl.BlockSpec((tm, LANES), lambda i: (i, 0)),
                   pl.BlockSpec((tm, LANES), lambda i: (i, 0))],
        out_shape=[jax.ShapeDtypeStruct((M_TOK, LANES), F32),
                   jax.ShapeDtypeStruct((M_TOK, LANES), jnp.int32)],
        compiler_params=_params(("arbitrary",), 40),
        name="router",
    )(x, g.reshape(1, D), w_router_pad)


def _gather_kernel(tok_ref, nvalid_ref, src_hbm, g_ref, o_ref, buf, sem):
    i = pl.program_id(0)
    nv = nvalid_ref[0]
    tm = buf.shape[1]
    slot = lax.rem(i, 2)

    def issue(tile, dst_slot):
        base = tile * tm

        def body(r, carry):
            t = tok_ref[base + r]
            pltpu.make_async_copy(src_hbm.at[pl.ds(t, 1)], buf.at[dst_slot, pl.ds(r, 1)],
                                  sem.at[dst_slot]).start()
            return carry

        lax.fori_loop(0, tm, body, 0, unroll=8)

    @pl.when(i == 0)
    def _():
        issue(0, 0)

    @pl.when(i + 1 < nv)
    def _():
        issue(i + 1, 1 - slot)

    @pl.when(i < nv)
    def _():
        pltpu.make_async_copy(src_hbm.at[pl.ds(0, tm)], buf.at[slot], sem.at[slot]).wait()
        o_ref[...] = _rmsnorm(buf[slot], g_ref[...]).astype(o_ref.dtype)

    @pl.when(i >= nv)
    def _():
        o_ref[...] = jnp.zeros_like(o_ref)


def gather_rows(row_token, n_valid_tiles, src, g, tm=GROUP_ROWS):
    return pl.pallas_call(
        _gather_kernel,
        grid_spec=pltpu.PrefetchScalarGridSpec(
            num_scalar_prefetch=2,
            grid=(R_PAD // tm,),
            in_specs=[pl.BlockSpec(memory_space=pl.ANY),
                      pl.BlockSpec((1, D), lambda i, tok, nv: (0, 0))],
            out_specs=pl.BlockSpec((tm, D), lambda i, tok, nv: (i, 0)),
            scratch_shapes=[pltpu.VMEM((2, tm, D), F32), pltpu.SemaphoreType.DMA((2,))]),
        out_shape=jax.ShapeDtypeStruct((R_PAD, D), BF16),
        compiler_params=_params(("arbitrary",), 32),
        name="moe_gather",
    )(row_token, n_valid_tiles, src, g.reshape(1, D))


def _grouped_kernel(tstart, ntiles, x_hbm, *refs, n_w, n_col, total_tiles):
    w_hbm = refs[:n_w]
    o_hbm = refs[n_w]
    wbf, stage, xbuf, obuf, xsem, osem, wsem = refs[n_w + 1:]
    s = pl.program_id(0)
    n_items = pl.num_programs(0)
    e = s // n_col
    j = lax.rem(s, n_col)
    p = lax.rem(s, 2)
    tm = xbuf.shape[1]
    tn = obuf.shape[2]
    ck = stage.shape[2]
    n_chunks = wbf.shape[2] // ck
    nt = ntiles[e]
    t0 = tstart[e]
    col = pl.multiple_of(j * tn, tn)
    has_next = s + 1 < n_items

    def w_copies(item, c, q):
        row = pl.multiple_of(c * ck, ck)
        wcol = pl.multiple_of(lax.rem(item, n_col) * tn, tn)
        return [pltpu.make_async_copy(w.at[0, item // n_col, pl.ds(row, ck), pl.ds(wcol, tn)],
                                      stage.at[q, i], wsem.at[q, i])
                for i, w in enumerate(w_hbm)]

    def w_start(item, c, q):
        for cp in w_copies(item, c, q):
            cp.start()

    def w_chunk(item, c, dst):
        q = lax.rem(c, 2)

        @pl.when(c + 1 < n_chunks)
        def _():
            w_start(item, c + 1, 1 - q)

        row = pl.multiple_of(c * ck, ck)
        for i, cp in enumerate(w_copies(item, c, q)):
            cp.wait()
            wbf[dst, i, pl.ds(row, ck), :] = stage[q, i].astype(BF16)

    @pl.when(s == 0)
    def _():
        w_start(0, 0, 0)

        def first(c, carry):
            w_chunk(0, c, 0)
            return carry

        lax.fori_loop(0, n_chunks, first, 0)

    @pl.when(has_next)
    def _():
        w_start(s + 1, 0, 0)

    def x_copy(tile, slot):
        row = pl.multiple_of(tile * tm, tm)
        return pltpu.make_async_copy(x_hbm.at[pl.ds(row, tm)], xbuf.at[slot], xsem.at[slot])

    def o_copy(tile, slot):
        row = pl.multiple_of(tile * tm, tm)
        return pltpu.make_async_copy(obuf.at[slot], o_hbm.at[pl.ds(row, tm), pl.ds(col, tn)],
                                     osem.at[slot])

    @pl.when(nt > 0)
    def _():
        x_copy(t0, 0).start()

    def body(t, chunks_done):
        slot = lax.rem(t, 2)
        x_copy(t0 + t, slot).wait()

        @pl.when(t + 1 < nt)
        def _():
            x_copy(t0 + t + 1, 1 - slot).start()

        @pl.when(t >= 2)
        def _():
            o_copy(t0 + t - 2, slot).wait()

        x = xbuf[slot]
        if n_w == 2:
            g = jnp.dot(x, wbf[p, 0], preferred_element_type=F32)
            u = jnp.dot(x, wbf[p, 1], preferred_element_type=F32)
            obuf[slot] = (_silu(g) * u).astype(obuf.dtype)
        else:
            obuf[slot] = jnp.dot(x, wbf[p, 0], preferred_element_type=F32).astype(obuf.dtype)
        o_copy(t0 + t, slot).start()

        stream = has_next & (chunks_done < n_chunks)

        @pl.when(stream)
        def _():
            w_chunk(s + 1, chunks_done, 1 - p)

        return chunks_done + stream.astype(jnp.int32)

    chunks_done = lax.fori_loop(0, nt, body, jnp.int32(0))

    @pl.when(has_next)
    def _():
        def rest(c, carry):
            w_chunk(s + 1, c, 1 - p)
            return carry

        lax.fori_loop(chunks_done, n_chunks, rest, 0)

    @pl.when(nt >= 2)
    def _():
        o_copy(t0 + nt - 2, lax.rem(nt, 2)).wait()

    @pl.when(nt >= 1)
    def _():
        o_copy(t0 + nt - 1, lax.rem(nt + 1, 2)).wait()

    @pl.when(e == N_EXPERTS - 1)
    def _():
        obuf[0] = jnp.zeros(obuf.shape[1:], obuf.dtype)

        def zero_tile(tile, carry):
            cp = o_copy(tile, 0)
            cp.start()
            cp.wait()
            return carry

        lax.fori_loop(t0 + nt, total_tiles, zero_tile, 0)


def grouped_matmul(x, ws, tstart, ntiles, tm, tn, ck, out_dtype, vmem_mb, name):
    k = x.shape[1]
    n = ws[0].shape[-1]
    n_w = len(ws)
    n_col = n // tn
    any_spec = pl.BlockSpec(memory_space=pl.ANY)
    return pl.pallas_call(
        functools.partial(_grouped_kernel, n_w=n_w, n_col=n_col, total_tiles=R_PAD // tm),
        grid_spec=pltpu.PrefetchScalarGridSpec(
            num_scalar_prefetch=2,
            grid=(N_EXPERTS * n_col,),
            in_specs=[any_spec] * (1 + n_w),
            out_specs=any_spec,
            scratch_shapes=[pltpu.VMEM((2, n_w, k, tn), BF16), pltpu.VMEM((2, n_w, ck, tn), F32),
                            pltpu.VMEM((2, tm, k), BF16), pltpu.VMEM((2, tm, tn), out_dtype),
                            pltpu.SemaphoreType.DMA((2,)), pltpu.SemaphoreType.DMA((2,)),
                            pltpu.SemaphoreType.DMA((2, n_w))]),
        out_shape=jax.ShapeDtypeStruct((R_PAD, n), out_dtype),
        compiler_params=_params(("arbitrary",), vmem_mb),
        name=name,
    )(tstart, ntiles, x, *ws)


COMBINE_ROWS = 128
COMBINE_PROMPT_STEPS = M_PROMPT // COMBINE_ROWS


def _combine_kernel(p1_ref, p2_ref, x_ref, route_ref, g_ref, y_hbm, op_ref, os_ref, b1, b2, sem):
    i = pl.program_id(0)
    tm = b1.shape[1]
    slot = lax.rem(i, 2)

    def issue(tile, dst_slot):
        base = tile * tm

        def body(r, carry):
            pltpu.make_async_copy(y_hbm.at[pl.ds(p1_ref[base + r], 1)], b1.at[dst_slot, pl.ds(r, 1)],
                                  sem.at[0, dst_slot]).start()
            pltpu.make_async_copy(y_hbm.at[pl.ds(p2_ref[base + r], 1)], b2.at[dst_slot, pl.ds(r, 1)],
                                  sem.at[1, dst_slot]).start()
            return carry

        lax.fori_loop(0, tm, body, 0, unroll=8)

    @pl.when(i == 0)
    def _():
        issue(0, 0)

    @pl.when(i + 1 < pl.num_programs(0))
    def _():
        issue(i + 1, 1 - slot)

    pltpu.make_async_copy(y_hbm.at[pl.ds(0, tm)], b1.at[slot], sem.at[0, slot]).wait()
    pltpu.make_async_copy(y_hbm.at[pl.ds(0, tm)], b2.at[slot], sem.at[1, slot]).wait()
    route = route_ref[...]
    w1 = route[:, 0:1]
    w2 = route[:, 1:2]
    x = x_ref[...] + (w1 * b1[slot] + w2 * b2[slot])
    out = _rmsnorm(x, g_ref[...])

    @pl.when(i < COMBINE_PROMPT_STEPS)
    def _():
        op_ref[...] = out

    @pl.when(i == COMBINE_PROMPT_STEPS)
    def _():
        os_ref[...] = out


def moe_combine(pos1, pos2, x, route, g_final, y):
    tm = COMBINE_ROWS
    last_prompt = COMBINE_PROMPT_STEPS - 1
    return pl.pallas_call(
        _combine_kernel,
        grid_spec=pltpu.PrefetchScalarGridSpec(
            num_scalar_prefetch=2,
            grid=(M_TOK // tm,),
            in_specs=[pl.BlockSpec((tm, D), lambda i, p1, p2: (i, 0)),
                      pl.BlockSpec((tm, LANES), lambda i, p1, p2: (i, 0)),
                      pl.BlockSpec((1, D), lambda i, p1, p2: (0, 0)),
                      pl.BlockSpec(memory_space=pl.ANY)],
            out_specs=[pl.BlockSpec((tm, D), lambda i, p1, p2: (jnp.minimum(i, last_prompt), 0)),
                       pl.BlockSpec((M_SAMPLE, D), lambda i, p1, p2: (0, 0))],
            scratch_shapes=[pltpu.VMEM((2, tm, D), F32), pltpu.VMEM((2, tm, D), F32),
                            pltpu.SemaphoreType.DMA((2, 2))]),
        out_shape=[jax.ShapeDtypeStruct((M_PROMPT, D), F32),
                   jax.ShapeDtypeStruct((M_SAMPLE, D), F32)],
        compiler_params=_params(("arbitrary",), 40),
        name="moe_combine",
    )(pos1, pos2, x, route, g_final.reshape(1, D), y)


def _group_tables(idx):
    e_flat = jnp.concatenate([idx[:, 0], idx[:, 1]])
    onehot = (e_flat[:, None] == jnp.arange(N_EXPERTS, dtype=jnp.int32)[None, :]).astype(jnp.int32)
    csum = jnp.cumsum(onehot, axis=0)
    counts = csum[-1]
    rank = jnp.sum(csum * onehot, axis=1) - 1
    ntiles = (counts + GROUP_ROWS - 1) // GROUP_ROWS
    tile_end = jnp.cumsum(ntiles)
    tstart = tile_end - ntiles
    dest = jnp.sum(onehot * (tstart * GROUP_ROWS)[None, :], axis=1) + rank
    token = jnp.concatenate([jnp.arange(M_TOK, dtype=jnp.int32)] * 2)
    row_token = jnp.zeros((R_PAD,), jnp.int32).at[dest].set(token)
    return row_token, tile_end[-1:], dest[:M_TOK], dest[M_TOK:], tstart, ntiles


def kernel(x_prompt, x_sample, state_conv, state_gla, norm_mix, norm_ffn, norm_final,
           conv_w_in, conv_w, conv_w_out, gla_w_in, gla_w_gate, gla_b_gate, gla_norm, gla_w_out,
           ffn_w_gate, ffn_w_up, ffn_w_down, moe_w_router, moe_w_gate, moe_w_up, moe_w_down):
    x0, h = stack_norm(x_prompt.reshape(M_PROMPT, D), x_sample.reshape(M_SAMPLE, D), norm_mix[0])
    p = linear(h, conv_w_in, 0, 3 * D, tn=1024, tm=TM_DENSE, name="conv_in")
    bz, conv_prompt_state, conv_sample_state = conv_mix(
        p, conv_w[0], state_conv[0].reshape(M_SAMPLE, 2 * D))
    conv_sample_state = conv_sample_state.reshape(1, M_SAMPLE, 2, D)
    x1 = linear(bz, conv_w_out, 0, D, tn=1024, tm=TM_DENSE, res=x0, name="conv_out")

    h = rmsnorm_bf16(x1, norm_ffn[0])
    a = swiglu_up(h, ffn_w_gate, ffn_w_up, 0)
    x2 = linear(a, ffn_w_down, 0, D, tn=512, tm=640, res=x1, name="ffn_down")

    h = rmsnorm_bf16(x2, norm_mix[1])
    p = linear(h, gla_w_in, 0, 2 * HK + 2 * HV, tn=1024, tm=TM_DENSE, name="gla_in")
    w_gate_pad = jnp.pad(gla_w_gate[0], ((0, LANES - GATE_RANK), (0, 0)))
    g = gla_gate(h, gla_w_in, w_gate_pad, gla_b_gate[0].reshape(1, HK))
    gn = gla_norm[0].reshape(1, DV)
    gated, gla_prompt_state = gla_prompt(p, g, gn)
    gated, gla_sample_state = gla_sample(gated, p, g, gn, state_gla[0])
    x3 = linear(gated, gla_w_out, 0, D, tn=1024, tm=TM_DENSE, res=x2, name="gla_out")

    w_router_pad = jnp.pad(moe_w_router[0], ((0, 0), (0, LANES - N_EXPERTS)))
    route, idx = router(x3, norm_ffn[1], w_router_pad)
    row_token, n_valid, pos1, pos2, tstart, ntiles = _group_tables(idx)
    xs = gather_rows(row_token, n_valid, x3, norm_ffn[1])
    act = grouped_matmul(xs, (moe_w_gate, moe_w_up), tstart, ntiles, tm=GROUP_ROWS, tn=1024, ck=512,
                         out_dtype=BF16, vmem_mb=48, name="moe_up")
    sub = GROUP_ROWS // TM_DOWN
    y = grouped_matmul(act, (moe_w_down,), tstart * sub, ntiles * sub, tm=TM_DOWN, tn=1024, ck=896,
                       out_dtype=F32, vmem_mb=56, name="moe_down")
    y_prompt, y_sample = moe_combine(pos1, pos2, x3, route, norm_final, y)

    y_prompt = y_prompt.reshape(N_PROMPT_SEQ, SEQ, D)
    y_sample = y_sample.reshape(M_SAMPLE, 1, D)
    return (y_prompt, y_sample,
            conv_prompt_state.reshape(1, N_PROMPT_SEQ, 2, D), conv_sample_state,
            gla_prompt_state.reshape(1, N_PROMPT_SEQ, HEADS, DK, DV),
            gla_sample_state.reshape(1, M_SAMPLE, HEADS, DK, DV))
```

```python
import functools

import jax
import jax.numpy as jnp
from jax import lax
from jax.experimental import pallas as pl
from jax.experimental.pallas import tpu as pltpu

F32 = jnp.float32
BF16 = jnp.bfloat16

D = 2048
N_PROMPT_SEQ = 4
SEQ = 2048
M_PROMPT = N_PROMPT_SEQ * SEQ
M_SAMPLE = 128
M_TOK = M_PROMPT + M_SAMPLE
HEADS = 4
DK = 256
DV = 512
HK = HEADS * DK
HV = HEADS * DV
GATE_RANK = 16
GATE_TAU = 16.0
CHUNK = 64
N_CHUNKS = SEQ // CHUNK
D_FF = 5632
N_EXPERTS = 8
D_FF_EXPERT = 7168
EPS = 1e-6
LANES = 128

TM_DENSE = 1040
GROUP_ROWS = 256
N_ASSIGN = 2 * M_TOK
N_GROUP_TILES = N_ASSIGN // GROUP_ROWS + N_EXPERTS
R_PAD = N_GROUP_TILES * GROUP_ROWS


def _params(sem, vmem_mb):
    return pltpu.CompilerParams(dimension_semantics=sem,
                                vmem_limit_bytes=vmem_mb * 1024 * 1024)


def _rmsnorm(x, g):
    return x * lax.rsqrt(jnp.mean(x * x, axis=-1, keepdims=True) + EPS) * g


def _silu(x):
    return x * jax.nn.sigmoid(x)


def _norm_kernel(x_ref, g_ref, o_ref):
    o_ref[...] = _rmsnorm(x_ref[...], g_ref[...]).astype(o_ref.dtype)


def rmsnorm_bf16(x, g, tr=832):
    m = x.shape[0]
    return pl.pallas_call(
        _norm_kernel,
        grid=(m // tr,),
        in_specs=[pl.BlockSpec((tr, D), lambda i: (i, 0)),
                  pl.BlockSpec((1, D), lambda i: (0, 0))],
        out_specs=pl.BlockSpec((tr, D), lambda i: (i, 0)),
        out_shape=jax.ShapeDtypeStruct((m, D), BF16),
        compiler_params=_params(("arbitrary",), 40),
        name="rmsnorm",
    )(x, g.reshape(1, D))


STACK_ROWS = 512
STACK_PROMPT_STEPS = M_PROMPT // STACK_ROWS


def _stack_norm_kernel(xp_ref, xs_ref, g_ref, x_ref, h_ref):
    i = pl.program_id(0)
    g = g_ref[...]

    @pl.when(i < STACK_PROMPT_STEPS)
    def _():
        x = xp_ref[...]
        x_ref[...] = x
        h_ref[...] = _rmsnorm(x, g).astype(h_ref.dtype)

    @pl.when(i == STACK_PROMPT_STEPS)
    def _():
        x = xs_ref[...]
        x_ref[0:M_SAMPLE, :] = x
        h_ref[0:M_SAMPLE, :] = _rmsnorm(x, g).astype(h_ref.dtype)


def stack_norm(xp, xs, g):
    last_prompt = STACK_PROMPT_STEPS - 1
    return pl.pallas_call(
        _stack_norm_kernel,
        grid=(STACK_PROMPT_STEPS + 1,),
        in_specs=[pl.BlockSpec((STACK_ROWS, D), lambda i: (jnp.minimum(i, last_prompt), 0)),
                  pl.BlockSpec((M_SAMPLE, D), lambda i: (0, 0)),
                  pl.BlockSpec((1, D), lambda i: (0, 0))],
        out_specs=[pl.BlockSpec((STACK_ROWS, D), lambda i: (i, 0)),
                   pl.BlockSpec((STACK_ROWS, D), lambda i: (i, 0))],
        out_shape=[jax.ShapeDtypeStruct((M_TOK, D), F32),
                   jax.ShapeDtypeStruct((M_TOK, D), BF16)],
        compiler_params=_params(("arbitrary",), 40),
        name="stack_norm",
    )(xp, xs, g.reshape(1, D))


def _linear_kernel(*refs, has_res, w_is_nk):
    if has_res:
        x_ref, w_ref, r_ref, o_ref, wb_ref = refs
    else:
        x_ref, w_ref, o_ref, wb_ref = refs

    @pl.when(pl.program_id(1) == 0)
    def _():
        wb_ref[...] = w_ref[...].astype(BF16)

    contract_w = 1 if w_is_nk else 0
    acc = lax.dot_general(x_ref[...], wb_ref[...], (((1,), (contract_w,)), ((), ())),
                          preferred_element_type=F32)
    if has_res:
        acc = acc + r_ref[...]
    o_ref[...] = acc.astype(o_ref.dtype)


def linear(x, w, layer, n_out, tn, tm, res=None, out_dtype=F32, vmem_mb=56, name="linear",
           w_is_nk=False):
    m, k = x.shape
    wblock = (None, tn, k) if w_is_nk else (None, k, tn)
    wmap = (lambda j, i: (layer, j, 0)) if w_is_nk else (lambda j, i: (layer, 0, j))
    in_specs = [pl.BlockSpec((tm, k), lambda j, i: (i, 0)),
                pl.BlockSpec(wblock, wmap)]
    args = [x, w]
    if res is not None:
        in_specs.append(pl.BlockSpec((tm, tn), lambda j, i: (i, j)))
        args.append(res)
    return pl.pallas_call(
        functools.partial(_linear_kernel, has_res=res is not None, w_is_nk=w_is_nk),
        grid=(n_out // tn, m // tm),
        in_specs=in_specs,
        out_specs=pl.BlockSpec((tm, tn), lambda j, i: (i, j)),
        out_shape=jax.ShapeDtypeStruct((m, n_out), out_dtype),
        scratch_shapes=[pltpu.VMEM(wblock[1:], BF16)],
        compiler_params=_params(("arbitrary", "arbitrary"), vmem_mb),
        name=name,
    )(*args)


def _swiglu_kernel(x_ref, wg_ref, wu_ref, o_ref, wgb_ref, wub_ref):
    @pl.when(pl.program_id(1) == 0)
    def _():
        wgb_ref[...] = wg_ref[...].astype(BF16)
        wub_ref[...] = wu_ref[...].astype(BF16)

    x = x_ref[...]
    g = jnp.dot(x, wgb_ref[...], preferred_element_type=F32)
    u = jnp.dot(x, wub_ref[...], preferred_element_type=F32)
    o_ref[...] = (_silu(g) * u).astype(o_ref.dtype)


def swiglu_up(x, wg, wu, layer, tn=512, tm=TM_DENSE):
    m, k = x.shape
    f = wg.shape[-1]
    wspec = pl.BlockSpec((None, k, tn), lambda j, i: (layer, 0, j))
    return pl.pallas_call(
        _swiglu_kernel,
        grid=(f // tn, m // tm),
        in_specs=[pl.BlockSpec((tm, k), lambda j, i: (i, 0)), wspec, wspec],
        out_specs=pl.BlockSpec((tm, tn), lambda j, i: (i, j)),
        out_shape=jax.ShapeDtypeStruct((m, f), BF16),
        scratch_shapes=[pltpu.VMEM((k, tn), BF16), pltpu.VMEM((k, tn), BF16)],
        compiler_params=_params(("arbitrary", "arbitrary"), 56),
        name="ffn_up",
    )(x, wg, wu)


CONV_ROWS = 128
CONV_BLOCKS_PER_SEQ = SEQ // CONV_ROWS
CONV_PROMPT_BLOCKS = M_PROMPT // CONV_ROWS


def _conv_kernel(h_ref, b_ref, c_ref, w_ref, buf_ref, bz_ref, stp_ref, sts_ref, carry_ref):
    i = pl.program_id(0)
    u = c_ref[...] * h_ref[...]
    w = w_ref[...]

    @pl.when(i < CONV_PROMPT_BLOCKS)
    def _():
        @pl.when(i % CONV_BLOCKS_PER_SEQ == 0)
        def _():
            carry_ref[...] = jnp.zeros_like(carry_ref)

        c2 = carry_ref[0:1, :]
        c1 = carry_ref[1:2, :]
        row = lax.broadcasted_iota(jnp.int32, u.shape, 0)
        u1 = jnp.where(row == 0, c1, pltpu.roll(u, 1, 0))
        u2 = jnp.where(row == 0, c2, jnp.where(row == 1, c1, pltpu.roll(u, 2, 0)))
        z = w[0:1, :] * u2
        z = z + w[1:2, :] * u1
        z = z + w[2:3, :] * u
        bz_ref[...] = (b_ref[...] * z).astype(bz_ref.dtype)
        tail = u[CONV_ROWS - 2:CONV_ROWS, :]
        carry_ref[0:2, :] = tail
        stp_ref[...] = tail

    @pl.when(i == CONV_PROMPT_BLOCKS)
    def _():
        b0 = buf_ref[:, :D]
        b1 = buf_ref[:, D:]
        z = w[0:1, :] * b0
        z = z + w[1:2, :] * b1
        z = z + w[2:3, :] * u
        bz_ref[...] = (b_ref[...] * z).astype(bz_ref.dtype)
        sts_ref[:, :D] = b1
        sts_ref[:, D:] = u


def conv_mix(p, conv_w, buf):
    last_seq = N_PROMPT_SEQ - 1
    return pl.pallas_call(
        _conv_kernel,
        grid=(M_TOK // CONV_ROWS,),
        in_specs=[pl.BlockSpec((CONV_ROWS, D), lambda i: (i, 0)),
                  pl.BlockSpec((CONV_ROWS, D), lambda i: (i, 1)),
                  pl.BlockSpec((CONV_ROWS, D), lambda i: (i, 2)),
                  pl.BlockSpec((3, D), lambda i: (0, 0)),
                  pl.BlockSpec((M_SAMPLE, 2 * D), lambda i: (0, 0))],
        out_specs=[pl.BlockSpec((CONV_ROWS, D), lambda i: (i, 0)),
                   pl.BlockSpec((None, 2, D),
                                lambda i: (jnp.minimum(i // CONV_BLOCKS_PER_SEQ, last_seq), 0, 0)),
                   pl.BlockSpec((M_SAMPLE, 2 * D), lambda i: (0, 0))],
        out_shape=[jax.ShapeDtypeStruct((M_TOK, D), BF16),
                   jax.ShapeDtypeStruct((N_PROMPT_SEQ, 2, D), F32),
                   jax.ShapeDtypeStruct((M_SAMPLE, 2 * D), F32)],
        scratch_shapes=[pltpu.VMEM((8, D), F32)],
        compiler_params=_params(("arbitrary",), 32),
        name="conv_mix",
    )(p, p, p, conv_w, buf)


def _log_sigmoid(x):
    return jnp.minimum(x, 0.0) - jnp.log1p(jnp.exp(-jnp.abs(x)))


def _gla_gate_kernel(h_ref, wa_ref, wg_ref, bg_ref, g_ref):
    a = jnp.dot(h_ref[...], wa_ref[...].astype(BF16), preferred_element_type=F32)
    z = jnp.dot(a.astype(BF16), wg_ref[...].astype(BF16), preferred_element_type=F32) + bg_ref[...]
    g_ref[...] = _log_sigmoid(z) * (1.0 / GATE_TAU)


def gla_gate(h, w_a_pad, w_gate_pad, b_gate, tm=TM_DENSE):
    return pl.pallas_call(
        _gla_gate_kernel,
        grid=(M_TOK // tm,),
        in_specs=[pl.BlockSpec((tm, D), lambda i: (i, 0)),
                  pl.BlockSpec((D, LANES), lambda i: (0, 0)),
                  pl.BlockSpec((LANES, HK), lambda i: (0, 0)),
                  pl.BlockSpec((1, HK), lambda i: (0, 0))],
        out_specs=pl.BlockSpec((tm, HK), lambda i: (i, 0)),
        out_shape=jax.ShapeDtypeStruct((M_TOK, HK), F32),
        compiler_params=_params(("arbitrary",), 40),
        name="gla_gate",
    )(h, w_a_pad, w_gate_pad, b_gate)


def _row_to_cols(row):
    return jnp.transpose(jnp.broadcast_to(row, (LANES, row.shape[1])))


def _split3_bf16(x):
    x1 = x.astype(BF16)
    r1 = x - x1.astype(F32)
    x2 = r1.astype(BF16)
    x3 = (r1 - x2.astype(F32)).astype(BF16)
    return x1, x2, x3


def _gla_prompt_kernel(o_init, q_ref, k_ref, v_ref, r_ref, g_ref, gn_ref, o_ref, sout_ref, s_ref):
    del o_init
    c = pl.program_id(1)

    @pl.when(c == 0)
    def _():
        s_ref[...] = jnp.zeros_like(s_ref)

    row = lax.broadcasted_iota(jnp.int32, (CHUNK, CHUNK), 0)
    col = lax.broadcasted_iota(jnp.int32, (CHUNK, CHUNK), 1)
    tri = row >= col
    trib = tri.astype(BF16)

    g1, g2, g3 = _split3_bf16(g_ref[...])
    b = (jnp.dot(trib, g1, preferred_element_type=F32)
         + jnp.dot(trib, g2, preferred_element_type=F32)
         + jnp.dot(trib, g3, preferred_element_type=F32))
    b_last = b[CHUNK - 1:CHUNK, :]
    q = q_ref[...] * (DK ** -0.5)
    k = k_ref[...]
    q_dec = (q * jnp.exp(b)).astype(BF16)
    k_inv = (k * jnp.exp(-b)).astype(BF16)
    k_end = (k * jnp.exp(b_last - b)).astype(BF16)
    decay = jnp.exp(b_last)
    gn = gn_ref[...]

    for h in range(HEADS):
        ks = slice(h * DK, (h + 1) * DK)
        vs = slice(h * DV, (h + 1) * DV)
        vb = v_ref[:, vs].astype(BF16)
        s_old = s_ref[h]
        scores = lax.dot_general(q_dec[:, ks], k_inv[:, ks], (((1,), (1,)), ((), ())),
                                 preferred_element_type=F32)
        scores = jnp.where(tri, scores, 0.0).astype(BF16)
        o = (jnp.dot(scores, vb, preferred_element_type=F32)
             + jnp.dot(q_dec[:, ks], s_old.astype(BF16), preferred_element_type=F32))
        kv = lax.dot_general(k_end[:, ks], vb, (((0,), (0,)), ((), ())),
                             preferred_element_type=F32)
        dcol = _row_to_cols(decay[:, ks])
        s_ref[h] = jnp.concatenate(
            [s_old[:, j * LANES:(j + 1) * LANES] * dcol for j in range(DV // LANES)], axis=1) + kv
        on = _rmsnorm(o, gn)
        o_ref[:, vs] = (_silu(r_ref[:, vs]) * on).astype(o_ref.dtype)

    @pl.when(c == N_CHUNKS - 1)
    def _():
        sout_ref[...] = s_ref[...]


def gla_prompt(p, g, g_norm):
    rows = lambda n, c: n * N_CHUNKS + c
    return pl.pallas_call(
        _gla_prompt_kernel,
        grid=(N_PROMPT_SEQ, N_CHUNKS),
        in_specs=[pl.BlockSpec(memory_space=pl.ANY),
                  pl.BlockSpec((CHUNK, HK), lambda n, c: (rows(n, c), 0)),
                  pl.BlockSpec((CHUNK, HK), lambda n, c: (rows(n, c), 1)),
                  pl.BlockSpec((CHUNK, HV), lambda n, c: (rows(n, c), 1)),
                  pl.BlockSpec((CHUNK, HV), lambda n, c: (rows(n, c), 2)),
                  pl.BlockSpec((CHUNK, HK), lambda n, c: (rows(n, c), 0)),
                  pl.BlockSpec((1, DV), lambda n, c: (0, 0))],
        out_specs=[pl.BlockSpec((CHUNK, HV), lambda n, c: (rows(n, c), 0)),
                   pl.BlockSpec((None, HEADS, DK, DV), lambda n, c: (n, 0, 0, 0))],
        out_shape=[jax.ShapeDtypeStruct((M_TOK, HV), BF16),
                   jax.ShapeDtypeStruct((N_PROMPT_SEQ, HEADS, DK, DV), F32)],
        scratch_shapes=[pltpu.VMEM((HEADS, DK, DV), F32)],
        input_output_aliases={0: 0},
        compiler_params=_params(("arbitrary", "arbitrary"), 40),
        name="gla_prompt",
    )(jnp.zeros((M_TOK, HV), BF16), p, p, p, p, g, g_norm)


GLA_SAMPLE_SEQS = 2


def _gla_sample_kernel(gated_any, q_ref, k_ref, v_ref, r_ref, g_ref, gn_ref, s_ref,
                       gated_ref, sout_ref, qt_ref, kt_ref, et_ref, o_scr):
    del gated_any
    i = pl.program_id(0)

    @pl.when(i == 0)
    def _():
        q = q_ref[...] * (DK ** -0.5)
        k = k_ref[...]
        e = jnp.exp(g_ref[...])
        for h in range(HEADS):
            ks = slice(h * DK, (h + 1) * DK)
            qt_ref[h] = jnp.transpose(q[:, ks])
            kt_ref[h] = jnp.transpose(k[:, ks])
            et_ref[h] = jnp.transpose(e[:, ks])

    lane = lax.broadcasted_iota(jnp.int32, (DK, M_SAMPLE), 1)
    for s in range(GLA_SAMPLE_SEQS):
        n = i * GLA_SAMPLE_SEQS + s
        pick = lane == n

        def column(t):
            return jnp.sum(jnp.where(pick, t, 0.0), axis=1, keepdims=True)

        for h in range(HEADS):
            vs = slice(h * DV, (h + 1) * DV)
            v_row = v_ref[pl.ds(n, 1), vs]
            s_new = s_ref[s, h] * column(et_ref[h]) + column(kt_ref[h]) * v_row
            sout_ref[s, h] = s_new
            o_scr[pl.ds(n, 1), vs] = jnp.sum(column(qt_ref[h]) * s_new, axis=0, keepdims=True)

    @pl.when(i == pl.num_programs(0) - 1)
    def _():
        gn = gn_ref[...]
        for h in range(HEADS):
            vs = slice(h * DV, (h + 1) * DV)
            on = _rmsnorm(o_scr[:, vs], gn)
            gated_ref[:, vs] = (_silu(r_ref[:, vs]) * on).astype(gated_ref.dtype)


def gla_sample(gated, p, g, g_norm, state):
    rb = M_PROMPT // M_SAMPLE
    bs = GLA_SAMPLE_SEQS
    return pl.pallas_call(
        _gla_sample_kernel,
        grid=(M_SAMPLE // bs,),
        in_specs=[pl.BlockSpec(memory_space=pl.ANY),
                  pl.BlockSpec((M_SAMPLE, HK), lambda i: (rb, 0)),
                  pl.BlockSpec((M_SAMPLE, HK), lambda i: (rb, 1)),
                  pl.BlockSpec((M_SAMPLE, HV), lambda i: (rb, 1)),
                  pl.BlockSpec((M_SAMPLE, HV), lambda i: (rb, 2)),
                  pl.BlockSpec((M_SAMPLE, HK), lambda i: (rb, 0)),
                  pl.BlockSpec((1, DV), lambda i: (0, 0)),
                  pl.BlockSpec((bs, HEADS, DK, DV), lambda i: (i, 0, 0, 0))],
        out_specs=[pl.BlockSpec((M_SAMPLE, HV), lambda i: (rb, 0)),
                   pl.BlockSpec((bs, HEADS, DK, DV), lambda i: (i, 0, 0, 0))],
        out_shape=[jax.ShapeDtypeStruct((M_TOK, HV), BF16),
                   jax.ShapeDtypeStruct((M_SAMPLE, HEADS, DK, DV), F32)],
        scratch_shapes=[pltpu.VMEM((HEADS, DK, M_SAMPLE), F32),
                        pltpu.VMEM((HEADS, DK, M_SAMPLE), F32),
                        pltpu.VMEM((HEADS, DK, M_SAMPLE), F32),
                        pltpu.VMEM((M_SAMPLE, HV), F32)],
        input_output_aliases={0: 0},
        compiler_params=_params(("arbitrary",), 40),
        name="gla_sample",
    )(gated, p, p, p, p, g, g_norm, state)


def _router_kernel(x_ref, g_ref, wr_ref, route_ref, idx_ref):
    h = _rmsnorm(x_ref[...], g_ref[...])
    logits = jnp.dot(h, wr_ref[...], preferred_element_type=F32, precision=lax.Precision.HIGHEST)
    lane = lax.broadcasted_iota(jnp.int32, logits.shape, 1)
    lane_f = lane.astype(F32)
    neg = jnp.float32(-jnp.inf)
    logits = jnp.where(lane < N_EXPERTS, logits, neg)
    m1 = jnp.max(logits, axis=1, keepdims=True)
    i1 = jnp.min(jnp.where(logits == m1, lane_f, float(LANES)), axis=1, keepdims=True)
    rest = jnp.where(lane_f == i1, neg, logits)
    m2 = jnp.max(rest, axis=1, keepdims=True)
    i2 = jnp.min(jnp.where(rest == m2, lane_f, float(LANES)), axis=1, keepdims=True)
    e2 = jnp.exp(m2 - m1)
    den = 1.0 + e2
    w1 = 1.0 / den
    w2 = e2 / den
    route_ref[...] = jnp.where(lane == 0, w1, jnp.where(lane == 1, w2, 0.0))
    idx_ref[...] = jnp.where(lane == 0, i1, jnp.where(lane == 1, i2, 0.0)).astype(jnp.int32)


def router(x, g, w_router_pad, tm=416):
    return pl.pallas_call(
        _router_kernel,
        grid=(M_TOK // tm,),
        in_specs=[pl.BlockSpec((tm, D), lambda i: (i, 0)),
                  pl.BlockSpec((1, D), lambda i: (0, 0)),
                  pl.BlockSpec((D, LANES), lambda i: (0, 0))],
        out_specs=[pl.BlockSpec((tm, LANES), lambda i: (i, 0)),
                   pl.BlockSpec((tm, LANES), lambda i: (i, 0))],
        out_shape=[jax.ShapeDtypeStruct((M_TOK, LANES), F32),
                   jax.ShapeDtypeStruct((M_TOK, LANES), jnp.int32)],
        compiler_params=_params(("arbitrary",), 40),
        name="router",
    )(x, g.reshape(1, D), w_router_pad)


def _gather_kernel(tok_ref, nvalid_ref, src_hbm, g_ref, o_ref, buf, sem):
    i = pl.program_id(0)
    nv = nvalid_ref[0]
    tm = buf.shape[1]
    slot = lax.rem(i, 2)

    def issue(tile, dst_slot):
        base = tile * tm

        def body(r2, carry):
            for k in range(2):
                r = 2 * r2 + k
                t = tok_ref[base + r]
                pltpu.make_async_copy(src_hbm.at[pl.ds(t, 1)], buf.at[dst_slot, pl.ds(r, 1)],
                                      sem.at[dst_slot]).start(priority=k)
            return carry

        lax.fori_loop(0, tm // 2, body, 0, unroll=4)

    @pl.when(i == 0)
    def _():
        issue(0, 0)

    @pl.when(i + 1 < nv)
    def _():
        issue(i + 1, 1 - slot)

    @pl.when(i < nv)
    def _():
        pltpu.make_async_copy(src_hbm.at[pl.ds(0, tm)], buf.at[slot], sem.at[slot]).wait()
        o_ref[...] = _rmsnorm(buf[slot], g_ref[...]).astype(o_ref.dtype)

    @pl.when(i >= nv)
    def _():
        o_ref[...] = jnp.zeros_like(o_ref)


def gather_rows(row_token, n_valid_tiles, src, g, tm=GROUP_ROWS):
    return pl.pallas_call(
        _gather_kernel,
        grid_spec=pltpu.PrefetchScalarGridSpec(
            num_scalar_prefetch=2,
            grid=(R_PAD // tm,),
            in_specs=[pl.BlockSpec(memory_space=pl.ANY),
                      pl.BlockSpec((1, D), lambda i, tok, nv: (0, 0))],
            out_specs=pl.BlockSpec((tm, D), lambda i, tok, nv: (i, 0)),
            scratch_shapes=[pltpu.VMEM((2, tm, D), F32), pltpu.SemaphoreType.DMA((2,))]),
        out_shape=jax.ShapeDtypeStruct((R_PAD, D), BF16),
        compiler_params=_params(("arbitrary",), 32),
        name="moe_gather",
    )(row_token, n_valid_tiles, src, g.reshape(1, D))


def _grouped_kernel(tstart, ntiles, x_hbm, *refs, n_w, n_col, total_tiles):
    w_hbm = refs[:n_w]
    o_hbm = refs[n_w]
    wbf, stage, xbuf, obuf, xsem, osem, wsem = refs[n_w + 1:]
    s = pl.program_id(0)
    n_items = pl.num_programs(0)
    e = s // n_col
    j = lax.rem(s, n_col)
    p = lax.rem(s, 2)
    tm = xbuf.shape[1]
    tn = obuf.shape[2]
    ck = stage.shape[2]
    n_chunks = wbf.shape[2] // ck
    nt = ntiles[e]
    t0 = tstart[e]
    col = pl.multiple_of(j * tn, tn)
    has_next = s + 1 < n_items

    def w_copies(item, c, q):
        row = pl.multiple_of(c * ck, ck)
        wcol = pl.multiple_of(lax.rem(item, n_col) * tn, tn)
        return [pltpu.make_async_copy(w.at[0, item // n_col, pl.ds(row, ck), pl.ds(wcol, tn)],
                                      stage.at[q, i], wsem.at[q, i])
                for i, w in enumerate(w_hbm)]

    def w_start(item, c, q):
        for cp in w_copies(item, c, q):
            cp.start()

    def w_chunk(item, c, dst):
        q = lax.rem(c, 2)

        @pl.when(c + 1 < n_chunks)
        def _():
            w_start(item, c + 1, 1 - q)

        row = pl.multiple_of(c * ck, ck)
        for i, cp in enumerate(w_copies(item, c, q)):
            cp.wait()
            wbf[dst, i, pl.ds(row, ck), :] = stage[q, i].astype(BF16)

    @pl.when(s == 0)
    def _():
        w_start(0, 0, 0)

        def first(c, carry):
            w_chunk(0, c, 0)
            return carry

        lax.fori_loop(0, n_chunks, first, 0)

    @pl.when(has_next)
    def _():
        w_start(s + 1, 0, 0)

    def x_copy(tile, slot):
        row = pl.multiple_of(tile * tm, tm)
        return pltpu.make_async_copy(x_hbm.at[pl.ds(row, tm)], xbuf.at[slot], xsem.at[slot])

    def o_copy(tile, slot):
        row = pl.multiple_of(tile * tm, tm)
        return pltpu.make_async_copy(obuf.at[slot], o_hbm.at[pl.ds(row, tm), pl.ds(col, tn)],
                                     osem.at[slot])

    @pl.when(nt > 0)
    def _():
        x_copy(t0, 0).start()

    def body(t, chunks_done):
        slot = lax.rem(t, 2)
        x_copy(t0 + t, slot).wait()

        @pl.when(t + 1 < nt)
        def _():
            x_copy(t0 + t + 1, 1 - slot).start()

        @pl.when(t >= 2)
        def _():
            o_copy(t0 + t - 2, slot).wait()

        x = xbuf[slot]
        if n_w == 2:
            g = jnp.dot(x, wbf[p, 0], preferred_element_type=F32)
            u = jnp.dot(x, wbf[p, 1], preferred_element_type=F32)
            obuf[slot] = (_silu(g) * u).astype(obuf.dtype)
        else:
            obuf[slot] = jnp.dot(x, wbf[p, 0], preferred_element_type=F32).astype(obuf.dtype)
        o_copy(t0 + t, slot).start()

        stream = has_next & (chunks_done < n_chunks)

        @pl.when(stream)
        def _():
            w_chunk(s + 1, chunks_done, 1 - p)

        return chunks_done + stream.astype(jnp.int32)

    chunks_done = lax.fori_loop(0, nt, body, jnp.int32(0))

    @pl.when(has_next)
    def _():
        def rest(c, carry):
            w_chunk(s + 1, c, 1 - p)
            return carry

        lax.fori_loop(chunks_done, n_chunks, rest, 0)

    @pl.when(nt >= 2)
    def _():
        o_copy(t0 + nt - 2, lax.rem(nt, 2)).wait()

    @pl.when(nt >= 1)
    def _():
        o_copy(t0 + nt - 1, lax.rem(nt + 1, 2)).wait()

    @pl.when(e == N_EXPERTS - 1)
    def _():
        obuf[0] = jnp.zeros(obuf.shape[1:], obuf.dtype)

        def zero_tile(tile, carry):
            cp = o_copy(tile, 0)
            cp.start()
            cp.wait()
            return carry

        lax.fori_loop(t0 + nt, total_tiles, zero_tile, 0)


def grouped_matmul(x, ws, tstart, ntiles, tm, tn, ck, out_dtype, vmem_mb, name):
    k = x.shape[1]
    n = ws[0].shape[-1]
    n_w = len(ws)
    n_col = n // tn
    any_spec = pl.BlockSpec(memory_space=pl.ANY)
    return pl.pallas_call(
        functools.partial(_grouped_kernel, n_w=n_w, n_col=n_col, total_tiles=R_PAD // tm),
        grid_spec=pltpu.PrefetchScalarGridSpec(
            num_scalar_prefetch=2,
            grid=(N_EXPERTS * n_col,),
            in_specs=[any_spec] * (1 + n_w),
            out_specs=any_spec,
            scratch_shapes=[pltpu.VMEM((2, n_w, k, tn), BF16), pltpu.VMEM((2, n_w, ck, tn), F32),
                            pltpu.VMEM((2, tm, k), BF16), pltpu.VMEM((2, tm, tn), out_dtype),
                            pltpu.SemaphoreType.DMA((2,)), pltpu.SemaphoreType.DMA((2,)),
                            pltpu.SemaphoreType.DMA((2, n_w))]),
        out_shape=jax.ShapeDtypeStruct((R_PAD, n), out_dtype),
        compiler_params=_params(("arbitrary",), vmem_mb),
        name=name,
    )(tstart, ntiles, x, *ws)


COMBINE_ROWS = 128
COMBINE_PROMPT_STEPS = M_PROMPT // COMBINE_ROWS


def _combine_kernel(p1_ref, p2_ref, x_ref, route_ref, g_ref, y_hbm, op_ref, os_ref, b1, b2, sem):
    i = pl.program_id(0)
    tm = b1.shape[1]
    slot = lax.rem(i, 2)

    def issue(tile, dst_slot):
        base = tile * tm

        def body(r, carry):
            pltpu.make_async_copy(y_hbm.at[pl.ds(p1_ref[base + r], 1)], b1.at[dst_slot, pl.ds(r, 1)],
                                  sem.at[0, dst_slot]).start(priority=0)
            pltpu.make_async_copy(y_hbm.at[pl.ds(p2_ref[base + r], 1)], b2.at[dst_slot, pl.ds(r, 1)],
                                  sem.at[1, dst_slot]).start(priority=1)
            return carry

        lax.fori_loop(0, tm, body, 0, unroll=8)

    @pl.when(i == 0)
    def _():
        issue(0, 0)

    @pl.when(i + 1 < pl.num_programs(0))
    def _():
        issue(i + 1, 1 - slot)

    pltpu.make_async_copy(y_hbm.at[pl.ds(0, tm)], b1.at[slot], sem.at[0, slot]).wait()
    pltpu.make_async_copy(y_hbm.at[pl.ds(0, tm)], b2.at[slot], sem.at[1, slot]).wait()
    route = route_ref[...]
    w1 = route[:, 0:1]
    w2 = route[:, 1:2]
    x = x_ref[...] + (w1 * b1[slot] + w2 * b2[slot])
    out = _rmsnorm(x, g_ref[...])

    @pl.when(i < COMBINE_PROMPT_STEPS)
    def _():
        op_ref[...] = out

    @pl.when(i == COMBINE_PROMPT_STEPS)
    def _():
        os_ref[...] = out


def moe_combine(pos1, pos2, x, route, g_final, y):
    tm = COMBINE_ROWS
    last_prompt = COMBINE_PROMPT_STEPS - 1
    return pl.pallas_call(
        _combine_kernel,
        grid_spec=pltpu.PrefetchScalarGridSpec(
            num_scalar_prefetch=2,
            grid=(M_TOK // tm,),
            in_specs=[pl.BlockSpec((tm, D), lambda i, p1, p2: (i, 0)),
                      pl.BlockSpec((tm, LANES), lambda i, p1, p2: (i, 0)),
                      pl.BlockSpec((1, D), lambda i, p1, p2: (0, 0)),
                      pl.BlockSpec(memory_space=pl.ANY)],
            out_specs=[pl.BlockSpec((tm, D), lambda i, p1, p2: (jnp.minimum(i, last_prompt), 0)),
                       pl.BlockSpec((M_SAMPLE, D), lambda i, p1, p2: (0, 0))],
            scratch_shapes=[pltpu.VMEM((2, tm, D), F32), pltpu.VMEM((2, tm, D), F32),
                            pltpu.SemaphoreType.DMA((2, 2))]),
        out_shape=[jax.ShapeDtypeStruct((M_PROMPT, D), F32),
                   jax.ShapeDtypeStruct((M_SAMPLE, D), F32)],
        compiler_params=_params(("arbitrary",), 40),
        name="moe_combine",
    )(pos1, pos2, x, route, g_final.reshape(1, D), y)


def _group_tables(idx):
    e_flat = jnp.concatenate([idx[:, 0], idx[:, 1]])
    onehot = (e_flat[:, None] == jnp.arange(N_EXPERTS, dtype=jnp.int32)[None, :]).astype(jnp.int32)
    csum = jnp.cumsum(onehot, axis=0)
    counts = csum[-1]
    rank = jnp.sum(csum * onehot, axis=1) - 1
    ntiles = (counts + GROUP_ROWS - 1) // GROUP_ROWS
    tile_end = jnp.cumsum(ntiles)
    tstart = tile_end - ntiles
    dest = jnp.sum(onehot * (tstart * GROUP_ROWS)[None, :], axis=1) + rank
    token = jnp.concatenate([jnp.arange(M_TOK, dtype=jnp.int32)] * 2)
    row_token = jnp.zeros((R_PAD,), jnp.int32).at[dest].set(token)
    return row_token, tile_end[-1:], dest[:M_TOK], dest[M_TOK:], tstart, ntiles


def kernel(x_prompt, x_sample, state_conv, state_gla, norm_mix, norm_ffn, norm_final,
           conv_w_in, conv_w, conv_w_out, gla_w_in, gla_w_gate, gla_b_gate, gla_norm, gla_w_out,
           ffn_w_gate, ffn_w_up, ffn_w_down, moe_w_router, moe_w_gate, moe_w_up, moe_w_down):
    x0, h = stack_norm(x_prompt.reshape(M_PROMPT, D), x_sample.reshape(M_SAMPLE, D), norm_mix[0])
    p = linear(h, conv_w_in, 0, 3 * D, tn=1024, tm=TM_DENSE, name="conv_in")
    bz, conv_prompt_state, conv_sample_state = conv_mix(
        p, conv_w[0], state_conv[0].reshape(M_SAMPLE, 2 * D))
    conv_sample_state = conv_sample_state.reshape(1, M_SAMPLE, 2, D)
    x1 = linear(bz, conv_w_out, 0, D, tn=1024, tm=TM_DENSE, res=x0, name="conv_out")

    h = rmsnorm_bf16(x1, norm_ffn[0])
    a = swiglu_up(h, ffn_w_gate, ffn_w_up, 0)
    x2 = linear(a, ffn_w_down, 0, D, tn=512, tm=640, res=x1, name="ffn_down")

    h = rmsnorm_bf16(x2, norm_mix[1])
    p = linear(h, jnp.swapaxes(gla_w_in, 1, 2), 0, 2 * HK + 2 * HV, tn=1024, tm=TM_DENSE,
               name="gla_in", w_is_nk=True)
    w_gate_pad = jnp.pad(gla_w_gate[0], ((0, LANES - GATE_RANK), (0, 0)))
    w_a_pad = jnp.pad(gla_w_in[0, :, 2 * HK + 2 * HV:], ((0, 0), (0, LANES - GATE_RANK)))
    g = gla_gate(h, w_a_pad, w_gate_pad, gla_b_gate[0].reshape(1, HK))
    gn = gla_norm[0].reshape(1, DV)
    gated, gla_prompt_state = gla_prompt(p, g, gn)
    gated, gla_sample_state = gla_sample(gated, p, g, gn, state_gla[0])
    x3 = linear(gated, gla_w_out, 0, D, tn=1024, tm=TM_DENSE, res=x2, name="gla_out")

    w_router_pad = jnp.pad(moe_w_router[0], ((0, 0), (0, LANES - N_EXPERTS)))
    route, idx = router(x3, norm_ffn[1], w_router_pad)
    row_token, n_valid, pos1, pos2, tstart, ntiles = _group_tables(idx)
    xs = gather_rows(row_token, n_valid, x3, norm_ffn[1])
    act = grouped_matmul(xs, (moe_w_gate, moe_w_up), tstart, ntiles, tm=GROUP_ROWS, tn=1024, ck=512,
                         out_dtype=BF16, vmem_mb=48, name="moe_up")
    y = grouped_matmul(act, (moe_w_down,), tstart, ntiles, tm=GROUP_ROWS, tn=1024, ck=896,
                       out_dtype=F32, vmem_mb=56, name="moe_down")
    y_prompt, y_sample = moe_combine(pos1, pos2, x3, route, norm_final, y)

    y_prompt = y_prompt.reshape(N_PROMPT_SEQ, SEQ, D)
    y_sample = y_sample.reshape(M_SAMPLE, 1, D)
    return (y_prompt, y_sample,
            conv_prompt_state.reshape(1, N_PROMPT_SEQ, 2, D), conv_sample_state,
            gla_prompt_state.reshape(1, N_PROMPT_SEQ, HEADS, DK, DV),
            gla_sample_state.reshape(1, M_SAMPLE, HEADS, DK, DV))
```

```python
import functools

import jax
import jax.numpy as jnp
from jax import lax
from jax.experimental import pallas as pl
from jax.experimental.pallas import tpu as pltpu

F32 = jnp.float32
BF16 = jnp.bfloat16

D = 2048
N_PROMPT_SEQ = 4
SEQ = 2048
M_PROMPT = N_PROMPT_SEQ * SEQ
M_SAMPLE = 128
M_TOK = M_PROMPT + M_SAMPLE
HEADS = 4
DK = 256
DV = 512
HK = HEADS * DK
HV = HEADS * DV
GATE_RANK = 16
GATE_TAU = 16.0
CHUNK = 64
N_CHUNKS = SEQ // CHUNK
D_FF = 5632
N_EXPERTS = 8
D_FF_EXPERT = 7168
EPS = 1e-6
LANES = 128

TM_DENSE = 1040
GROUP_ROWS = 256
N_ASSIGN = 2 * M_TOK
N_GROUP_TILES = N_ASSIGN // GROUP_ROWS + N_EXPERTS
R_PAD = N_GROUP_TILES * GROUP_ROWS


def _params(sem, vmem_mb):
    return pltpu.CompilerParams(dimension_semantics=sem,
                                vmem_limit_bytes=vmem_mb * 1024 * 1024)


def _rmsnorm(x, g):
    return x * lax.rsqrt(jnp.mean(x * x, axis=-1, keepdims=True) + EPS) * g


def _silu(x):
    return x * jax.nn.sigmoid(x)


def _norm_kernel(x_ref, g_ref, o_ref):
    o_ref[...] = _rmsnorm(x_ref[...], g_ref[...]).astype(o_ref.dtype)


def rmsnorm_bf16(x, g, tr=832):
    m = x.shape[0]
    return pl.pallas_call(
        _norm_kernel,
        grid=(m // tr,),
        in_specs=[pl.BlockSpec((tr, D), lambda i: (i, 0)),
                  pl.BlockSpec((1, D), lambda i: (0, 0))],
        out_specs=pl.BlockSpec((tr, D), lambda i: (i, 0)),
        out_shape=jax.ShapeDtypeStruct((m, D), BF16),
        compiler_params=_params(("arbitrary",), 40),
        name="rmsnorm",
    )(x, g.reshape(1, D))


STACK_ROWS = 512
STACK_PROMPT_STEPS = M_PROMPT // STACK_ROWS


def _stack_norm_kernel(xp_ref, xs_ref, g_ref, x_ref, h_ref):
    i = pl.program_id(0)
    g = g_ref[...]

    @pl.when(i < STACK_PROMPT_STEPS)
    def _():
        x = xp_ref[...]
        x_ref[...] = x
        h_ref[...] = _rmsnorm(x, g).astype(h_ref.dtype)

    @pl.when(i == STACK_PROMPT_STEPS)
    def _():
        x = xs_ref[...]
        x_ref[0:M_SAMPLE, :] = x
        h_ref[0:M_SAMPLE, :] = _rmsnorm(x, g).astype(h_ref.dtype)


def stack_norm(xp, xs, g):
    last_prompt = STACK_PROMPT_STEPS - 1
    return pl.pallas_call(
        _stack_norm_kernel,
        grid=(STACK_PROMPT_STEPS + 1,),
        in_specs=[pl.BlockSpec((STACK_ROWS, D), lambda i: (jnp.minimum(i, last_prompt), 0)),
                  pl.BlockSpec((M_SAMPLE, D), lambda i: (0, 0)),
                  pl.BlockSpec((1, D), lambda i: (0, 0))],
        out_specs=[pl.BlockSpec((STACK_ROWS, D), lambda i: (i, 0)),
                   pl.BlockSpec((STACK_ROWS, D), lambda i: (i, 0))],
        out_shape=[jax.ShapeDtypeStruct((M_TOK, D), F32),
                   jax.ShapeDtypeStruct((M_TOK, D), BF16)],
        compiler_params=_params(("arbitrary",), 40),
        name="stack_norm",
    )(xp, xs, g.reshape(1, D))


def _linear_kernel(*refs, has_res, w_is_nk):
    if has_res:
        x_ref, w_ref, r_ref, o_ref, wb_ref = refs
    else:
        x_ref, w_ref, o_ref, wb_ref = refs

    @pl.when(pl.program_id(1) == 0)
    def _():
        wb_ref[...] = w_ref[...].astype(BF16)

    contract_w = 1 if w_is_nk else 0
    acc = lax.dot_general(x_ref[...], wb_ref[...], (((1,), (contract_w,)), ((), ())),
                          preferred_element_type=F32)
    if has_res:
        acc = acc + r_ref[...]
    o_ref[...] = acc.astype(o_ref.dtype)


def linear(x, w, layer, n_out, tn, tm, res=None, out_dtype=F32, vmem_mb=56, name="linear",
           w_is_nk=False):
    m, k = x.shape
    wblock = (None, tn, k) if w_is_nk else (None, k, tn)
    wmap = (lambda j, i: (layer, j, 0)) if w_is_nk else (lambda j, i: (layer, 0, j))
    in_specs = [pl.BlockSpec((tm, k), lambda j, i: (i, 0)),
                pl.BlockSpec(wblock, wmap)]
    args = [x, w]
    if res is not None:
        in_specs.append(pl.BlockSpec((tm, tn), lambda j, i: (i, j)))
        args.append(res)
    return pl.pallas_call(
        functools.partial(_linear_kernel, has_res=res is not None, w_is_nk=w_is_nk),
        grid=(n_out // tn, m // tm),
        in_specs=in_specs,
        out_specs=pl.BlockSpec((tm, tn), lambda j, i: (i, j)),
        out_shape=jax.ShapeDtypeStruct((m, n_out), out_dtype),
        scratch_shapes=[pltpu.VMEM(wblock[1:], BF16)],
        compiler_params=_params(("arbitrary", "arbitrary"), vmem_mb),
        name=name,
    )(*args)


def _linear_res_norm_kernel(x_ref, w_ref, r_ref, g_ref, o_ref, h_ref, wb_ref):
    @pl.when(pl.program_id(0) == 0)
    def _():
        wb_ref[...] = w_ref[...].astype(BF16)

    y = jnp.dot(x_ref[...], wb_ref[...], preferred_element_type=F32) + r_ref[...]
    o_ref[...] = y
    h_ref[...] = _rmsnorm(y, g_ref[...]).astype(h_ref.dtype)


def linear_res_norm(x, w, layer, res, g, tm=416, name="linear_res_norm"):
    m, k = x.shape
    return pl.pallas_call(
        _linear_res_norm_kernel,
        grid=(m // tm,),
        in_specs=[pl.BlockSpec((tm, k), lambda i: (i, 0)),
                  pl.BlockSpec((None, k, D), lambda i: (layer, 0, 0), pipeline_mode=pl.Buffered(1)),
                  pl.BlockSpec((tm, D), lambda i: (i, 0)),
                  pl.BlockSpec((1, D), lambda i: (0, 0))],
        out_specs=[pl.BlockSpec((tm, D), lambda i: (i, 0)),
                   pl.BlockSpec((tm, D), lambda i: (i, 0))],
        out_shape=[jax.ShapeDtypeStruct((m, D), F32),
                   jax.ShapeDtypeStruct((m, D), BF16)],
        scratch_shapes=[pltpu.VMEM((k, D), BF16)],
        compiler_params=_params(("arbitrary",), 56),
        name=name,
    )(x, w, res, g.reshape(1, D))


def _swiglu_kernel(x_ref, wg_ref, wu_ref, o_ref, wgb_ref, wub_ref):
    @pl.when(pl.program_id(1) == 0)
    def _():
        wgb_ref[...] = wg_ref[...].astype(BF16)
        wub_ref[...] = wu_ref[...].astype(BF16)

    x = x_ref[...]
    g = jnp.dot(x, wgb_ref[...], preferred_element_type=F32)
    u = jnp.dot(x, wub_ref[...], preferred_element_type=F32)
    o_ref[...] = (_silu(g) * u).astype(o_ref.dtype)


def swiglu_up(x, wg, wu, layer, tn=512, tm=TM_DENSE):
    m, k = x.shape
    f = wg.shape[-1]
    wspec = pl.BlockSpec((None, k, tn), lambda j, i: (layer, 0, j))
    return pl.pallas_call(
        _swiglu_kernel,
        grid=(f // tn, m // tm),
        in_specs=[pl.BlockSpec((tm, k), lambda j, i: (i, 0)), wspec, wspec],
        out_specs=pl.BlockSpec((tm, tn), lambda j, i: (i, j)),
        out_shape=jax.ShapeDtypeStruct((m, f), BF16),
        scratch_shapes=[pltpu.VMEM((k, tn), BF16), pltpu.VMEM((k, tn), BF16)],
        compiler_params=_params(("arbitrary", "arbitrary"), 56),
        name="ffn_up",
    )(x, wg, wu)


CONV_TM = 512
CONV_TC = 512
CONV_TILES_PER_SEQ = SEQ // CONV_TM
CONV_PROMPT_TILES = M_PROMPT // CONV_TM


def _conv_kernel(x_ref, xs_ref, wh_ref, wb_ref, wc_ref, cw_ref, b0_ref, b1_ref,
                 bz_ref, stp_ref, s0_ref, s1_ref, wbf_ref, carry_ref):
    i = pl.program_id(1)

    @pl.when(i == 0)
    def _():
        wbf_ref[0] = wh_ref[...].astype(BF16)
        wbf_ref[1] = wb_ref[...].astype(BF16)
        wbf_ref[2] = wc_ref[...].astype(BF16)

    def project(x):
        hh = jnp.dot(x, wbf_ref[0], preferred_element_type=F32)
        bb = jnp.dot(x, wbf_ref[1], preferred_element_type=F32)
        cc = jnp.dot(x, wbf_ref[2], preferred_element_type=F32)
        return bb, cc * hh

    w = cw_ref[...]

    @pl.when(i < CONV_PROMPT_TILES)
    def _():
        @pl.when(i % CONV_TILES_PER_SEQ == 0)
        def _():
            carry_ref[...] = jnp.zeros_like(carry_ref)

        bb, u = project(x_ref[...])
        c2 = carry_ref[0:1, :]
        c1 = carry_ref[1:2, :]
        row = lax.broadcasted_iota(jnp.int32, u.shape, 0)
        u1 = jnp.where(row == 0, c1, pltpu.roll(u, 1, 0))
        u2 = jnp.where(row == 0, c2, jnp.where(row == 1, c1, pltpu.roll(u, 2, 0)))
        z = w[0:1, :] * u2
        z = z + w[1:2, :] * u1
        z = z + w[2:3, :] * u
        bz_ref[...] = (bb * z).astype(bz_ref.dtype)
        tail = u[CONV_TM - 2:CONV_TM, :]
        carry_ref[0:2, :] = tail
        stp_ref[...] = tail

    @pl.when(i == CONV_PROMPT_TILES)
    def _():
        bb, u = project(xs_ref[...])
        b1 = b1_ref[...]
        z = w[0:1, :] * b0_ref[...]
        z = z + w[1:2, :] * b1
        z = z + w[2:3, :] * u
        bz_ref[0:M_SAMPLE, :] = (bb * z).astype(bz_ref.dtype)
        s0_ref[...] = b1
        s1_ref[...] = u


def conv_mixer(h, w_in, conv_w, buf):
    tm, tc = CONV_TM, CONV_TC
    nj = D // tc
    last_tile = CONV_PROMPT_TILES - 1
    last_seq = N_PROMPT_SEQ - 1
    wspec = lambda part: pl.BlockSpec((None, D, tc), lambda j, i: (0, 0, part * nj + j))
    return pl.pallas_call(
        _conv_kernel,
        grid=(nj, CONV_PROMPT_TILES + 1),
        in_specs=[pl.BlockSpec((tm, D), lambda j, i: (jnp.minimum(i, last_tile), 0)),
                  pl.BlockSpec((M_SAMPLE, D), lambda j, i: (M_PROMPT // M_SAMPLE, 0)),
                  wspec(0), wspec(1), wspec(2),
                  pl.BlockSpec((3, tc), lambda j, i: (0, j)),
                  pl.BlockSpec((M_SAMPLE, tc), lambda j, i: (0, j)),
                  pl.BlockSpec((M_SAMPLE, tc), lambda j, i: (0, nj + j))],
        out_specs=[pl.BlockSpec((tm, tc), lambda j, i: (i, j)),
                   pl.BlockSpec((None, 2, tc),
                                lambda j, i: (jnp.minimum(i // CONV_TILES_PER_SEQ, last_seq), 0, j)),
                   pl.BlockSpec((M_SAMPLE, tc), lambda j, i: (0, j)),
                   pl.BlockSpec((M_SAMPLE, tc), lambda j, i: (0, j))],
        out_shape=[jax.ShapeDtypeStruct((M_TOK, D), BF16),
                   jax.ShapeDtypeStruct((N_PROMPT_SEQ, 2, D), F32),
                   jax.ShapeDtypeStruct((M_SAMPLE, D), F32),
                   jax.ShapeDtypeStruct((M_SAMPLE, D), F32)],
        scratch_shapes=[pltpu.VMEM((3, D, tc), BF16), pltpu.VMEM((8, tc), F32)],
        compiler_params=_params(("arbitrary", "arbitrary"), 56),
        name="conv_mixer",
    )(h, h, w_in, w_in, w_in, conv_w, buf, buf)


def _log_sigmoid(x):
    return jnp.minimum(x, 0.0) - jnp.log1p(jnp.exp(-jnp.abs(x)))


def _gla_gate_kernel(h_ref, wa_ref, wg_ref, bg_ref, g_ref):
    a = jnp.dot(h_ref[...], wa_ref[...].astype(BF16), preferred_element_type=F32)
    z = jnp.dot(a.astype(BF16), wg_ref[...].astype(BF16), preferred_element_type=F32) + bg_ref[...]
    g_ref[...] = _log_sigmoid(z) * (1.0 / GATE_TAU)


def gla_gate(h, w_a_pad, w_gate_pad, b_gate, tm=TM_DENSE):
    return pl.pallas_call(
        _gla_gate_kernel,
        grid=(M_TOK // tm,),
        in_specs=[pl.BlockSpec((tm, D), lambda i: (i, 0)),
                  pl.BlockSpec((D, LANES), lambda i: (0, 0)),
                  pl.BlockSpec((LANES, HK), lambda i: (0, 0)),
                  pl.BlockSpec((1, HK), lambda i: (0, 0))],
        out_specs=pl.BlockSpec((tm, HK), lambda i: (i, 0)),
        out_shape=jax.ShapeDtypeStruct((M_TOK, HK), F32),
        compiler_params=_params(("arbitrary",), 40),
        name="gla_gate",
    )(h, w_a_pad, w_gate_pad, b_gate)


def _row_to_cols(row):
    return jnp.transpose(jnp.broadcast_to(row, (LANES, row.shape[1])))


def _split3_bf16(x):
    x1 = x.astype(BF16)
    r1 = x - x1.astype(F32)
    x2 = r1.astype(BF16)
    x3 = (r1 - x2.astype(F32)).astype(BF16)
    return x1, x2, x3


def _gla_prompt_kernel(o_init, q_ref, k_ref, v_ref, r_ref, g_ref, gn_ref, o_ref, sout_ref, s_ref):
    del o_init
    c = pl.program_id(1)

    @pl.when(c == 0)
    def _():
        s_ref[...] = jnp.zeros_like(s_ref)

    row = lax.broadcasted_iota(jnp.int32, (CHUNK, CHUNK), 0)
    col = lax.broadcasted_iota(jnp.int32, (CHUNK, CHUNK), 1)
    tri = row >= col
    trib = tri.astype(BF16)

    g1, g2, g3 = _split3_bf16(g_ref[...])
    b = (jnp.dot(trib, g1, preferred_element_type=F32)
         + jnp.dot(trib, g2, preferred_element_type=F32)
         + jnp.dot(trib, g3, preferred_element_type=F32))
    b_last = b[CHUNK - 1:CHUNK, :]
    q = q_ref[...] * (DK ** -0.5)
    k = k_ref[...]
    q_dec = (q * jnp.exp(b)).astype(BF16)
    k_inv = (k * jnp.exp(-b)).astype(BF16)
    k_end = (k * jnp.exp(b_last - b)).astype(BF16)
    decay = jnp.exp(b_last)
    gn = gn_ref[...]

    for h in range(HEADS):
        ks = slice(h * DK, (h + 1) * DK)
        vs = slice(h * DV, (h + 1) * DV)
        vb = v_ref[:, vs].astype(BF16)
        s_old = s_ref[h]
        scores = lax.dot_general(q_dec[:, ks], k_inv[:, ks], (((1,), (1,)), ((), ())),
                                 preferred_element_type=F32)
        scores = jnp.where(tri, scores, 0.0).astype(BF16)
        o = (jnp.dot(scores, vb, preferred_element_type=F32)
             + jnp.dot(q_dec[:, ks], s_old.astype(BF16), preferred_element_type=F32))
        kv = lax.dot_general(k_end[:, ks], vb, (((0,), (0,)), ((), ())),
                             preferred_element_type=F32)
        dcol = _row_to_cols(decay[:, ks])
        s_ref[h] = jnp.concatenate(
            [s_old[:, j * LANES:(j + 1) * LANES] * dcol for j in range(DV // LANES)], axis=1) + kv
        on = _rmsnorm(o, gn)
        o_ref[:, vs] = (_silu(r_ref[:, vs]) * on).astype(o_ref.dtype)

    @pl.when(c == N_CHUNKS - 1)
    def _():
        sout_ref[...] = s_ref[...]


def gla_prompt(p, g, g_norm):
    rows = lambda n, c: n * N_CHUNKS + c
    return pl.pallas_call(
        _gla_prompt_kernel,
        grid=(N_PROMPT_SEQ, N_CHUNKS),
        in_specs=[pl.BlockSpec(memory_space=pl.ANY),
                  pl.BlockSpec((CHUNK, HK), lambda n, c: (rows(n, c), 0)),
                  pl.BlockSpec((CHUNK, HK), lambda n, c: (rows(n, c), 1)),
                  pl.BlockSpec((CHUNK, HV), lambda n, c: (rows(n, c), 1)),
                  pl.BlockSpec((CHUNK, HV), lambda n, c: (rows(n, c), 2)),
                  pl.BlockSpec((CHUNK, HK), lambda n, c: (rows(n, c), 0)),
                  pl.BlockSpec((1, DV), lambda n, c: (0, 0))],
        out_specs=[pl.BlockSpec((CHUNK, HV), lambda n, c: (rows(n, c), 0)),
                   pl.BlockSpec((None, HEADS, DK, DV), lambda n, c: (n, 0, 0, 0))],
        out_shape=[jax.ShapeDtypeStruct((M_TOK, HV), BF16),
                   jax.ShapeDtypeStruct((N_PROMPT_SEQ, HEADS, DK, DV), F32)],
        scratch_shapes=[pltpu.VMEM((HEADS, DK, DV), F32)],
        input_output_aliases={0: 0},
        compiler_params=_params(("arbitrary", "arbitrary"), 40),
        name="gla_prompt",
    )(jnp.zeros((M_TOK, HV), BF16), p, p, p, p, g, g_norm)


GLA_SAMPLE_SEQS = 2


def _gla_sample_kernel(gated_any, q_ref, k_ref, v_ref, r_ref, g_ref, gn_ref, s_ref,
                       gated_ref, sout_ref, qt_ref, kt_ref, et_ref, o_scr):
    del gated_any
    i = pl.program_id(0)

    @pl.when(i == 0)
    def _():
        q = q_ref[...] * (DK ** -0.5)
        k = k_ref[...]
        e = jnp.exp(g_ref[...])
        for h in range(HEADS):
            ks = slice(h * DK, (h + 1) * DK)
            qt_ref[h] = jnp.transpose(q[:, ks])
            kt_ref[h] = jnp.transpose(k[:, ks])
            et_ref[h] = jnp.transpose(e[:, ks])

    lane = lax.broadcasted_iota(jnp.int32, (DK, M_SAMPLE), 1)
    for s in range(GLA_SAMPLE_SEQS):
        n = i * GLA_SAMPLE_SEQS + s
        pick = lane == n

        def column(t):
            return jnp.sum(jnp.where(pick, t, 0.0), axis=1, keepdims=True)

        for h in range(HEADS):
            vs = slice(h * DV, (h + 1) * DV)
            v_row = v_ref[pl.ds(n, 1), vs]
            s_new = s_ref[s, h] * column(et_ref[h]) + column(kt_ref[h]) * v_row
            sout_ref[s, h] = s_new
            o_scr[pl.ds(n, 1), vs] = jnp.sum(column(qt_ref[h]) * s_new, axis=0, keepdims=True)

    @pl.when(i == pl.num_programs(0) - 1)
    def _():
        gn = gn_ref[...]
        for h in range(HEADS):
            vs = slice(h * DV, (h + 1) * DV)
            on = _rmsnorm(o_scr[:, vs], gn)
            gated_ref[:, vs] = (_silu(r_ref[:, vs]) * on).astype(gated_ref.dtype)


def gla_sample(gated, p, g, g_norm, state):
    rb = M_PROMPT // M_SAMPLE
    bs = GLA_SAMPLE_SEQS
    return pl.pallas_call(
        _gla_sample_kernel,
        grid=(M_SAMPLE // bs,),
        in_specs=[pl.BlockSpec(memory_space=pl.ANY),
                  pl.BlockSpec((M_SAMPLE, HK), lambda i: (rb, 0)),
                  pl.BlockSpec((M_SAMPLE, HK), lambda i: (rb, 1)),
                  pl.BlockSpec((M_SAMPLE, HV), lambda i: (rb, 1)),
                  pl.BlockSpec((M_SAMPLE, HV), lambda i: (rb, 2)),
                  pl.BlockSpec((M_SAMPLE, HK), lambda i: (rb, 0)),
                  pl.BlockSpec((1, DV), lambda i: (0, 0)),
                  pl.BlockSpec((bs, HEADS, DK, DV), lambda i: (i, 0, 0, 0))],
        out_specs=[pl.BlockSpec((M_SAMPLE, HV), lambda i: (rb, 0)),
                   pl.BlockSpec((bs, HEADS, DK, DV), lambda i: (i, 0, 0, 0))],
        out_shape=[jax.ShapeDtypeStruct((M_TOK, HV), BF16),
                   jax.ShapeDtypeStruct((M_SAMPLE, HEADS, DK, DV), F32)],
        scratch_shapes=[pltpu.VMEM((HEADS, DK, M_SAMPLE), F32),
                        pltpu.VMEM((HEADS, DK, M_SAMPLE), F32),
                        pltpu.VMEM((HEADS, DK, M_SAMPLE), F32),
                        pltpu.VMEM((M_SAMPLE, HV), F32)],
        input_output_aliases={0: 0},
        compiler_params=_params(("arbitrary",), 40),
        name="gla_sample",
    )(gated, p, p, p, p, g, g_norm, state)


def _router_kernel(x_ref, g_ref, wr_ref, route_ref, idx_ref):
    h = _rmsnorm(x_ref[...], g_ref[...])
    logits = jnp.dot(h, wr_ref[...], preferred_element_type=F32, precision=lax.Precision.HIGHEST)
    lane = lax.broadcasted_iota(jnp.int32, logits.shape, 1)
    lane_f = lane.astype(F32)
    neg = jnp.float32(-jnp.inf)
    logits = jnp.where(lane < N_EXPERTS, logits, neg)
    m1 = jnp.max(logits, axis=1, keepdims=True)
    i1 = jnp.min(jnp.where(logits == m1, lane_f, float(LANES)), axis=1, keepdims=True)
    rest = jnp.where(lane_f == i1, neg, logits)
    m2 = jnp.max(rest, axis=1, keepdims=True)
    i2 = jnp.min(jnp.where(rest == m2, lane_f, float(LANES)), axis=1, keepdims=True)
    e2 = jnp.exp(m2 - m1)
    den = 1.0 + e2
    w1 = 1.0 / den
    w2 = e2 / den
    route_ref[...] = jnp.where(lane == 0, w1, jnp.where(lane == 1, w2, 0.0))
    idx_ref[...] = jnp.where(lane == 0, i1, jnp.where(lane == 1, i2, 0.0)).astype(jnp.int32)


def router(x, g, w_router_pad, tm=416):
    return pl.pallas_call(
        _router_kernel,
        grid=(M_TOK // tm,),
        in_specs=[pl.BlockSpec((tm, D), lambda i: (i, 0)),
                  pl.BlockSpec((1, D), lambda i: (0, 0)),
                  pl.BlockSpec((D, LANES), lambda i: (0, 0))],
        out_specs=[pl.BlockSpec((tm, LANES), lambda i: (i, 0)),
                   pl.BlockSpec((tm, LANES), lambda i: (i, 0))],
        out_shape=[jax.ShapeDtypeStruct((M_TOK, LANES), F32),
                   jax.ShapeDtypeStruct((M_TOK, LANES), jnp.int32)],
        compiler_params=_params(("arbitrary",), 40),
        name="router",
    )(x, g.reshape(1, D), w_router_pad)


def _gather_kernel(tok_ref, nvalid_ref, src_hbm, g_ref, o_ref, buf, sem):
    i = pl.program_id(0)
    nv = nvalid_ref[0]
    tm = buf.shape[1]
    slot = lax.rem(i, 2)

    def issue(tile, dst_slot):
        base = tile * tm

        def body(r2, carry):
            for k in range(2):
                r = 2 * r2 + k
                t = tok_ref[base + r]
                pltpu.make_async_copy(src_hbm.at[pl.ds(t, 1)], buf.at[dst_slot, pl.ds(r, 1)],
                                      sem.at[dst_slot]).start(priority=k)
            return carry

        lax.fori_loop(0, tm // 2, body, 0, unroll=4)

    @pl.when(i == 0)
    def _():
        issue(0, 0)

    @pl.when(i + 1 < nv)
    def _():
        issue(i + 1, 1 - slot)

    @pl.when(i < nv)
    def _():
        pltpu.make_async_copy(src_hbm.at[pl.ds(0, tm)], buf.at[slot], sem.at[slot]).wait()
        o_ref[...] = _rmsnorm(buf[slot], g_ref[...]).astype(o_ref.dtype)

    @pl.when(i >= nv)
    def _():
        o_ref[...] = jnp.zeros_like(o_ref)


def gather_rows(row_token, n_valid_tiles, src, g, tm=GROUP_ROWS):
    return pl.pallas_call(
        _gather_kernel,
        grid_spec=pltpu.PrefetchScalarGridSpec(
            num_scalar_prefetch=2,
            grid=(R_PAD // tm,),
            in_specs=[pl.BlockSpec(memory_space=pl.ANY),
                      pl.BlockSpec((1, D), lambda i, tok, nv: (0, 0))],
            out_specs=pl.BlockSpec((tm, D), lambda i, tok, nv: (i, 0)),
            scratch_shapes=[pltpu.VMEM((2, tm, D), F32), pltpu.SemaphoreType.DMA((2,))]),
        out_shape=jax.ShapeDtypeStruct((R_PAD, D), BF16),
        compiler_params=_params(("arbitrary",), 32),
        name="moe_gather",
    )(row_token, n_valid_tiles, src, g.reshape(1, D))


def _grouped_kernel(tstart, ntiles, x_hbm, *refs, n_w, n_col, total_tiles):
    w_hbm = refs[:n_w]
    o_hbm = refs[n_w]
    wbf, stage, xbuf, obuf, xsem, osem, wsem = refs[n_w + 1:]
    s = pl.program_id(0)
    n_items = pl.num_programs(0)
    e = s // n_col
    j = lax.rem(s, n_col)
    p = lax.rem(s, 2)
    tm = xbuf.shape[1]
    tn = obuf.shape[2]
    ck = stage.shape[2]
    n_chunks = wbf.shape[2] // ck
    nt = ntiles[e]
    t0 = tstart[e]
    col = pl.multiple_of(j * tn, tn)
    has_next = s + 1 < n_items

    def w_copies(item, c, q):
        row = pl.multiple_of(c * ck, ck)
        wcol = pl.multiple_of(lax.rem(item, n_col) * tn, tn)
        return [pltpu.make_async_copy(w.at[0, item // n_col, pl.ds(row, ck), pl.ds(wcol, tn)],
                                      stage.at[q, i], wsem.at[q, i])
                for i, w in enumerate(w_hbm)]

    def w_start(item, c, q):
        for cp in w_copies(item, c, q):
            cp.start()

    def w_chunk(item, c, dst):
        q = lax.rem(c, 2)

        @pl.when(c + 1 < n_chunks)
        def _():
            w_start(item, c + 1, 1 - q)

        row = pl.multiple_of(c * ck, ck)
        for i, cp in enumerate(w_copies(item, c, q)):
            cp.wait()
            wbf[dst, i, pl.ds(row, ck), :] = stage[q, i].astype(BF16)

    @pl.when(s == 0)
    def _():
        w_start(0, 0, 0)

        def first(c, carry):
            w_chunk(0, c, 0)
            return carry

        lax.fori_loop(0, n_chunks, first, 0)

    @pl.when(has_next)
    def _():
        w_start(s + 1, 0, 0)

    def x_copy(tile, slot):
        row = pl.multiple_of(tile * tm, tm)
        return pltpu.make_async_copy(x_hbm.at[pl.ds(row, tm)], xbuf.at[slot], xsem.at[slot])

    def o_copy(tile, slot):
        row = pl.multiple_of(tile * tm, tm)
        return pltpu.make_async_copy(obuf.at[slot], o_hbm.at[pl.ds(row, tm), pl.ds(col, tn)],
                                     osem.at[slot])

    @pl.when(nt > 0)
    def _():
        x_copy(t0, 0).start()

    def body(t, chunks_done):
        slot = lax.rem(t, 2)
        x_copy(t0 + t, slot).wait()

        @pl.when(t + 1 < nt)
        def _():
            x_copy(t0 + t + 1, 1 - slot).start()

        @pl.when(t >= 2)
        def _():
            o_copy(t0 + t - 2, slot).wait()

        x = xbuf[slot]
        if n_w == 2:
            g = jnp.dot(x, wbf[p, 0], preferred_element_type=F32)
            u = jnp.dot(x, wbf[p, 1], preferred_element_type=F32)
            obuf[slot] = (_silu(g) * u).astype(obuf.dtype)
        else:
            obuf[slot] = jnp.dot(x, wbf[p, 0], preferred_element_type=F32).astype(obuf.dtype)
        o_copy(t0 + t, slot).start()

        stream = has_next & (chunks_done < n_chunks)

        @pl.when(stream)
        def _():
            w_chunk(s + 1, chunks_done, 1 - p)

        return chunks_done + stream.astype(jnp.int32)

    chunks_done = lax.fori_loop(0, nt, body, jnp.int32(0))

    @pl.when(has_next)
    def _():
        def rest(c, carry):
            w_chunk(s + 1, c, 1 - p)
            return carry

        lax.fori_loop(chunks_done, n_chunks, rest, 0)

    @pl.when(nt >= 2)
    def _():
        o_copy(t0 + nt - 2, lax.rem(nt, 2)).wait()

    @pl.when(nt >= 1)
    def _():
        o_copy(t0 + nt - 1, lax.rem(nt + 1, 2)).wait()

    @pl.when(e == N_EXPERTS - 1)
    def _():
        obuf[0] = jnp.zeros(obuf.shape[1:], obuf.dtype)

        def zero_tile(tile, carry):
            cp = o_copy(tile, 0)
            cp.start()
            cp.wait()
            return carry

        lax.fori_loop(t0 + nt, total_tiles, zero_tile, 0)


def grouped_matmul(x, ws, tstart, ntiles, tm, tn, ck, out_dtype, vmem_mb, name):
    k = x.shape[1]
    n = ws[0].shape[-1]
    n_w = len(ws)
    n_col = n // tn
    any_spec = pl.BlockSpec(memory_space=pl.ANY)
    return pl.pallas_call(
        functools.partial(_grouped_kernel, n_w=n_w, n_col=n_col, total_tiles=R_PAD // tm),
        grid_spec=pltpu.PrefetchScalarGridSpec(
            num_scalar_prefetch=2,
            grid=(N_EXPERTS * n_col,),
            in_specs=[any_spec] * (1 + n_w),
            out_specs=any_spec,
            scratch_shapes=[pltpu.VMEM((2, n_w, k, tn), BF16), pltpu.VMEM((2, n_w, ck, tn), F32),
                            pltpu.VMEM((2, tm, k), BF16), pltpu.VMEM((2, tm, tn), out_dtype),
                            pltpu.SemaphoreType.DMA((2,)), pltpu.SemaphoreType.DMA((2,)),
                            pltpu.SemaphoreType.DMA((2, n_w))]),
        out_shape=jax.ShapeDtypeStruct((R_PAD, n), out_dtype),
        compiler_params=_params(("arbitrary",), vmem_mb),
        name=name,
    )(tstart, ntiles, x, *ws)


COMBINE_ROWS = 128
COMBINE_PROMPT_STEPS = M_PROMPT // COMBINE_ROWS


def _combine_kernel(p1_ref, p2_ref, x_ref, route_ref, g_ref, y_hbm, op_ref, os_ref, b1, b2, sem):
    i = pl.program_id(0)
    tm = b1.shape[1]
    slot = lax.rem(i, 2)

    def issue(tile, dst_slot):
        base = tile * tm

        def body(r, carry):
            pltpu.make_async_copy(y_hbm.at[pl.ds(p1_ref[base + r], 1)], b1.at[dst_slot, pl.ds(r, 1)],
                                  sem.at[0, dst_slot]).start(priority=0)
            pltpu.make_async_copy(y_hbm.at[pl.ds(p2_ref[base + r], 1)], b2.at[dst_slot, pl.ds(r, 1)],
                                  sem.at[1, dst_slot]).start(priority=1)
            return carry

        lax.fori_loop(0, tm, body, 0, unroll=8)

    @pl.when(i == 0)
    def _():
        issue(0, 0)

    @pl.when(i + 1 < pl.num_programs(0))
    def _():
        issue(i + 1, 1 - slot)

    pltpu.make_async_copy(y_hbm.at[pl.ds(0, tm)], b1.at[slot], sem.at[0, slot]).wait()
    pltpu.make_async_copy(y_hbm.at[pl.ds(0, tm)], b2.at[slot], sem.at[1, slot]).wait()
    route = route_ref[...]
    w1 = route[:, 0:1]
    w2 = route[:, 1:2]
    x = x_ref[...] + (w1 * b1[slot] + w2 * b2[slot])
    out = _rmsnorm(x, g_ref[...])

    @pl.when(i < COMBINE_PROMPT_STEPS)
    def _():
        op_ref[...] = out

    @pl.when(i == COMBINE_PROMPT_STEPS)
    def _():
        os_ref[...] = out


def moe_combine(pos1, pos2, x, route, g_final, y):
    tm = COMBINE_ROWS
    last_prompt = COMBINE_PROMPT_STEPS - 1
    return pl.pallas_call(
        _combine_kernel,
        grid_spec=pltpu.PrefetchScalarGridSpec(
            num_scalar_prefetch=2,
            grid=(M_TOK // tm,),
            in_specs=[pl.BlockSpec((tm, D), lambda i, p1, p2: (i, 0)),
                      pl.BlockSpec((tm, LANES), lambda i, p1, p2: (i, 0)),
                      pl.BlockSpec((1, D), lambda i, p1, p2: (0, 0)),
                      pl.BlockSpec(memory_space=pl.ANY)],
            out_specs=[pl.BlockSpec((tm, D), lambda i, p1, p2: (jnp.minimum(i, last_prompt), 0)),
                       pl.BlockSpec((M_SAMPLE, D), lambda i, p1, p2: (0, 0))],
            scratch_shapes=[pltpu.VMEM((2, tm, D), F32), pltpu.VMEM((2, tm, D), F32),
                            pltpu.SemaphoreType.DMA((2, 2))]),
        out_shape=[jax.ShapeDtypeStruct((M_PROMPT, D), F32),
                   jax.ShapeDtypeStruct((M_SAMPLE, D), F32)],
        compiler_params=_params(("arbitrary",), 40),
        name="moe_combine",
    )(pos1, pos2, x, route, g_final.reshape(1, D), y)


def _group_tables(idx):
    e_flat = jnp.concatenate([idx[:, 0], idx[:, 1]])
    onehot = (e_flat[:, None] == jnp.arange(N_EXPERTS, dtype=jnp.int32)[None, :]).astype(jnp.int32)
    csum = jnp.cumsum(onehot, axis=0)
    counts = csum[-1]
    rank = jnp.sum(csum * onehot, axis=1) - 1
    ntiles = (counts + GROUP_ROWS - 1) // GROUP_ROWS
    tile_end = jnp.cumsum(ntiles)
    tstart = tile_end - ntiles
    dest = jnp.sum(onehot * (tstart * GROUP_ROWS)[None, :], axis=1) + rank
    token = jnp.concatenate([jnp.arange(M_TOK, dtype=jnp.int32)] * 2)
    row_token = jnp.zeros((R_PAD,), jnp.int32).at[dest].set(token)
    return row_token, tile_end[-1:], dest[:M_TOK], dest[M_TOK:], tstart, ntiles


def kernel(x_prompt, x_sample, state_conv, state_gla, norm_mix, norm_ffn, norm_final,
           conv_w_in, conv_w, conv_w_out, gla_w_in, gla_w_gate, gla_b_gate, gla_norm, gla_w_out,
           ffn_w_gate, ffn_w_up, ffn_w_down, moe_w_router, moe_w_gate, moe_w_up, moe_w_down):
    x0, h = stack_norm(x_prompt.reshape(M_PROMPT, D), x_sample.reshape(M_SAMPLE, D), norm_mix[0])
    bz, conv_prompt_state, s0, s1 = conv_mixer(
        h, conv_w_in, conv_w[0], state_conv[0].reshape(M_SAMPLE, 2 * D))
    conv_sample_state = jnp.stack([s0, s1], axis=1).reshape(1, M_SAMPLE, 2, D)
    x1, h = linear_res_norm(bz, conv_w_out, 0, x0, norm_ffn[0], name="conv_out")

    a = swiglu_up(h, ffn_w_gate, ffn_w_up, 0)
    x2 = linear(a, ffn_w_down, 0, D, tn=512, tm=640, res=x1, name="ffn_down")

    h = rmsnorm_bf16(x2, norm_mix[1])
    p = linear(h, jnp.swapaxes(gla_w_in, 1, 2), 0, 2 * HK + 2 * HV, tn=1024, tm=TM_DENSE,
               name="gla_in", w_is_nk=True)
    w_gate_pad = jnp.pad(gla_w_gate[0], ((0, LANES - GATE_RANK), (0, 0)))
    w_a_pad = jnp.pad(gla_w_in[0, :, 2 * HK + 2 * HV:], ((0, 0), (0, LANES - GATE_RANK)))
    g = gla_gate(h, w_a_pad, w_gate_pad, gla_b_gate[0].reshape(1, HK))
    gn = gla_norm[0].reshape(1, DV)
    gated, gla_prompt_state = gla_prompt(p, g, gn)
    gated, gla_sample_state = gla_sample(gated, p, g, gn, state_gla[0])
    x3 = linear(gated, gla_w_out, 0, D, tn=1024, tm=TM_DENSE, res=x2, name="gla_out")

    w_router_pad = jnp.pad(moe_w_router[0], ((0, 0), (0, LANES - N_EXPERTS)))
    route, idx = router(x3, norm_ffn[1], w_router_pad)
    row_token, n_valid, pos1, pos2, tstart, ntiles = _group_tables(idx)
    xs = gather_rows(row_token, n_valid, x3, norm_ffn[1])
    act = grouped_matmul(xs, (moe_w_gate, moe_w_up), tstart, ntiles, tm=GROUP_ROWS, tn=1792, ck=256,
                         out_dtype=BF16, vmem_mb=56, name="moe_up")
    y = grouped_matmul(act, (moe_w_down,), tstart, ntiles, tm=GROUP_ROWS, tn=1024, ck=896,
                       out_dtype=F32, vmem_mb=56, name="moe_down")
    y_prompt, y_sample = moe_combine(pos1, pos2, x3, route, norm_final, y)

    y_prompt = y_prompt.reshape(N_PROMPT_SEQ, SEQ, D)
    y_sample = y_sample.reshape(M_SAMPLE, 1, D)
    return (y_prompt, y_sample,
            conv_prompt_state.reshape(1, N_PROMPT_SEQ, 2, D), conv_sample_state,
            gla_prompt_state.reshape(1, N_PROMPT_SEQ, HEADS, DK, DV),
            gla_sample_state.reshape(1, M_SAMPLE, HEADS, DK, DV))
```

```python
import functools

import jax
import jax.numpy as jnp
from jax import lax
from jax.experimental import pallas as pl
from jax.experimental.pallas import tpu as pltpu

F32 = jnp.float32
BF16 = jnp.bfloat16

D = 2048
N_PROMPT_SEQ = 4
SEQ = 2048
M_PROMPT = N_PROMPT_SEQ * SEQ
M_SAMPLE = 128
M_TOK = M_PROMPT + M_SAMPLE
HEADS = 4
DK = 256
DV = 512
HK = HEADS * DK
HV = HEADS * DV
GATE_RANK = 16
GATE_TAU = 16.0
CHUNK = 64
N_CHUNKS = SEQ // CHUNK
D_FF = 5632
N_EXPERTS = 8
D_FF_EXPERT = 7168
EPS = 1e-6
LANES = 128

TM_DENSE = 1040
GROUP_ROWS = 256
N_ASSIGN = 2 * M_TOK
N_GROUP_TILES = N_ASSIGN // GROUP_ROWS + N_EXPERTS
R_PAD = N_GROUP_TILES * GROUP_ROWS


def _params(sem, vmem_mb):
    return pltpu.CompilerParams(dimension_semantics=sem,
                                vmem_limit_bytes=vmem_mb * 1024 * 1024)


def _rmsnorm(x, g):
    return x * lax.rsqrt(jnp.mean(x * x, axis=-1, keepdims=True) + EPS) * g


def _silu(x):
    return x * jax.nn.sigmoid(x)


def _norm_kernel(x_ref, g_ref, o_ref):
    o_ref[...] = _rmsnorm(x_ref[...], g_ref[...]).astype(o_ref.dtype)


def rmsnorm_bf16(x, g, tr=832):
    m = x.shape[0]
    return pl.pallas_call(
        _norm_kernel,
        grid=(m // tr,),
        in_specs=[pl.BlockSpec((tr, D), lambda i: (i, 0)),
                  pl.BlockSpec((1, D), lambda i: (0, 0))],
        out_specs=pl.BlockSpec((tr, D), lambda i: (i, 0)),
        out_shape=jax.ShapeDtypeStruct((m, D), BF16),
        compiler_params=_params(("arbitrary",), 40),
        name="rmsnorm",
    )(x, g.reshape(1, D))


STACK_ROWS = 512
STACK_PROMPT_STEPS = M_PROMPT // STACK_ROWS


def _stack_norm_kernel(xp_ref, xs_ref, g_ref, x_ref, h_ref):
    i = pl.program_id(0)
    g = g_ref[...]

    @pl.when(i < STACK_PROMPT_STEPS)
    def _():
        x = xp_ref[...]
        x_ref[...] = x
        h_ref[...] = _rmsnorm(x, g).astype(h_ref.dtype)

    @pl.when(i == STACK_PROMPT_STEPS)
    def _():
        x = xs_ref[...]
        x_ref[0:M_SAMPLE, :] = x
        h_ref[0:M_SAMPLE, :] = _rmsnorm(x, g).astype(h_ref.dtype)


def stack_norm(xp, xs, g):
    last_prompt = STACK_PROMPT_STEPS - 1
    return pl.pallas_call(
        _stack_norm_kernel,
        grid=(STACK_PROMPT_STEPS + 1,),
        in_specs=[pl.BlockSpec((STACK_ROWS, D), lambda i: (jnp.minimum(i, last_prompt), 0)),
                  pl.BlockSpec((M_SAMPLE, D), lambda i: (0, 0)),
                  pl.BlockSpec((1, D), lambda i: (0, 0))],
        out_specs=[pl.BlockSpec((STACK_ROWS, D), lambda i: (i, 0)),
                   pl.BlockSpec((STACK_ROWS, D), lambda i: (i, 0))],
        out_shape=[jax.ShapeDtypeStruct((M_TOK, D), F32),
                   jax.ShapeDtypeStruct((M_TOK, D), BF16)],
        compiler_params=_params(("arbitrary",), 40),
        name="stack_norm",
    )(xp, xs, g.reshape(1, D))


def _linear_kernel(*refs, has_res, w_is_nk):
    if has_res:
        x_ref, w_ref, r_ref, o_ref, wb_ref = refs
    else:
        x_ref, w_ref, o_ref, wb_ref = refs

    @pl.when(pl.program_id(1) == 0)
    def _():
        wb_ref[...] = w_ref[...].astype(BF16)

    contract_w = 1 if w_is_nk else 0
    acc = lax.dot_general(x_ref[...], wb_ref[...], (((1,), (contract_w,)), ((), ())),
                          preferred_element_type=F32)
    if has_res:
        acc = acc + r_ref[...]
    o_ref[...] = acc.astype(o_ref.dtype)


def linear(x, w, layer, n_out, tn, tm, res=None, out_dtype=F32, vmem_mb=56, name="linear",
           w_is_nk=False):
    m, k = x.shape
    wblock = (None, tn, k) if w_is_nk else (None, k, tn)
    wmap = (lambda j, i: (layer, j, 0)) if w_is_nk else (lambda j, i: (layer, 0, j))
    in_specs = [pl.BlockSpec((tm, k), lambda j, i: (i, 0)),
                pl.BlockSpec(wblock, wmap)]
    args = [x, w]
    if res is not None:
        in_specs.append(pl.BlockSpec((tm, tn), lambda j, i: (i, j)))
        args.append(res)
    return pl.pallas_call(
        functools.partial(_linear_kernel, has_res=res is not None, w_is_nk=w_is_nk),
        grid=(n_out // tn, m // tm),
        in_specs=in_specs,
        out_specs=pl.BlockSpec((tm, tn), lambda j, i: (i, j)),
        out_shape=jax.ShapeDtypeStruct((m, n_out), out_dtype),
        scratch_shapes=[pltpu.VMEM(wblock[1:], BF16)],
        compiler_params=_params(("arbitrary", "arbitrary"), vmem_mb),
        name=name,
    )(*args)


def _linear_res_norm_kernel(x_ref, w_ref, r_ref, g_ref, o_ref, h_ref, wb_ref):
    @pl.when(pl.program_id(0) == 0)
    def _():
        wb_ref[...] = w_ref[...].astype(BF16)

    y = jnp.dot(x_ref[...], wb_ref[...], preferred_element_type=F32) + r_ref[...]
    o_ref[...] = y
    h_ref[...] = _rmsnorm(y, g_ref[...]).astype(h_ref.dtype)


def linear_res_norm(x, w, layer, res, g, tm=416, name="linear_res_norm"):
    m, k = x.shape
    return pl.pallas_call(
        _linear_res_norm_kernel,
        grid=(m // tm,),
        in_specs=[pl.BlockSpec((tm, k), lambda i: (i, 0)),
                  pl.BlockSpec((None, k, D), lambda i: (layer, 0, 0), pipeline_mode=pl.Buffered(1)),
                  pl.BlockSpec((tm, D), lambda i: (i, 0)),
                  pl.BlockSpec((1, D), lambda i: (0, 0))],
        out_specs=[pl.BlockSpec((tm, D), lambda i: (i, 0)),
                   pl.BlockSpec((tm, D), lambda i: (i, 0))],
        out_shape=[jax.ShapeDtypeStruct((m, D), F32),
                   jax.ShapeDtypeStruct((m, D), BF16)],
        scratch_shapes=[pltpu.VMEM((k, D), BF16)],
        compiler_params=_params(("arbitrary",), 56),
        name=name,
    )(x, w, res, g.reshape(1, D))


def _swiglu_kernel(x_ref, wg_ref, wu_ref, o_ref, wgb_ref, wub_ref):
    @pl.when(pl.program_id(1) == 0)
    def _():
        wgb_ref[...] = wg_ref[...].astype(BF16)
        wub_ref[...] = wu_ref[...].astype(BF16)

    x = x_ref[...]
    g = jnp.dot(x, wgb_ref[...], preferred_element_type=F32)
    u = jnp.dot(x, wub_ref[...], preferred_element_type=F32)
    o_ref[...] = (_silu(g) * u).astype(o_ref.dtype)


def swiglu_up(x, wg, wu, layer, tn=512, tm=TM_DENSE):
    m, k = x.shape
    f = wg.shape[-1]
    wspec = pl.BlockSpec((None, k, tn), lambda j, i: (layer, 0, j))
    return pl.pallas_call(
        _swiglu_kernel,
        grid=(f // tn, m // tm),
        in_specs=[pl.BlockSpec((tm, k), lambda j, i: (i, 0)), wspec, wspec],
        out_specs=pl.BlockSpec((tm, tn), lambda j, i: (i, j)),
        out_shape=jax.ShapeDtypeStruct((m, f), BF16),
        scratch_shapes=[pltpu.VMEM((k, tn), BF16), pltpu.VMEM((k, tn), BF16)],
        compiler_params=_params(("arbitrary", "arbitrary"), 56),
        name="ffn_up",
    )(x, wg, wu)


CONV_TM = 512
CONV_TC = 512
CONV_TILES_PER_SEQ = SEQ // CONV_TM
CONV_PROMPT_TILES = M_PROMPT // CONV_TM


def _conv_kernel(x_ref, xs_ref, wh_ref, wb_ref, wc_ref, cw_ref, b0_ref, b1_ref,
                 bz_ref, stp_ref, s0_ref, s1_ref, wbf_ref, carry_ref):
    i = pl.program_id(1)

    @pl.when(i == 0)
    def _():
        wbf_ref[0] = wh_ref[...].astype(BF16)
        wbf_ref[1] = wb_ref[...].astype(BF16)
        wbf_ref[2] = wc_ref[...].astype(BF16)

    def project(x):
        hh = jnp.dot(x, wbf_ref[0], preferred_element_type=F32)
        bb = jnp.dot(x, wbf_ref[1], preferred_element_type=F32)
        cc = jnp.dot(x, wbf_ref[2], preferred_element_type=F32)
        return bb, cc * hh

    w = cw_ref[...]

    @pl.when(i < CONV_PROMPT_TILES)
    def _():
        @pl.when(i % CONV_TILES_PER_SEQ == 0)
        def _():
            carry_ref[...] = jnp.zeros_like(carry_ref)

        bb, u = project(x_ref[...])
        c2 = carry_ref[0:1, :]
        c1 = carry_ref[1:2, :]
        row = lax.broadcasted_iota(jnp.int32, u.shape, 0)
        u1 = jnp.where(row == 0, c1, pltpu.roll(u, 1, 0))
        u2 = jnp.where(row == 0, c2, jnp.where(row == 1, c1, pltpu.roll(u, 2, 0)))
        z = w[0:1, :] * u2
        z = z + w[1:2, :] * u1
        z = z + w[2:3, :] * u
        bz_ref[...] = (bb * z).astype(bz_ref.dtype)
        tail = u[CONV_TM - 2:CONV_TM, :]
        carry_ref[0:2, :] = tail
        stp_ref[...] = tail

    @pl.when(i == CONV_PROMPT_TILES)
    def _():
        bb, u = project(xs_ref[...])
        b1 = b1_ref[...]
        z = w[0:1, :] * b0_ref[...]
        z = z + w[1:2, :] * b1
        z = z + w[2:3, :] * u
        bz_ref[0:M_SAMPLE, :] = (bb * z).astype(bz_ref.dtype)
        s0_ref[...] = b1
        s1_ref[...] = u


def conv_mixer(h, w_in, conv_w, buf):
    tm, tc = CONV_TM, CONV_TC
    nj = D // tc
    last_tile = CONV_PROMPT_TILES - 1
    last_seq = N_PROMPT_SEQ - 1
    wspec = lambda part: pl.BlockSpec((None, D, tc), lambda j, i: (0, 0, part * nj + j))
    return pl.pallas_call(
        _conv_kernel,
        grid=(nj, CONV_PROMPT_TILES + 1),
        in_specs=[pl.BlockSpec((tm, D), lambda j, i: (jnp.minimum(i, last_tile), 0)),
                  pl.BlockSpec((M_SAMPLE, D), lambda j, i: (M_PROMPT // M_SAMPLE, 0)),
                  wspec(0), wspec(1), wspec(2),
                  pl.BlockSpec((3, tc), lambda j, i: (0, j)),
                  pl.BlockSpec((M_SAMPLE, tc), lambda j, i: (0, j)),
                  pl.BlockSpec((M_SAMPLE, tc), lambda j, i: (0, nj + j))],
        out_specs=[pl.BlockSpec((tm, tc), lambda j, i: (i, j)),
                   pl.BlockSpec((None, 2, tc),
                                lambda j, i: (jnp.minimum(i // CONV_TILES_PER_SEQ, last_seq), 0, j)),
                   pl.BlockSpec((M_SAMPLE, tc), lambda j, i: (0, j)),
                   pl.BlockSpec((M_SAMPLE, tc), lambda j, i: (0, j))],
        out_shape=[jax.ShapeDtypeStruct((M_TOK, D), BF16),
                   jax.ShapeDtypeStruct((N_PROMPT_SEQ, 2, D), F32),
                   jax.ShapeDtypeStruct((M_SAMPLE, D), F32),
                   jax.ShapeDtypeStruct((M_SAMPLE, D), F32)],
        scratch_shapes=[pltpu.VMEM((3, D, tc), BF16), pltpu.VMEM((8, tc), F32)],
        compiler_params=_params(("arbitrary", "arbitrary"), 56),
        name="conv_mixer",
    )(h, h, w_in, w_in, w_in, conv_w, buf, buf)


def _log_sigmoid(x):
    return jnp.minimum(x, 0.0) - jnp.log1p(jnp.exp(-jnp.abs(x)))


def _gla_gate_kernel(h_ref, wa_ref, wg_ref, bg_ref, g_ref):
    a = jnp.dot(h_ref[...], wa_ref[...].astype(BF16), preferred_element_type=F32)
    z = jnp.dot(a.astype(BF16), wg_ref[...].astype(BF16), preferred_element_type=F32) + bg_ref[...]
    g_ref[...] = _log_sigmoid(z) * (1.0 / GATE_TAU)


def gla_gate(h, w_a_pad, w_gate_pad, b_gate, tm=TM_DENSE):
    return pl.pallas_call(
        _gla_gate_kernel,
        grid=(M_TOK // tm,),
        in_specs=[pl.BlockSpec((tm, D), lambda i: (i, 0)),
                  pl.BlockSpec((D, LANES), lambda i: (0, 0)),
                  pl.BlockSpec((LANES, HK), lambda i: (0, 0)),
                  pl.BlockSpec((1, HK), lambda i: (0, 0))],
        out_specs=pl.BlockSpec((tm, HK), lambda i: (i, 0)),
        out_shape=jax.ShapeDtypeStruct((M_TOK, HK), F32),
        compiler_params=_params(("arbitrary",), 40),
        name="gla_gate",
    )(h, w_a_pad, w_gate_pad, b_gate)


def _row_to_cols(row):
    return jnp.transpose(jnp.broadcast_to(row, (LANES, row.shape[1])))


def _split3_bf16(x):
    x1 = x.astype(BF16)
    r1 = x - x1.astype(F32)
    x2 = r1.astype(BF16)
    x3 = (r1 - x2.astype(F32)).astype(BF16)
    return x1, x2, x3


def _gla_prompt_kernel(o_init, q_ref, k_ref, v_ref, r_ref, g_ref, gn_ref, o_ref, sout_ref, s_ref):
    del o_init
    c = pl.program_id(1)

    @pl.when(c == 0)
    def _():
        s_ref[...] = jnp.zeros_like(s_ref)

    row = lax.broadcasted_iota(jnp.int32, (CHUNK, CHUNK), 0)
    col = lax.broadcasted_iota(jnp.int32, (CHUNK, CHUNK), 1)
    tri = row >= col
    trib = tri.astype(BF16)

    g1, g2, g3 = _split3_bf16(g_ref[...])
    b = (jnp.dot(trib, g1, preferred_element_type=F32)
         + jnp.dot(trib, g2, preferred_element_type=F32)
         + jnp.dot(trib, g3, preferred_element_type=F32))
    b_last = b[CHUNK - 1:CHUNK, :]
    q = q_ref[...] * (DK ** -0.5)
    k = k_ref[...]
    q_dec = (q * jnp.exp(b)).astype(BF16)
    k_inv = (k * jnp.exp(-b)).astype(BF16)
    k_end = (k * jnp.exp(b_last - b)).astype(BF16)
    decay = jnp.exp(b_last)
    gn = gn_ref[...]

    for h in range(HEADS):
        ks = slice(h * DK, (h + 1) * DK)
        vs = slice(h * DV, (h + 1) * DV)
        vb = v_ref[:, vs].astype(BF16)
        s_old = s_ref[h]
        scores = lax.dot_general(q_dec[:, ks], k_inv[:, ks], (((1,), (1,)), ((), ())),
                                 preferred_element_type=F32)
        scores = jnp.where(tri, scores, 0.0).astype(BF16)
        o = (jnp.dot(scores, vb, preferred_element_type=F32)
             + jnp.dot(q_dec[:, ks], s_old.astype(BF16), preferred_element_type=F32))
        kv = lax.dot_general(k_end[:, ks], vb, (((0,), (0,)), ((), ())),
                             preferred_element_type=F32)
        dcol = _row_to_cols(decay[:, ks])
        s_ref[h] = jnp.concatenate(
            [s_old[:, j * LANES:(j + 1) * LANES] * dcol for j in range(DV // LANES)], axis=1) + kv
        on = _rmsnorm(o, gn)
        o_ref[:, vs] = (_silu(r_ref[:, vs]) * on).astype(o_ref.dtype)

    @pl.when(c == N_CHUNKS - 1)
    def _():
        sout_ref[...] = s_ref[...]


def gla_prompt(p, g, g_norm):
    rows = lambda n, c: n * N_CHUNKS + c
    return pl.pallas_call(
        _gla_prompt_kernel,
        grid=(N_PROMPT_SEQ, N_CHUNKS),
        in_specs=[pl.BlockSpec(memory_space=pl.ANY),
                  pl.BlockSpec((CHUNK, HK), lambda n, c: (rows(n, c), 0)),
                  pl.BlockSpec((CHUNK, HK), lambda n, c: (rows(n, c), 1)),
                  pl.BlockSpec((CHUNK, HV), lambda n, c: (rows(n, c), 1)),
                  pl.BlockSpec((CHUNK, HV), lambda n, c: (rows(n, c), 2)),
                  pl.BlockSpec((CHUNK, HK), lambda n, c: (rows(n, c), 0)),
                  pl.BlockSpec((1, DV), lambda n, c: (0, 0))],
        out_specs=[pl.BlockSpec((CHUNK, HV), lambda n, c: (rows(n, c), 0)),
                   pl.BlockSpec((None, HEADS, DK, DV), lambda n, c: (n, 0, 0, 0))],
        out_shape=[jax.ShapeDtypeStruct((M_TOK, HV), BF16),
                   jax.ShapeDtypeStruct((N_PROMPT_SEQ, HEADS, DK, DV), F32)],
        scratch_shapes=[pltpu.VMEM((HEADS, DK, DV), F32)],
        input_output_aliases={0: 0},
        compiler_params=_params(("arbitrary", "arbitrary"), 40),
        name="gla_prompt",
    )(jnp.zeros((M_TOK, HV), BF16), p, p, p, p, g, g_norm)


GLA_SAMPLE_SEQS = 2


def _gla_sample_kernel(gated_any, q_ref, k_ref, v_ref, r_ref, g_ref, gn_ref, s_ref,
                       gated_ref, sout_ref, qt_ref, kt_ref, et_ref, o_scr):
    del gated_any
    i = pl.program_id(0)

    @pl.when(i == 0)
    def _():
        q = q_ref[...] * (DK ** -0.5)
        k = k_ref[...]
        e = jnp.exp(g_ref[...])
        for h in range(HEADS):
            ks = slice(h * DK, (h + 1) * DK)
            qt_ref[h] = jnp.transpose(q[:, ks])
            kt_ref[h] = jnp.transpose(k[:, ks])
            et_ref[h] = jnp.transpose(e[:, ks])

    lane = lax.broadcasted_iota(jnp.int32, (DK, M_SAMPLE), 1)
    for s in range(GLA_SAMPLE_SEQS):
        n = i * GLA_SAMPLE_SEQS + s
        pick = lane == n

        def column(t):
            return jnp.sum(jnp.where(pick, t, 0.0), axis=1, keepdims=True)

        for h in range(HEADS):
            vs = slice(h * DV, (h + 1) * DV)
            v_row = v_ref[pl.ds(n, 1), vs]
            s_new = s_ref[s, h] * column(et_ref[h]) + column(kt_ref[h]) * v_row
            sout_ref[s, h] = s_new
            o_scr[pl.ds(n, 1), vs] = jnp.sum(column(qt_ref[h]) * s_new, axis=0, keepdims=True)

    @pl.when(i == pl.num_programs(0) - 1)
    def _():
        gn = gn_ref[...]
        for h in range(HEADS):
            vs = slice(h * DV, (h + 1) * DV)
            on = _rmsnorm(o_scr[:, vs], gn)
            gated_ref[:, vs] = (_silu(r_ref[:, vs]) * on).astype(gated_ref.dtype)


def gla_sample(gated, p, g, g_norm, state):
    rb = M_PROMPT // M_SAMPLE
    bs = GLA_SAMPLE_SEQS
    return pl.pallas_call(
        _gla_sample_kernel,
        grid=(M_SAMPLE // bs,),
        in_specs=[pl.BlockSpec(memory_space=pl.ANY),
                  pl.BlockSpec((M_SAMPLE, HK), lambda i: (rb, 0)),
                  pl.BlockSpec((M_SAMPLE, HK), lambda i: (rb, 1)),
                  pl.BlockSpec((M_SAMPLE, HV), lambda i: (rb, 1)),
                  pl.BlockSpec((M_SAMPLE, HV), lambda i: (rb, 2)),
                  pl.BlockSpec((M_SAMPLE, HK), lambda i: (rb, 0)),
                  pl.BlockSpec((1, DV), lambda i: (0, 0)),
                  pl.BlockSpec((bs, HEADS, DK, DV), lambda i: (i, 0, 0, 0))],
        out_specs=[pl.BlockSpec((M_SAMPLE, HV), lambda i: (rb, 0)),
                   pl.BlockSpec((bs, HEADS, DK, DV), lambda i: (i, 0, 0, 0))],
        out_shape=[jax.ShapeDtypeStruct((M_TOK, HV), BF16),
                   jax.ShapeDtypeStruct((M_SAMPLE, HEADS, DK, DV), F32)],
        scratch_shapes=[pltpu.VMEM((HEADS, DK, M_SAMPLE), F32),
                        pltpu.VMEM((HEADS, DK, M_SAMPLE), F32),
                        pltpu.VMEM((HEADS, DK, M_SAMPLE), F32),
                        pltpu.VMEM((M_SAMPLE, HV), F32)],
        input_output_aliases={0: 0},
        compiler_params=_params(("arbitrary",), 40),
        name="gla_sample",
    )(gated, p, p, p, p, g, g_norm, state)


def _router_kernel(x_ref, g_ref, wr_ref, route_ref, idx_ref):
    h = _rmsnorm(x_ref[...], g_ref[...])
    logits = jnp.dot(h, wr_ref[...], preferred_element_type=F32, precision=lax.Precision.HIGHEST)
    lane = lax.broadcasted_iota(jnp.int32, logits.shape, 1)
    lane_f = lane.astype(F32)
    neg = jnp.float32(-jnp.inf)
    logits = jnp.where(lane < N_EXPERTS, logits, neg)
    m1 = jnp.max(logits, axis=1, keepdims=True)
    i1 = jnp.min(jnp.where(logits == m1, lane_f, float(LANES)), axis=1, keepdims=True)
    rest = jnp.where(lane_f == i1, neg, logits)
    m2 = jnp.max(rest, axis=1, keepdims=True)
    i2 = jnp.min(jnp.where(rest == m2, lane_f, float(LANES)), axis=1, keepdims=True)
    e2 = jnp.exp(m2 - m1)
    den = 1.0 + e2
    w1 = 1.0 / den
    w2 = e2 / den
    route_ref[...] = jnp.where(lane == 0, w1, jnp.where(lane == 1, w2, 0.0))
    idx_ref[...] = jnp.where(lane == 0, i1, jnp.where(lane == 1, i2, 0.0)).astype(jnp.int32)


def router(x, g, w_router_pad, tm=416):
    return pl.pallas_call(
        _router_kernel,
        grid=(M_TOK // tm,),
        in_specs=[pl.BlockSpec((tm, D), lambda i: (i, 0)),
                  pl.BlockSpec((1, D), lambda i: (0, 0)),
                  pl.BlockSpec((D, LANES), lambda i: (0, 0))],
        out_specs=[pl.BlockSpec((tm, LANES), lambda i: (i, 0)),
                   pl.BlockSpec((tm, LANES), lambda i: (i, 0))],
        out_shape=[jax.ShapeDtypeStruct((M_TOK, LANES), F32),
                   jax.ShapeDtypeStruct((M_TOK, LANES), jnp.int32)],
        compiler_params=_params(("arbitrary",), 40),
        name="router",
    )(x, g.reshape(1, D), w_router_pad)


def _gather_kernel(tok_ref, nvalid_ref, src_hbm, g_ref, o_ref, buf, sem):
    i = pl.program_id(0)
    nv = nvalid_ref[0]
    tm = buf.shape[1]
    slot = lax.rem(i, 2)

    def issue(tile, dst_slot):
        base = tile * tm

        def body(r2, carry):
            for k in range(2):
                r = 2 * r2 + k
                t = tok_ref[base + r]
                pltpu.make_async_copy(src_hbm.at[pl.ds(t, 1)], buf.at[dst_slot, pl.ds(r, 1)],
                                      sem.at[dst_slot]).start(priority=k)
            return carry

        lax.fori_loop(0, tm // 2, body, 0, unroll=4)

    @pl.when(i == 0)
    def _():
        issue(0, 0)

    @pl.when(i + 1 < nv)
    def _():
        issue(i + 1, 1 - slot)

    @pl.when(i < nv)
    def _():
        pltpu.make_async_copy(src_hbm.at[pl.ds(0, tm)], buf.at[slot], sem.at[slot]).wait()
        o_ref[...] = _rmsnorm(buf[slot], g_ref[...]).astype(o_ref.dtype)

    @pl.when(i >= nv)
    def _():
        o_ref[...] = jnp.zeros_like(o_ref)


def gather_rows(row_token, n_valid_tiles, src, g, tm=GROUP_ROWS):
    return pl.pallas_call(
        _gather_kernel,
        grid_spec=pltpu.PrefetchScalarGridSpec(
            num_scalar_prefetch=2,
            grid=(R_PAD // tm,),
            in_specs=[pl.BlockSpec(memory_space=pl.ANY),
                      pl.BlockSpec((1, D), lambda i, tok, nv: (0, 0))],
            out_specs=pl.BlockSpec((tm, D), lambda i, tok, nv: (i, 0)),
            scratch_shapes=[pltpu.VMEM((2, tm, D), F32), pltpu.SemaphoreType.DMA((2,))]),
        out_shape=jax.ShapeDtypeStruct((R_PAD, D), BF16),
        compiler_params=_params(("arbitrary",), 32),
        name="moe_gather",
    )(row_token, n_valid_tiles, src, g.reshape(1, D))


def _grouped_kernel(tstart, ntiles, x_hbm, *refs, n_w, n_col, total_tiles):
    w_hbm = refs[:n_w]
    o_hbm = refs[n_w]
    wbf, stage, xbuf, obuf, xsem, osem, wsem = refs[n_w + 1:]
    s = pl.program_id(0)
    n_items = pl.num_programs(0)
    e = s // n_col
    j = lax.rem(s, n_col)
    p = lax.rem(s, 2)
    tm = xbuf.shape[1]
    tn = obuf.shape[2]
    ck = stage.shape[2]
    n_chunks = wbf.shape[2] // ck
    nt = ntiles[e]
    t0 = tstart[e]
    col = pl.multiple_of(j * tn, tn)
    has_next = s + 1 < n_items

    def w_copies(item, c, q):
        row = pl.multiple_of(c * ck, ck)
        wcol = pl.multiple_of(lax.rem(item, n_col) * tn, tn)
        return [pltpu.make_async_copy(w.at[0, item // n_col, pl.ds(row, ck), pl.ds(wcol, tn)],
                                      stage.at[q, i], wsem.at[q, i])
                for i, w in enumerate(w_hbm)]

    def w_start(item, c, q):
        for cp in w_copies(item, c, q):
            cp.start(priority=1)

    def w_chunk(item, c, dst):
        q = lax.rem(c, 2)

        @pl.when(c + 1 < n_chunks)
        def _():
            w_start(item, c + 1, 1 - q)

        row = pl.multiple_of(c * ck, ck)
        for i, cp in enumerate(w_copies(item, c, q)):
            cp.wait()
            wbf[dst, i, pl.ds(row, ck), :] = stage[q, i].astype(BF16)

    @pl.when(s == 0)
    def _():
        w_start(0, 0, 0)

        def first(c, carry):
            w_chunk(0, c, 0)
            return carry

        lax.fori_loop(0, n_chunks, first, 0)

    @pl.when(has_next)
    def _():
        w_start(s + 1, 0, 0)

    def x_copy(tile, slot):
        row = pl.multiple_of(tile * tm, tm)
        return pltpu.make_async_copy(x_hbm.at[pl.ds(row, tm)], xbuf.at[slot], xsem.at[slot])

    def o_copy(tile, slot):
        row = pl.multiple_of(tile * tm, tm)
        return pltpu.make_async_copy(obuf.at[slot], o_hbm.at[pl.ds(row, tm), pl.ds(col, tn)],
                                     osem.at[slot])

    nx = xbuf.shape[0]
    for d in range(nx - 1):
        @pl.when(d < nt)
        def _():
            x_copy(t0 + d, d).start()

    def body(t, chunks_done):
        slot = lax.rem(t, 2)
        xslot = lax.rem(t, nx)
        x_copy(t0 + t, xslot).wait()
        ahead = t + (nx - 1)

        @pl.when(ahead < nt)
        def _():
            x_copy(t0 + ahead, lax.rem(ahead, nx)).start()

        @pl.when(t >= 2)
        def _():
            o_copy(t0 + t - 2, slot).wait()

        x = xbuf[xslot]
        if n_w == 2:
            g = jnp.dot(x, wbf[p, 0], preferred_element_type=F32)
            u = jnp.dot(x, wbf[p, 1], preferred_element_type=F32)
            obuf[slot] = (_silu(g) * u).astype(obuf.dtype)
        else:
            obuf[slot] = jnp.dot(x, wbf[p, 0], preferred_element_type=F32).astype(obuf.dtype)
        o_copy(t0 + t, slot).start()

        stream = has_next & (chunks_done < n_chunks)

        @pl.when(stream)
        def _():
            w_chunk(s + 1, chunks_done, 1 - p)

        return chunks_done + stream.astype(jnp.int32)

    chunks_done = lax.fori_loop(0, nt, body, jnp.int32(0))

    @pl.when(has_next)
    def _():
        def rest(c, carry):
            w_chunk(s + 1, c, 1 - p)
            return carry

        lax.fori_loop(chunks_done, n_chunks, rest, 0)

    @pl.when(nt >= 2)
    def _():
        o_copy(t0 + nt - 2, lax.rem(nt, 2)).wait()

    @pl.when(nt >= 1)
    def _():
        o_copy(t0 + nt - 1, lax.rem(nt + 1, 2)).wait()

    @pl.when(e == N_EXPERTS - 1)
    def _():
        obuf[0] = jnp.zeros(obuf.shape[1:], obuf.dtype)

        def zero_tile(tile, carry):
            cp = o_copy(tile, 0)
            cp.start()
            cp.wait()
            return carry

        lax.fori_loop(t0 + nt, total_tiles, zero_tile, 0)


def grouped_matmul(x, ws, tstart, ntiles, tm, tn, ck, out_dtype, vmem_mb, name):
    k = x.shape[1]
    n = ws[0].shape[-1]
    n_w = len(ws)
    n_col = n // tn
    any_spec = pl.BlockSpec(memory_space=pl.ANY)
    return pl.pallas_call(
        functools.partial(_grouped_kernel, n_w=n_w, n_col=n_col, total_tiles=R_PAD // tm),
        grid_spec=pltpu.PrefetchScalarGridSpec(
            num_scalar_prefetch=2,
            grid=(N_EXPERTS * n_col,),
            in_specs=[any_spec] * (1 + n_w),
            out_specs=any_spec,
            scratch_shapes=[pltpu.VMEM((2, n_w, k, tn), BF16), pltpu.VMEM((2, n_w, ck, tn), F32),
                            pltpu.VMEM((3, tm, k), BF16), pltpu.VMEM((2, tm, tn), out_dtype),
                            pltpu.SemaphoreType.DMA((3,)), pltpu.SemaphoreType.DMA((2,)),
                            pltpu.SemaphoreType.DMA((2, n_w))]),
        out_shape=jax.ShapeDtypeStruct((R_PAD, n), out_dtype),
        compiler_params=_params(("arbitrary",), vmem_mb),
        name=name,
    )(tstart, ntiles, x, *ws)


COMBINE_ROWS = 128
COMBINE_PROMPT_STEPS = M_PROMPT // COMBINE_ROWS


def _combine_kernel(p1_ref, p2_ref, x_ref, route_ref, g_ref, y_hbm, op_ref, os_ref, b1, b2, sem):
    i = pl.program_id(0)
    tm = b1.shape[1]
    slot = lax.rem(i, 2)

    def issue(tile, dst_slot):
        base = tile * tm

        def body(r, carry):
            pltpu.make_async_copy(y_hbm.at[pl.ds(p1_ref[base + r], 1)], b1.at[dst_slot, pl.ds(r, 1)],
                                  sem.at[0, dst_slot]).start(priority=0)
            pltpu.make_async_copy(y_hbm.at[pl.ds(p2_ref[base + r], 1)], b2.at[dst_slot, pl.ds(r, 1)],
                                  sem.at[1, dst_slot]).start(priority=1)
            return carry

        lax.fori_loop(0, tm, body, 0, unroll=8)

    @pl.when(i == 0)
    def _():
        issue(0, 0)

    @pl.when(i + 1 < pl.num_programs(0))
    def _():
        issue(i + 1, 1 - slot)

    pltpu.make_async_copy(y_hbm.at[pl.ds(0, tm)], b1.at[slot], sem.at[0, slot]).wait()
    pltpu.make_async_copy(y_hbm.at[pl.ds(0, tm)], b2.at[slot], sem.at[1, slot]).wait()
    route = route_ref[...]
    w1 = route[:, 0:1]
    w2 = route[:, 1:2]
    x = x_ref[...] + (w1 * b1[slot] + w2 * b2[slot])
    out = _rmsnorm(x, g_ref[...])

    @pl.when(i < COMBINE_PROMPT_STEPS)
    def _():
        op_ref[...] = out

    @pl.when(i == COMBINE_PROMPT_STEPS)
    def _():
        os_ref[...] = out


def moe_combine(pos1, pos2, x, route, g_final, y):
    tm = COMBINE_ROWS
    last_prompt = COMBINE_PROMPT_STEPS - 1
    return pl.pallas_call(
        _combine_kernel,
        grid_spec=pltpu.PrefetchScalarGridSpec(
            num_scalar_prefetch=2,
            grid=(M_TOK // tm,),
            in_specs=[pl.BlockSpec((tm, D), lambda i, p1, p2: (i, 0)),
                      pl.BlockSpec((tm, LANES), lambda i, p1, p2: (i, 0)),
                      pl.BlockSpec((1, D), lambda i, p1, p2: (0, 0)),
                      pl.BlockSpec(memory_space=pl.ANY)],
            out_specs=[pl.BlockSpec((tm, D), lambda i, p1, p2: (jnp.minimum(i, last_prompt), 0)),
                       pl.BlockSpec((M_SAMPLE, D), lambda i, p1, p2: (0, 0))],
            scratch_shapes=[pltpu.VMEM((2, tm, D), F32), pltpu.VMEM((2, tm, D), F32),
                            pltpu.SemaphoreType.DMA((2, 2))]),
        out_shape=[jax.ShapeDtypeStruct((M_PROMPT, D), F32),
                   jax.ShapeDtypeStruct((M_SAMPLE, D), F32)],
        compiler_params=_params(("arbitrary",), 40),
        name="moe_combine",
    )(pos1, pos2, x, route, g_final.reshape(1, D), y)


def _group_tables(idx):
    e_flat = jnp.concatenate([idx[:, 0], idx[:, 1]])
    onehot = (e_flat[:, None] == jnp.arange(N_EXPERTS, dtype=jnp.int32)[None, :]).astype(jnp.int32)
    csum = jnp.cumsum(onehot, axis=0)
    counts = csum[-1]
    rank = jnp.sum(csum * onehot, axis=1) - 1
    ntiles = (counts + GROUP_ROWS - 1) // GROUP_ROWS
    tile_end = jnp.cumsum(ntiles)
    tstart = tile_end - ntiles
    dest = jnp.sum(onehot * (tstart * GROUP_ROWS)[None, :], axis=1) + rank
    token = jnp.concatenate([jnp.arange(M_TOK, dtype=jnp.int32)] * 2)
    row_token = jnp.zeros((R_PAD,), jnp.int32).at[dest].set(token)
    return row_token, tile_end[-1:], dest[:M_TOK], dest[M_TOK:], tstart, ntiles


def kernel(x_prompt, x_sample, state_conv, state_gla, norm_mix, norm_ffn, norm_final,
           conv_w_in, conv_w, conv_w_out, gla_w_in, gla_w_gate, gla_b_gate, gla_norm, gla_w_out,
           ffn_w_gate, ffn_w_up, ffn_w_down, moe_w_router, moe_w_gate, moe_w_up, moe_w_down):
    x0, h = stack_norm(x_prompt.reshape(M_PROMPT, D), x_sample.reshape(M_SAMPLE, D), norm_mix[0])
    bz, conv_prompt_state, s0, s1 = conv_mixer(
        h, conv_w_in, conv_w[0], state_conv[0].reshape(M_SAMPLE, 2 * D))
    conv_sample_state = jnp.stack([s0, s1], axis=1).reshape(1, M_SAMPLE, 2, D)
    x1, h = linear_res_norm(bz, conv_w_out, 0, x0, norm_ffn[0], name="conv_out")

    a = swiglu_up(h, ffn_w_gate, ffn_w_up, 0)
    x2 = linear(a, ffn_w_down, 0, D, tn=512, tm=640, res=x1, name="ffn_down")

    h = rmsnorm_bf16(x2, norm_mix[1])
    p = linear(h, jnp.swapaxes(gla_w_in, 1, 2), 0, 2 * HK + 2 * HV, tn=1024, tm=TM_DENSE,
               name="gla_in", w_is_nk=True)
    w_gate_pad = jnp.pad(gla_w_gate[0], ((0, LANES - GATE_RANK), (0, 0)))
    w_a_pad = jnp.pad(gla_w_in[0, :, 2 * HK + 2 * HV:], ((0, 0), (0, LANES - GATE_RANK)))
    g = gla_gate(h, w_a_pad, w_gate_pad, gla_b_gate[0].reshape(1, HK))
    gn = gla_norm[0].reshape(1, DV)
    gated, gla_prompt_state = gla_prompt(p, g, gn)
    gated, gla_sample_state = gla_sample(gated, p, g, gn, state_gla[0])
    x3 = linear(gated, gla_w_out, 0, D, tn=1024, tm=TM_DENSE, res=x2, name="gla_out")

    w_router_pad = jnp.pad(moe_w_router[0], ((0, 0), (0, LANES - N_EXPERTS)))
    route, idx = router(x3, norm_ffn[1], w_router_pad)
    row_token, n_valid, pos1, pos2, tstart, ntiles = _group_tables(idx)
    xs = gather_rows(row_token, n_valid, x3, norm_ffn[1])
    act = grouped_matmul(xs, (moe_w_gate, moe_w_up), tstart, ntiles, tm=GROUP_ROWS, tn=1792, ck=256,
                         out_dtype=BF16, vmem_mb=56, name="moe_up")
    y = grouped_matmul(act, (moe_w_down,), tstart, ntiles, tm=GROUP_ROWS, tn=1024, ck=896,
                       out_dtype=F32, vmem_mb=56, name="moe_down")
    y_prompt, y_sample = moe_combine(pos1, pos2, x3, route, norm_final, y)

    y_prompt = y_prompt.reshape(N_PROMPT_SEQ, SEQ, D)
    y_sample = y_sample.reshape(M_SAMPLE, 1, D)
    return (y_prompt, y_sample,
            conv_prompt_state.reshape(1, N_PROMPT_SEQ, 2, D), conv_sample_state,
            gla_prompt_state.reshape(1, N_PROMPT_SEQ, HEADS, DK, DV),
            gla_sample_state.reshape(1, M_SAMPLE, HEADS, DK, DV))
```

```python
import functools

import jax
import jax.numpy as jnp
from jax import lax
from jax.experimental import pallas as pl
from jax.experimental.pallas import tpu as pltpu

F32 = jnp.float32
BF16 = jnp.bfloat16

D = 2048
N_PROMPT_SEQ = 4
SEQ = 2048
M_PROMPT = N_PROMPT_SEQ * SEQ
M_SAMPLE = 128
M_TOK = M_PROMPT + M_SAMPLE
HEADS = 4
DK = 256
DV = 512
HK = HEADS * DK
HV = HEADS * DV
GATE_RANK = 16
GATE_TAU = 16.0
CHUNK = 64
N_CHUNKS = SEQ // CHUNK
D_FF = 5632
N_EXPERTS = 8
D_FF_EXPERT = 7168
EPS = 1e-6
LANES = 128

TM_DENSE = 1040
GROUP_ROWS = 256
N_ASSIGN = 2 * M_TOK
N_GROUP_TILES = N_ASSIGN // GROUP_ROWS + N_EXPERTS
R_PAD = N_GROUP_TILES * GROUP_ROWS


def _params(sem, vmem_mb):
    return pltpu.CompilerParams(dimension_semantics=sem,
                                vmem_limit_bytes=vmem_mb * 1024 * 1024)


def _rmsnorm(x, g):
    return x * lax.rsqrt(jnp.mean(x * x, axis=-1, keepdims=True) + EPS) * g


def _silu(x):
    return x * jax.nn.sigmoid(x)


def _norm_kernel(x_ref, g_ref, o_ref):
    o_ref[...] = _rmsnorm(x_ref[...], g_ref[...]).astype(o_ref.dtype)


def rmsnorm_bf16(x, g, tr=832):
    m = x.shape[0]
    return pl.pallas_call(
        _norm_kernel,
        grid=(m // tr,),
        in_specs=[pl.BlockSpec((tr, D), lambda i: (i, 0)),
                  pl.BlockSpec((1, D), lambda i: (0, 0))],
        out_specs=pl.BlockSpec((tr, D), lambda i: (i, 0)),
        out_shape=jax.ShapeDtypeStruct((m, D), BF16),
        compiler_params=_params(("arbitrary",), 40),
        name="rmsnorm",
    )(x, g.reshape(1, D))


STACK_ROWS = 512
STACK_PROMPT_STEPS = M_PROMPT // STACK_ROWS


def _stack_norm_kernel(xp_ref, xs_ref, g_ref, x_ref, h_ref):
    i = pl.program_id(0)
    g = g_ref[...]

    @pl.when(i < STACK_PROMPT_STEPS)
    def _():
        x = xp_ref[...]
        x_ref[...] = x
        h_ref[...] = _rmsnorm(x, g).astype(h_ref.dtype)

    @pl.when(i == STACK_PROMPT_STEPS)
    def _():
        x = xs_ref[...]
        x_ref[0:M_SAMPLE, :] = x
        h_ref[0:M_SAMPLE, :] = _rmsnorm(x, g).astype(h_ref.dtype)


def stack_norm(xp, xs, g):
    last_prompt = STACK_PROMPT_STEPS - 1
    return pl.pallas_call(
        _stack_norm_kernel,
        grid=(STACK_PROMPT_STEPS + 1,),
        in_specs=[pl.BlockSpec((STACK_ROWS, D), lambda i: (jnp.minimum(i, last_prompt), 0)),
                  pl.BlockSpec((M_SAMPLE, D), lambda i: (0, 0)),
                  pl.BlockSpec((1, D), lambda i: (0, 0))],
        out_specs=[pl.BlockSpec((STACK_ROWS, D), lambda i: (i, 0)),
                   pl.BlockSpec((STACK_ROWS, D), lambda i: (i, 0))],
        out_shape=[jax.ShapeDtypeStruct((M_TOK, D), F32),
                   jax.ShapeDtypeStruct((M_TOK, D), BF16)],
        compiler_params=_params(("arbitrary",), 40),
        name="stack_norm",
    )(xp, xs, g.reshape(1, D))


def _linear_kernel(*refs, has_res, w_is_nk):
    if has_res:
        x_ref, w_ref, r_ref, o_ref, wb_ref = refs
    else:
        x_ref, w_ref, o_ref, wb_ref = refs

    @pl.when(pl.program_id(1) == 0)
    def _():
        wb_ref[...] = w_ref[...].astype(BF16)

    contract_w = 1 if w_is_nk else 0
    acc = lax.dot_general(x_ref[...], wb_ref[...], (((1,), (contract_w,)), ((), ())),
                          preferred_element_type=F32)
    if has_res:
        acc = acc + r_ref[...]
    o_ref[...] = acc.astype(o_ref.dtype)


def linear(x, w, layer, n_out, tn, tm, res=None, out_dtype=F32, vmem_mb=56, name="linear",
           w_is_nk=False):
    m, k = x.shape
    wblock = (None, tn, k) if w_is_nk else (None, k, tn)
    wmap = (lambda j, i: (layer, j, 0)) if w_is_nk else (lambda j, i: (layer, 0, j))
    in_specs = [pl.BlockSpec((tm, k), lambda j, i: (i, 0)),
                pl.BlockSpec(wblock, wmap)]
    args = [x, w]
    if res is not None:
        in_specs.append(pl.BlockSpec((tm, tn), lambda j, i: (i, j)))
        args.append(res)
    return pl.pallas_call(
        functools.partial(_linear_kernel, has_res=res is not None, w_is_nk=w_is_nk),
        grid=(n_out // tn, m // tm),
        in_specs=in_specs,
        out_specs=pl.BlockSpec((tm, tn), lambda j, i: (i, j)),
        out_shape=jax.ShapeDtypeStruct((m, n_out), out_dtype),
        scratch_shapes=[pltpu.VMEM(wblock[1:], BF16)],
        compiler_params=_params(("arbitrary", "arbitrary"), vmem_mb),
        name=name,
    )(*args)


def _linear_res_norm_kernel(x_ref, w_ref, r_ref, g_ref, o_ref, h_ref, wb_ref):
    @pl.when(pl.program_id(0) == 0)
    def _():
        wb_ref[...] = w_ref[...].astype(BF16)

    y = jnp.dot(x_ref[...], wb_ref[...], preferred_element_type=F32) + r_ref[...]
    o_ref[...] = y
    h_ref[...] = _rmsnorm(y, g_ref[...]).astype(h_ref.dtype)


def linear_res_norm(x, w, layer, res, g, tm=416, name="linear_res_norm"):
    m, k = x.shape
    return pl.pallas_call(
        _linear_res_norm_kernel,
        grid=(m // tm,),
        in_specs=[pl.BlockSpec((tm, k), lambda i: (i, 0)),
                  pl.BlockSpec((None, k, D), lambda i: (layer, 0, 0), pipeline_mode=pl.Buffered(1)),
                  pl.BlockSpec((tm, D), lambda i: (i, 0)),
                  pl.BlockSpec((1, D), lambda i: (0, 0))],
        out_specs=[pl.BlockSpec((tm, D), lambda i: (i, 0)),
                   pl.BlockSpec((tm, D), lambda i: (i, 0))],
        out_shape=[jax.ShapeDtypeStruct((m, D), F32),
                   jax.ShapeDtypeStruct((m, D), BF16)],
        scratch_shapes=[pltpu.VMEM((k, D), BF16)],
        compiler_params=_params(("arbitrary",), 56),
        name=name,
    )(x, w, res, g.reshape(1, D))


def _swiglu_kernel(x_ref, wg_ref, wu_ref, o_ref, wgb_ref, wub_ref):
    @pl.when(pl.program_id(1) == 0)
    def _():
        wgb_ref[...] = wg_ref[...].astype(BF16)
        wub_ref[...] = wu_ref[...].astype(BF16)

    x = x_ref[...]
    g = jnp.dot(x, wgb_ref[...], preferred_element_type=F32)
    u = jnp.dot(x, wub_ref[...], preferred_element_type=F32)
    o_ref[...] = (_silu(g) * u).astype(o_ref.dtype)


def swiglu_up(x, wg, wu, layer, tn=512, tm=TM_DENSE):
    m, k = x.shape
    f = wg.shape[-1]
    wspec = pl.BlockSpec((None, k, tn), lambda j, i: (layer, 0, j))
    return pl.pallas_call(
        _swiglu_kernel,
        grid=(f // tn, m // tm),
        in_specs=[pl.BlockSpec((tm, k), lambda j, i: (i, 0)), wspec, wspec],
        out_specs=pl.BlockSpec((tm, tn), lambda j, i: (i, j)),
        out_shape=jax.ShapeDtypeStruct((m, f), BF16),
        scratch_shapes=[pltpu.VMEM((k, tn), BF16), pltpu.VMEM((k, tn), BF16)],
        compiler_params=_params(("arbitrary", "arbitrary"), 56),
        name="ffn_up",
    )(x, wg, wu)


CONV_TM = 512
CONV_TC = 512
CONV_TILES_PER_SEQ = SEQ // CONV_TM
CONV_PROMPT_TILES = M_PROMPT // CONV_TM


def _conv_kernel(x_ref, xs_ref, wh_ref, wb_ref, wc_ref, cw_ref, b0_ref, b1_ref,
                 bz_ref, stp_ref, s0_ref, s1_ref, wbf_ref, carry_ref):
    i = pl.program_id(1)

    @pl.when(i == 0)
    def _():
        wbf_ref[0] = wh_ref[...].astype(BF16)
        wbf_ref[1] = wb_ref[...].astype(BF16)
        wbf_ref[2] = wc_ref[...].astype(BF16)

    def project(x):
        hh = jnp.dot(x, wbf_ref[0], preferred_element_type=F32)
        bb = jnp.dot(x, wbf_ref[1], preferred_element_type=F32)
        cc = jnp.dot(x, wbf_ref[2], preferred_element_type=F32)
        return bb, cc * hh

    w = cw_ref[...]

    @pl.when(i < CONV_PROMPT_TILES)
    def _():
        @pl.when(i % CONV_TILES_PER_SEQ == 0)
        def _():
            carry_ref[...] = jnp.zeros_like(carry_ref)

        bb, u = project(x_ref[...])
        c2 = carry_ref[0:1, :]
        c1 = carry_ref[1:2, :]
        row = lax.broadcasted_iota(jnp.int32, u.shape, 0)
        u1 = jnp.where(row == 0, c1, pltpu.roll(u, 1, 0))
        u2 = jnp.where(row == 0, c2, jnp.where(row == 1, c1, pltpu.roll(u, 2, 0)))
        z = w[0:1, :] * u2
        z = z + w[1:2, :] * u1
        z = z + w[2:3, :] * u
        bz_ref[...] = (bb * z).astype(bz_ref.dtype)
        tail = u[CONV_TM - 2:CONV_TM, :]
        carry_ref[0:2, :] = tail
        stp_ref[...] = tail

    @pl.when(i == CONV_PROMPT_TILES)
    def _():
        bb, u = project(xs_ref[...])
        b1 = b1_ref[...]
        z = w[0:1, :] * b0_ref[...]
        z = z + w[1:2, :] * b1
        z = z + w[2:3, :] * u
        bz_ref[0:M_SAMPLE, :] = (bb * z).astype(bz_ref.dtype)
        s0_ref[...] = b1
        s1_ref[...] = u


def conv_mixer(h, w_in, conv_w, buf):
    tm, tc = CONV_TM, CONV_TC
    nj = D // tc
    last_tile = CONV_PROMPT_TILES - 1
    last_seq = N_PROMPT_SEQ - 1
    wspec = lambda part: pl.BlockSpec((None, D, tc), lambda j, i: (0, 0, part * nj + j))
    return pl.pallas_call(
        _conv_kernel,
        grid=(nj, CONV_PROMPT_TILES + 1),
        in_specs=[pl.BlockSpec((tm, D), lambda j, i: (jnp.minimum(i, last_tile), 0)),
                  pl.BlockSpec((M_SAMPLE, D), lambda j, i: (M_PROMPT // M_SAMPLE, 0)),
                  wspec(0), wspec(1), wspec(2),
                  pl.BlockSpec((3, tc), lambda j, i: (0, j)),
                  pl.BlockSpec((M_SAMPLE, tc), lambda j, i: (0, j)),
                  pl.BlockSpec((M_SAMPLE, tc), lambda j, i: (0, nj + j))],
        out_specs=[pl.BlockSpec((tm, tc), lambda j, i: (i, j)),
                   pl.BlockSpec((None, 2, tc),
                                lambda j, i: (jnp.minimum(i // CONV_TILES_PER_SEQ, last_seq), 0, j)),
                   pl.BlockSpec((M_SAMPLE, tc), lambda j, i: (0, j)),
                   pl.BlockSpec((M_SAMPLE, tc), lambda j, i: (0, j))],
        out_shape=[jax.ShapeDtypeStruct((M_TOK, D), BF16),
                   jax.ShapeDtypeStruct((N_PROMPT_SEQ, 2, D), F32),
                   jax.ShapeDtypeStruct((M_SAMPLE, D), F32),
                   jax.ShapeDtypeStruct((M_SAMPLE, D), F32)],
        scratch_shapes=[pltpu.VMEM((3, D, tc), BF16), pltpu.VMEM((8, tc), F32)],
        compiler_params=_params(("arbitrary", "arbitrary"), 56),
        name="conv_mixer",
    )(h, h, w_in, w_in, w_in, conv_w, buf, buf)


def _log_sigmoid(x):
    return jnp.minimum(x, 0.0) - jnp.log1p(jnp.exp(-jnp.abs(x)))


def _gla_gate_kernel(h_ref, wa_ref, wg_ref, bg_ref, g_ref):
    a = jnp.dot(h_ref[...], wa_ref[...].astype(BF16), preferred_element_type=F32)
    z = jnp.dot(a.astype(BF16), wg_ref[...].astype(BF16), preferred_element_type=F32) + bg_ref[...]
    g_ref[...] = _log_sigmoid(z) * (1.0 / GATE_TAU)


def gla_gate(h, w_a_pad, w_gate_pad, b_gate, tm=TM_DENSE):
    return pl.pallas_call(
        _gla_gate_kernel,
        grid=(M_TOK // tm,),
        in_specs=[pl.BlockSpec((tm, D), lambda i: (i, 0)),
                  pl.BlockSpec((D, LANES), lambda i: (0, 0)),
                  pl.BlockSpec((LANES, HK), lambda i: (0, 0)),
                  pl.BlockSpec((1, HK), lambda i: (0, 0))],
        out_specs=pl.BlockSpec((tm, HK), lambda i: (i, 0)),
        out_shape=jax.ShapeDtypeStruct((M_TOK, HK), F32),
        compiler_params=_params(("arbitrary",), 40),
        name="gla_gate",
    )(h, w_a_pad, w_gate_pad, b_gate)


def _row_to_cols(row):
    return jnp.transpose(jnp.broadcast_to(row, (LANES, row.shape[1])))


def _split3_bf16(x):
    x1 = x.astype(BF16)
    r1 = x - x1.astype(F32)
    x2 = r1.astype(BF16)
    x3 = (r1 - x2.astype(F32)).astype(BF16)
    return x1, x2, x3


def _gla_prompt_kernel(o_init, q_ref, k_ref, v_ref, r_ref, g_ref, gn_ref, o_ref, sout_ref, s_ref):
    del o_init
    c = pl.program_id(1)

    @pl.when(c == 0)
    def _():
        s_ref[...] = jnp.zeros_like(s_ref)

    row = lax.broadcasted_iota(jnp.int32, (CHUNK, CHUNK), 0)
    col = lax.broadcasted_iota(jnp.int32, (CHUNK, CHUNK), 1)
    tri = row >= col
    trib = tri.astype(BF16)

    g1, g2, g3 = _split3_bf16(g_ref[...])
    b = (jnp.dot(trib, g1, preferred_element_type=F32)
         + jnp.dot(trib, g2, preferred_element_type=F32)
         + jnp.dot(trib, g3, preferred_element_type=F32))
    b_last = b[CHUNK - 1:CHUNK, :]
    q = q_ref[...] * (DK ** -0.5)
    k = k_ref[...]
    q_dec = (q * jnp.exp(b)).astype(BF16)
    k_inv = (k * jnp.exp(-b)).astype(BF16)
    k_end = (k * jnp.exp(b_last - b)).astype(BF16)
    decay = jnp.exp(b_last)
    gn = gn_ref[...]

    for h in range(HEADS):
        ks = slice(h * DK, (h + 1) * DK)
        vs = slice(h * DV, (h + 1) * DV)
        vb = v_ref[:, vs].astype(BF16)
        s_old = s_ref[h]
        scores = lax.dot_general(q_dec[:, ks], k_inv[:, ks], (((1,), (1,)), ((), ())),
                                 preferred_element_type=F32)
        scores = jnp.where(tri, scores, 0.0).astype(BF16)
        o = (jnp.dot(scores, vb, preferred_element_type=F32)
             + jnp.dot(q_dec[:, ks], s_old.astype(BF16), preferred_element_type=F32))
        kv = lax.dot_general(k_end[:, ks], vb, (((0,), (0,)), ((), ())),
                             preferred_element_type=F32)
        dcol = _row_to_cols(decay[:, ks])
        s_ref[h] = jnp.concatenate(
            [s_old[:, j * LANES:(j + 1) * LANES] * dcol for j in range(DV // LANES)], axis=1) + kv
        on = _rmsnorm(o, gn)
        o_ref[:, vs] = (_silu(r_ref[:, vs]) * on).astype(o_ref.dtype)

    @pl.when(c == N_CHUNKS - 1)
    def _():
        sout_ref[...] = s_ref[...]


def gla_prompt(p, g, g_norm):
    rows = lambda n, c: n * N_CHUNKS + c
    return pl.pallas_call(
        _gla_prompt_kernel,
        grid=(N_PROMPT_SEQ, N_CHUNKS),
        in_specs=[pl.BlockSpec(memory_space=pl.ANY),
                  pl.BlockSpec((CHUNK, HK), lambda n, c: (rows(n, c), 0)),
                  pl.BlockSpec((CHUNK, HK), lambda n, c: (rows(n, c), 1)),
                  pl.BlockSpec((CHUNK, HV), lambda n, c: (rows(n, c), 1)),
                  pl.BlockSpec((CHUNK, HV), lambda n, c: (rows(n, c), 2)),
                  pl.BlockSpec((CHUNK, HK), lambda n, c: (rows(n, c), 0)),
                  pl.BlockSpec((1, DV), lambda n, c: (0, 0))],
        out_specs=[pl.BlockSpec((CHUNK, HV), lambda n, c: (rows(n, c), 0)),
                   pl.BlockSpec((None, HEADS, DK, DV), lambda n, c: (n, 0, 0, 0))],
        out_shape=[jax.ShapeDtypeStruct((M_TOK, HV), BF16),
                   jax.ShapeDtypeStruct((N_PROMPT_SEQ, HEADS, DK, DV), F32)],
        scratch_shapes=[pltpu.VMEM((HEADS, DK, DV), F32)],
        input_output_aliases={0: 0},
        compiler_params=_params(("arbitrary", "arbitrary"), 40),
        name="gla_prompt",
    )(jnp.zeros((M_TOK, HV), BF16), p, p, p, p, g, g_norm)


GLA_SAMPLE_SEQS = 2


def _gla_sample_kernel(gated_any, q_ref, k_ref, v_ref, r_ref, g_ref, gn_ref, s_ref,
                       gated_ref, sout_ref, qt_ref, kt_ref, et_ref, o_scr):
    del gated_any
    i = pl.program_id(0)

    @pl.when(i == 0)
    def _():
        q = q_ref[...] * (DK ** -0.5)
        k = k_ref[...]
        e = jnp.exp(g_ref[...])
        for h in range(HEADS):
            ks = slice(h * DK, (h + 1) * DK)
            qt_ref[h] = jnp.transpose(q[:, ks])
            kt_ref[h] = jnp.transpose(k[:, ks])
            et_ref[h] = jnp.transpose(e[:, ks])

    lane = lax.broadcasted_iota(jnp.int32, (DK, M_SAMPLE), 1)
    for s in range(GLA_SAMPLE_SEQS):
        n = i * GLA_SAMPLE_SEQS + s
        pick = lane == n

        def column(t):
            return jnp.sum(jnp.where(pick, t, 0.0), axis=1, keepdims=True)

        for h in range(HEADS):
            vs = slice(h * DV, (h + 1) * DV)
            v_row = v_ref[pl.ds(n, 1), vs]
            s_new = s_ref[s, h] * column(et_ref[h]) + column(kt_ref[h]) * v_row
            sout_ref[s, h] = s_new
            o_scr[pl.ds(n, 1), vs] = jnp.sum(column(qt_ref[h]) * s_new, axis=0, keepdims=True)

    @pl.when(i == pl.num_programs(0) - 1)
    def _():
        gn = gn_ref[...]
        for h in range(HEADS):
            vs = slice(h * DV, (h + 1) * DV)
            on = _rmsnorm(o_scr[:, vs], gn)
            gated_ref[:, vs] = (_silu(r_ref[:, vs]) * on).astype(gated_ref.dtype)


def gla_sample(gated, p, g, g_norm, state):
    rb = M_PROMPT // M_SAMPLE
    bs = GLA_SAMPLE_SEQS
    return pl.pallas_call(
        _gla_sample_kernel,
        grid=(M_SAMPLE // bs,),
        in_specs=[pl.BlockSpec(memory_space=pl.ANY),
                  pl.BlockSpec((M_SAMPLE, HK), lambda i: (rb, 0)),
                  pl.BlockSpec((M_SAMPLE, HK), lambda i: (rb, 1)),
                  pl.BlockSpec((M_SAMPLE, HV), lambda i: (rb, 1)),
                  pl.BlockSpec((M_SAMPLE, HV), lambda i: (rb, 2)),
                  pl.BlockSpec((M_SAMPLE, HK), lambda i: (rb, 0)),
                  pl.BlockSpec((1, DV), lambda i: (0, 0)),
                  pl.BlockSpec((bs, HEADS, DK, DV), lambda i: (i, 0, 0, 0))],
        out_specs=[pl.BlockSpec((M_SAMPLE, HV), lambda i: (rb, 0)),
                   pl.BlockSpec((bs, HEADS, DK, DV), lambda i: (i, 0, 0, 0))],
        out_shape=[jax.ShapeDtypeStruct((M_TOK, HV), BF16),
                   jax.ShapeDtypeStruct((M_SAMPLE, HEADS, DK, DV), F32)],
        scratch_shapes=[pltpu.VMEM((HEADS, DK, M_SAMPLE), F32),
                        pltpu.VMEM((HEADS, DK, M_SAMPLE), F32),
                        pltpu.VMEM((HEADS, DK, M_SAMPLE), F32),
                        pltpu.VMEM((M_SAMPLE, HV), F32)],
        input_output_aliases={0: 0},
        compiler_params=_params(("arbitrary",), 40),
        name="gla_sample",
    )(gated, p, p, p, p, g, g_norm, state)


def _router_kernel(x_ref, g_ref, wr_ref, route_ref, idx_ref):
    h = _rmsnorm(x_ref[...], g_ref[...])
    logits = jnp.dot(h, wr_ref[...], preferred_element_type=F32, precision=lax.Precision.HIGHEST)
    lane = lax.broadcasted_iota(jnp.int32, logits.shape, 1)
    lane_f = lane.astype(F32)
    neg = jnp.float32(-jnp.inf)
    logits = jnp.where(lane < N_EXPERTS, logits, neg)
    m1 = jnp.max(logits, axis=1, keepdims=True)
    i1 = jnp.min(jnp.where(logits == m1, lane_f, float(LANES)), axis=1, keepdims=True)
    rest = jnp.where(lane_f == i1, neg, logits)
    m2 = jnp.max(rest, axis=1, keepdims=True)
    i2 = jnp.min(jnp.where(rest == m2, lane_f, float(LANES)), axis=1, keepdims=True)
    e2 = jnp.exp(m2 - m1)
    den = 1.0 + e2
    w1 = 1.0 / den
    w2 = e2 / den
    route_ref[...] = jnp.where(lane == 0, w1, jnp.where(lane == 1, w2, 0.0))
    idx_ref[...] = jnp.where(lane == 0, i1, jnp.where(lane == 1, i2, 0.0)).astype(jnp.int32)


def router(x, g, w_router_pad, tm=416):
    return pl.pallas_call(
        _router_kernel,
        grid=(M_TOK // tm,),
        in_specs=[pl.BlockSpec((tm, D), lambda i: (i, 0)),
                  pl.BlockSpec((1, D), lambda i: (0, 0)),
                  pl.BlockSpec((D, LANES), lambda i: (0, 0))],
        out_specs=[pl.BlockSpec((tm, LANES), lambda i: (i, 0)),
                   pl.BlockSpec((tm, LANES), lambda i: (i, 0))],
        out_shape=[jax.ShapeDtypeStruct((M_TOK, LANES), F32),
                   jax.ShapeDtypeStruct((M_TOK, LANES), jnp.int32)],
        compiler_params=_params(("arbitrary",), 40),
        name="router",
    )(x, g.reshape(1, D), w_router_pad)


def _grouped_kernel(*refs, n_w, n_col, total_tiles, gather):
    if gather:
        tstart, ntiles, tok, src_hbm, gn_ref = refs[:5]
        w_hbm = refs[5:5 + n_w]
        o_hbm, x_hbm = refs[5 + n_w:7 + n_w]
        wbf, stage, xbuf, obuf, xsem, osem, wsem, gbuf, gsem, xwsem = refs[7 + n_w:]
    else:
        tstart, ntiles, x_hbm = refs[:3]
        w_hbm = refs[3:3 + n_w]
        o_hbm = refs[3 + n_w]
        wbf, stage, xbuf, obuf, xsem, osem, wsem = refs[4 + n_w:]
    s = pl.program_id(0)
    n_items = pl.num_programs(0)
    e = s // n_col
    j = lax.rem(s, n_col)
    p = lax.rem(s, 2)
    tm = xbuf.shape[1]
    tn = obuf.shape[2]
    ck = stage.shape[2]
    n_chunks = wbf.shape[2] // ck
    nt = ntiles[e]
    t0 = tstart[e]
    col = pl.multiple_of(j * tn, tn)
    has_next = s + 1 < n_items

    def w_copies(item, c, q):
        row = pl.multiple_of(c * ck, ck)
        wcol = pl.multiple_of(lax.rem(item, n_col) * tn, tn)
        return [pltpu.make_async_copy(w.at[0, item // n_col, pl.ds(row, ck), pl.ds(wcol, tn)],
                                      stage.at[q, i], wsem.at[q, i])
                for i, w in enumerate(w_hbm)]

    def w_start(item, c, q):
        for cp in w_copies(item, c, q):
            cp.start(priority=1)

    def w_chunk(item, c, dst):
        q = lax.rem(c, 2)

        @pl.when(c + 1 < n_chunks)
        def _():
            w_start(item, c + 1, 1 - q)

        row = pl.multiple_of(c * ck, ck)
        for i, cp in enumerate(w_copies(item, c, q)):
            cp.wait()
            wbf[dst, i, pl.ds(row, ck), :] = stage[q, i].astype(BF16)

    @pl.when(s == 0)
    def _():
        w_start(0, 0, 0)

        def first(c, carry):
            w_chunk(0, c, 0)
            return carry

        lax.fori_loop(0, n_chunks, first, 0)

    @pl.when(has_next)
    def _():
        w_start(s + 1, 0, 0)

    def x_copy(tile, slot):
        row = pl.multiple_of(tile * tm, tm)
        return pltpu.make_async_copy(x_hbm.at[pl.ds(row, tm)], xbuf.at[slot], xsem.at[slot])

    def o_copy(tile, slot):
        row = pl.multiple_of(tile * tm, tm)
        return pltpu.make_async_copy(obuf.at[slot], o_hbm.at[pl.ds(row, tm), pl.ds(col, tn)],
                                     osem.at[slot])

    nx = xbuf.shape[0]

    def when_fetching(fn):
        if gather:
            pl.when(j > 0)(fn)
        else:
            fn()

    def g_issue(t, gslot):
        base = (t0 + t) * tm

        def rows(r2, carry):
            for k in range(2):
                r = 2 * r2 + k
                pltpu.make_async_copy(src_hbm.at[pl.ds(tok[base + r], 1)],
                                      gbuf.at[gslot, pl.ds(r, 1)], gsem.at[gslot]).start(priority=k)
            return carry

        lax.fori_loop(0, tm // 2, rows, 0, unroll=4)

    def xw_copy(tile, xslot):
        row = pl.multiple_of(tile * tm, tm)
        return pltpu.make_async_copy(xbuf.at[xslot], x_hbm.at[pl.ds(row, tm)], xwsem.at[xslot])

    @when_fetching
    def _():
        for d in range(nx - 1):
            @pl.when(d < nt)
            def _():
                x_copy(t0 + d, d).start()

    if gather:
        @pl.when((j == 0) & (nt > 0))
        def _():
            g_issue(0, 0)

    def body(t, chunks_done):
        slot = lax.rem(t, 2)
        xslot = lax.rem(t, nx)

        @when_fetching
        def _():
            x_copy(t0 + t, xslot).wait()
            ahead = t + (nx - 1)

            @pl.when(ahead < nt)
            def _():
                x_copy(t0 + ahead, lax.rem(ahead, nx)).start()

        if gather:
            @pl.when(j == 0)
            def _():
                @pl.when(t + 1 < nt)
                def _():
                    g_issue(t + 1, 1 - slot)

                pltpu.make_async_copy(src_hbm.at[pl.ds(0, tm)], gbuf.at[slot], gsem.at[slot]).wait()

                @pl.when(t >= nx)
                def _():
                    xw_copy(t0 + t - nx, xslot).wait()

                xbuf[xslot] = _rmsnorm(gbuf[slot], gn_ref[...]).astype(xbuf.dtype)
                xw_copy(t0 + t, xslot).start()

        @pl.when(t >= 2)
        def _():
            o_copy(t0 + t - 2, slot).wait()

        x = xbuf[xslot]
        if n_w == 2:
            g = jnp.dot(x, wbf[p, 0], preferred_element_type=F32)
            u = jnp.dot(x, wbf[p, 1], preferred_element_type=F32)
            obuf[slot] = (_silu(g) * u).astype(obuf.dtype)
        else:
            obuf[slot] = jnp.dot(x, wbf[p, 0], preferred_element_type=F32).astype(obuf.dtype)
        o_copy(t0 + t, slot).start()

        stream = has_next & (chunks_done < n_chunks)

        @pl.when(stream)
        def _():
            w_chunk(s + 1, chunks_done, 1 - p)

        return chunks_done + stream.astype(jnp.int32)

    chunks_done = lax.fori_loop(0, nt, body, jnp.int32(0))

    @pl.when(has_next)
    def _():
        def rest(c, carry):
            w_chunk(s + 1, c, 1 - p)
            return carry

        lax.fori_loop(chunks_done, n_chunks, rest, 0)

    @pl.when(nt >= 2)
    def _():
        o_copy(t0 + nt - 2, lax.rem(nt, 2)).wait()

    @pl.when(nt >= 1)
    def _():
        o_copy(t0 + nt - 1, lax.rem(nt + 1, 2)).wait()

    if gather:
        @pl.when(j == 0)
        def _():
            for d in range(nx):
                @pl.when(nt > d)
                def _():
                    tile = nt - 1 - d
                    xw_copy(t0 + tile, lax.rem(tile, nx)).wait()

    @pl.when(e == N_EXPERTS - 1)
    def _():
        obuf[0] = jnp.zeros(obuf.shape[1:], obuf.dtype)

        def zero_tile(tile, carry):
            cp = o_copy(tile, 0)
            cp.start()
            cp.wait()
            return carry

        lax.fori_loop(t0 + nt, total_tiles, zero_tile, 0)

        if gather:
            @pl.when(j == 0)
            def _():
                xbuf[0] = jnp.zeros(xbuf.shape[1:], xbuf.dtype)

                def zero_x(tile, carry):
                    cp = xw_copy(tile, 0)
                    cp.start()
                    cp.wait()
                    return carry

                lax.fori_loop(t0 + nt, total_tiles, zero_x, 0)


def grouped_matmul(x, ws, tstart, ntiles, tm, tn, ck, out_dtype, vmem_mb, name, gather_from=None):
    n_x = 3
    n = ws[0].shape[-1]
    n_w = len(ws)
    n_col = n // tn
    any_spec = pl.BlockSpec(memory_space=pl.ANY)
    gather = gather_from is not None
    if gather:
        row_token, src, g = gather_from
        k = src.shape[1]
        prefetch = (tstart, ntiles, row_token)
        inputs = (src, g.reshape(1, k)) + tuple(ws)
        in_specs = [any_spec, pl.BlockSpec((1, k), lambda s, *_: (0, 0))] + [any_spec] * n_w
        out_specs = [any_spec, any_spec]
        out_shape = [jax.ShapeDtypeStruct((R_PAD, n), out_dtype),
                     jax.ShapeDtypeStruct((R_PAD, k), BF16)]
        extra_scratch = [pltpu.VMEM((2, tm, k), F32), pltpu.SemaphoreType.DMA((2,)),
                         pltpu.SemaphoreType.DMA((n_x,))]
    else:
        k = x.shape[1]
        prefetch = (tstart, ntiles)
        inputs = (x,) + tuple(ws)
        in_specs = [any_spec] * (1 + n_w)
        out_specs = any_spec
        out_shape = jax.ShapeDtypeStruct((R_PAD, n), out_dtype)
        extra_scratch = []
    return pl.pallas_call(
        functools.partial(_grouped_kernel, n_w=n_w, n_col=n_col, total_tiles=R_PAD // tm,
                          gather=gather),
        grid_spec=pltpu.PrefetchScalarGridSpec(
            num_scalar_prefetch=len(prefetch),
            grid=(N_EXPERTS * n_col,),
            in_specs=in_specs,
            out_specs=out_specs,
            scratch_shapes=[pltpu.VMEM((2, n_w, k, tn), BF16), pltpu.VMEM((2, n_w, ck, tn), F32),
                            pltpu.VMEM((n_x, tm, k), BF16), pltpu.VMEM((2, tm, tn), out_dtype),
                            pltpu.SemaphoreType.DMA((n_x,)), pltpu.SemaphoreType.DMA((2,)),
                            pltpu.SemaphoreType.DMA((2, n_w))] + extra_scratch),
        out_shape=out_shape,
        compiler_params=_params(("arbitrary",), vmem_mb),
        name=name,
    )(*prefetch, *inputs)


COMBINE_ROWS = 128
COMBINE_PROMPT_STEPS = M_PROMPT // COMBINE_ROWS


def _combine_kernel(p1_ref, p2_ref, x_ref, route_ref, g_ref, y_hbm, op_ref, os_ref, b1, b2, sem):
    i = pl.program_id(0)
    tm = b1.shape[1]
    slot = lax.rem(i, 2)

    def issue(tile, dst_slot):
        base = tile * tm

        def body(r, carry):
            pltpu.make_async_copy(y_hbm.at[pl.ds(p1_ref[base + r], 1)], b1.at[dst_slot, pl.ds(r, 1)],
                                  sem.at[0, dst_slot]).start(priority=0)
            pltpu.make_async_copy(y_hbm.at[pl.ds(p2_ref[base + r], 1)], b2.at[dst_slot, pl.ds(r, 1)],
                                  sem.at[1, dst_slot]).start(priority=1)
            return carry

        lax.fori_loop(0, tm, body, 0, unroll=8)

    @pl.when(i == 0)
    def _():
        issue(0, 0)

    @pl.when(i + 1 < pl.num_programs(0))
    def _():
        issue(i + 1, 1 - slot)

    pltpu.make_async_copy(y_hbm.at[pl.ds(0, tm)], b1.at[slot], sem.at[0, slot]).wait()
    pltpu.make_async_copy(y_hbm.at[pl.ds(0, tm)], b2.at[slot], sem.at[1, slot]).wait()
    route = route_ref[...]
    w1 = route[:, 0:1]
    w2 = route[:, 1:2]
    x = x_ref[...] + (w1 * b1[slot] + w2 * b2[slot])
    out = _rmsnorm(x, g_ref[...])

    @pl.when(i < COMBINE_PROMPT_STEPS)
    def _():
        op_ref[...] = out

    @pl.when(i == COMBINE_PROMPT_STEPS)
    def _():
        os_ref[...] = out


def moe_combine(pos1, pos2, x, route, g_final, y):
    tm = COMBINE_ROWS
    last_prompt = COMBINE_PROMPT_STEPS - 1
    return pl.pallas_call(
        _combine_kernel,
        grid_spec=pltpu.PrefetchScalarGridSpec(
            num_scalar_prefetch=2,
            grid=(M_TOK // tm,),
            in_specs=[pl.BlockSpec((tm, D), lambda i, p1, p2: (i, 0)),
                      pl.BlockSpec((tm, LANES), lambda i, p1, p2: (i, 0)),
                      pl.BlockSpec((1, D), lambda i, p1, p2: (0, 0)),
                      pl.BlockSpec(memory_space=pl.ANY)],
            out_specs=[pl.BlockSpec((tm, D), lambda i, p1, p2: (jnp.minimum(i, last_prompt), 0)),
                       pl.BlockSpec((M_SAMPLE, D), lambda i, p1, p2: (0, 0))],
            scratch_shapes=[pltpu.VMEM((2, tm, D), F32), pltpu.VMEM((2, tm, D), F32),
                            pltpu.SemaphoreType.DMA((2, 2))]),
        out_shape=[jax.ShapeDtypeStruct((M_PROMPT, D), F32),
                   jax.ShapeDtypeStruct((M_SAMPLE, D), F32)],
        compiler_params=_params(("arbitrary",), 40),
        name="moe_combine",
    )(pos1, pos2, x, route, g_final.reshape(1, D), y)


def _group_tables(idx):
    e_flat = jnp.concatenate([idx[:, 0], idx[:, 1]])
    onehot = (e_flat[:, None] == jnp.arange(N_EXPERTS, dtype=jnp.int32)[None, :]).astype(jnp.int32)
    csum = jnp.cumsum(onehot, axis=0)
    counts = csum[-1]
    rank = jnp.sum(csum * onehot, axis=1) - 1
    ntiles = (counts + GROUP_ROWS - 1) // GROUP_ROWS
    tile_end = jnp.cumsum(ntiles)
    tstart = tile_end - ntiles
    dest = jnp.sum(onehot * (tstart * GROUP_ROWS)[None, :], axis=1) + rank
    token = jnp.concatenate([jnp.arange(M_TOK, dtype=jnp.int32)] * 2)
    row_token = jnp.zeros((R_PAD,), jnp.int32).at[dest].set(token)
    return row_token, dest[:M_TOK], dest[M_TOK:], tstart, ntiles


def kernel(x_prompt, x_sample, state_conv, state_gla, norm_mix, norm_ffn, norm_final,
           conv_w_in, conv_w, conv_w_out, gla_w_in, gla_w_gate, gla_b_gate, gla_norm, gla_w_out,
           ffn_w_gate, ffn_w_up, ffn_w_down, moe_w_router, moe_w_gate, moe_w_up, moe_w_down):
    x0, h = stack_norm(x_prompt.reshape(M_PROMPT, D), x_sample.reshape(M_SAMPLE, D), norm_mix[0])
    bz, conv_prompt_state, s0, s1 = conv_mixer(
        h, conv_w_in, conv_w[0], state_conv[0].reshape(M_SAMPLE, 2 * D))
    conv_sample_state = jnp.stack([s0, s1], axis=1).reshape(1, M_SAMPLE, 2, D)
    x1, h = linear_res_norm(bz, conv_w_out, 0, x0, norm_ffn[0], name="conv_out")

    a = swiglu_up(h, ffn_w_gate, ffn_w_up, 0)
    x2 = linear(a, ffn_w_down, 0, D, tn=512, tm=640, res=x1, name="ffn_down")

    h = rmsnorm_bf16(x2, norm_mix[1])
    p = linear(h, jnp.swapaxes(gla_w_in, 1, 2), 0, 2 * HK + 2 * HV, tn=1024, tm=TM_DENSE,
               name="gla_in", w_is_nk=True)
    w_gate_pad = jnp.pad(gla_w_gate[0], ((0, LANES - GATE_RANK), (0, 0)))
    w_a_pad = jnp.pad(gla_w_in[0, :, 2 * HK + 2 * HV:], ((0, 0), (0, LANES - GATE_RANK)))
    g = gla_gate(h, w_a_pad, w_gate_pad, gla_b_gate[0].reshape(1, HK))
    gn = gla_norm[0].reshape(1, DV)
    gated, gla_prompt_state = gla_prompt(p, g, gn)
    gated, gla_sample_state = gla_sample(gated, p, g, gn, state_gla[0])
    x3 = linear(gated, gla_w_out, 0, D, tn=1024, tm=TM_DENSE, res=x2, name="gla_out")

    w_router_pad = jnp.pad(moe_w_router[0], ((0, 0), (0, LANES - N_EXPERTS)))
    route, idx = router(x3, norm_ffn[1], w_router_pad)
    row_token, pos1, pos2, tstart, ntiles = _group_tables(idx)
    act, _ = grouped_matmul(None, (moe_w_gate, moe_w_up), tstart, ntiles, tm=GROUP_ROWS, tn=1792,
                            ck=256, out_dtype=BF16, vmem_mb=56, name="moe_up",
                            gather_from=(row_token, x3, norm_ffn[1]))
    y = grouped_matmul(act, (moe_w_down,), tstart, ntiles, tm=GROUP_ROWS, tn=1024, ck=896,
                       out_dtype=F32, vmem_mb=56, name="moe_down")
    y_prompt, y_sample = moe_combine(pos1, pos2, x3, route, norm_final, y)

    y_prompt = y_prompt.reshape(N_PROMPT_SEQ, SEQ, D)
    y_sample = y_sample.reshape(M_SAMPLE, 1, D)
    return (y_prompt, y_sample,
            conv_prompt_state.reshape(1, N_PROMPT_SEQ, 2, D), conv_sample_state,
            gla_prompt_state.reshape(1, N_PROMPT_SEQ, HEADS, DK, DV),
            gla_sample_state.reshape(1, M_SAMPLE, HEADS, DK, DV))
```

```python
import functools

import jax
import jax.numpy as jnp
from jax import lax
from jax.experimental import pallas as pl
from jax.experimental.pallas import tpu as pltpu

F32 = jnp.float32
BF16 = jnp.bfloat16

D = 2048
N_PROMPT_SEQ = 4
SEQ = 2048
M_PROMPT = N_PROMPT_SEQ * SEQ
M_SAMPLE = 128
M_TOK = M_PROMPT + M_SAMPLE
HEADS = 4
DK = 256
DV = 512
HK = HEADS * DK
HV = HEADS * DV
GATE_RANK = 16
GATE_TAU = 16.0
CHUNK = 64
N_CHUNKS = SEQ // CHUNK
D_FF = 5632
N_EXPERTS = 8
D_FF_EXPERT = 7168
EPS = 1e-6
LANES = 128

TM_DENSE = 1040
GROUP_ROWS = 256
N_ASSIGN = 2 * M_TOK
N_GROUP_TILES = N_ASSIGN // GROUP_ROWS + N_EXPERTS
R_PAD = N_GROUP_TILES * GROUP_ROWS


def _params(sem, vmem_mb):
    return pltpu.CompilerParams(dimension_semantics=sem,
                                vmem_limit_bytes=vmem_mb * 1024 * 1024)


def _rmsnorm(x, g):
    return x * lax.rsqrt(jnp.mean(x * x, axis=-1, keepdims=True) + EPS) * g


def _silu(x):
    return x * jax.nn.sigmoid(x)


def _norm_kernel(x_ref, g_ref, o_ref):
    o_ref[...] = _rmsnorm(x_ref[...], g_ref[...]).astype(o_ref.dtype)


def rmsnorm_bf16(x, g, tr=832):
    m = x.shape[0]
    return pl.pallas_call(
        _norm_kernel,
        grid=(m // tr,),
        in_specs=[pl.BlockSpec((tr, D), lambda i: (i, 0)),
                  pl.BlockSpec((1, D), lambda i: (0, 0))],
        out_specs=pl.BlockSpec((tr, D), lambda i: (i, 0)),
        out_shape=jax.ShapeDtypeStruct((m, D), BF16),
        compiler_params=_params(("arbitrary",), 40),
        name="rmsnorm",
    )(x, g.reshape(1, D))


STACK_ROWS = 512
STACK_PROMPT_STEPS = M_PROMPT // STACK_ROWS


def _stack_norm_kernel(xp_ref, xs_ref, g_ref, x_ref, h_ref):
    i = pl.program_id(0)
    g = g_ref[...]

    @pl.when(i < STACK_PROMPT_STEPS)
    def _():
        x = xp_ref[...]
        x_ref[...] = x
        h_ref[...] = _rmsnorm(x, g).astype(h_ref.dtype)

    @pl.when(i == STACK_PROMPT_STEPS)
    def _():
        x = xs_ref[...]
        x_ref[0:M_SAMPLE, :] = x
        h_ref[0:M_SAMPLE, :] = _rmsnorm(x, g).astype(h_ref.dtype)


def stack_norm(xp, xs, g):
    last_prompt = STACK_PROMPT_STEPS - 1
    return pl.pallas_call(
        _stack_norm_kernel,
        grid=(STACK_PROMPT_STEPS + 1,),
        in_specs=[pl.BlockSpec((STACK_ROWS, D), lambda i: (jnp.minimum(i, last_prompt), 0)),
                  pl.BlockSpec((M_SAMPLE, D), lambda i: (0, 0)),
                  pl.BlockSpec((1, D), lambda i: (0, 0))],
        out_specs=[pl.BlockSpec((STACK_ROWS, D), lambda i: (i, 0)),
                   pl.BlockSpec((STACK_ROWS, D), lambda i: (i, 0))],
        out_shape=[jax.ShapeDtypeStruct((M_TOK, D), F32),
                   jax.ShapeDtypeStruct((M_TOK, D), BF16)],
        compiler_params=_params(("arbitrary",), 40),
        name="stack_norm",
    )(xp, xs, g.reshape(1, D))


def _linear_kernel(*refs, has_res, w_is_nk):
    if has_res:
        x_ref, w_ref, r_ref, o_ref, wb_ref = refs
    else:
        x_ref, w_ref, o_ref, wb_ref = refs

    @pl.when(pl.program_id(1) == 0)
    def _():
        wb_ref[...] = w_ref[...].astype(BF16)

    contract_w = 1 if w_is_nk else 0
    acc = lax.dot_general(x_ref[...], wb_ref[...], (((1,), (contract_w,)), ((), ())),
                          preferred_element_type=F32)
    if has_res:
        acc = acc + r_ref[...]
    o_ref[...] = acc.astype(o_ref.dtype)


def linear(x, w, layer, n_out, tn, tm, res=None, out_dtype=F32, vmem_mb=56, name="linear",
           w_is_nk=False):
    m, k = x.shape
    wblock = (None, tn, k) if w_is_nk else (None, k, tn)
    wmap = (lambda j, i: (layer, j, 0)) if w_is_nk else (lambda j, i: (layer, 0, j))
    in_specs = [pl.BlockSpec((tm, k), lambda j, i: (i, 0)),
                pl.BlockSpec(wblock, wmap)]
    args = [x, w]
    if res is not None:
        in_specs.append(pl.BlockSpec((tm, tn), lambda j, i: (i, j)))
        args.append(res)
    return pl.pallas_call(
        functools.partial(_linear_kernel, has_res=res is not None, w_is_nk=w_is_nk),
        grid=(n_out // tn, m // tm),
        in_specs=in_specs,
        out_specs=pl.BlockSpec((tm, tn), lambda j, i: (i, j)),
        out_shape=jax.ShapeDtypeStruct((m, n_out), out_dtype),
        scratch_shapes=[pltpu.VMEM(wblock[1:], BF16)],
        compiler_params=_params(("arbitrary", "arbitrary"), vmem_mb),
        name=name,
    )(*args)


def _linear_res_norm_kernel(x_ref, w_ref, r_ref, g_ref, o_ref, h_ref, wb_ref):
    @pl.when(pl.program_id(0) == 0)
    def _():
        wb_ref[...] = w_ref[...].astype(BF16)

    y = jnp.dot(x_ref[...], wb_ref[...], preferred_element_type=F32) + r_ref[...]
    o_ref[...] = y
    h_ref[...] = _rmsnorm(y, g_ref[...]).astype(h_ref.dtype)


def linear_res_norm(x, w, layer, res, g, tm=416, name="linear_res_norm"):
    m, k = x.shape
    return pl.pallas_call(
        _linear_res_norm_kernel,
        grid=(m // tm,),
        in_specs=[pl.BlockSpec((tm, k), lambda i: (i, 0)),
                  pl.BlockSpec((None, k, D), lambda i: (layer, 0, 0), pipeline_mode=pl.Buffered(1)),
                  pl.BlockSpec((tm, D), lambda i: (i, 0)),
                  pl.BlockSpec((1, D), lambda i: (0, 0))],
        out_specs=[pl.BlockSpec((tm, D), lambda i: (i, 0)),
                   pl.BlockSpec((tm, D), lambda i: (i, 0))],
        out_shape=[jax.ShapeDtypeStruct((m, D), F32),
                   jax.ShapeDtypeStruct((m, D), BF16)],
        scratch_shapes=[pltpu.VMEM((k, D), BF16)],
        compiler_params=_params(("arbitrary",), 56),
        name=name,
    )(x, w, res, g.reshape(1, D))


def _swiglu_kernel(x_ref, wg_ref, wu_ref, o_ref, wgb_ref, wub_ref):
    @pl.when(pl.program_id(1) == 0)
    def _():
        wgb_ref[...] = wg_ref[...].astype(BF16)
        wub_ref[...] = wu_ref[...].astype(BF16)

    x = x_ref[...]
    g = jnp.dot(x, wgb_ref[...], preferred_element_type=F32)
    u = jnp.dot(x, wub_ref[...], preferred_element_type=F32)
    o_ref[...] = (_silu(g) * u).astype(o_ref.dtype)


def swiglu_up(x, wg, wu, layer, tn=512, tm=TM_DENSE):
    m, k = x.shape
    f = wg.shape[-1]
    wspec = pl.BlockSpec((None, k, tn), lambda j, i: (layer, 0, j))
    return pl.pallas_call(
        _swiglu_kernel,
        grid=(f // tn, m // tm),
        in_specs=[pl.BlockSpec((tm, k), lambda j, i: (i, 0)), wspec, wspec],
        out_specs=pl.BlockSpec((tm, tn), lambda j, i: (i, j)),
        out_shape=jax.ShapeDtypeStruct((m, f), BF16),
        scratch_shapes=[pltpu.VMEM((k, tn), BF16), pltpu.VMEM((k, tn), BF16)],
        compiler_params=_params(("arbitrary", "arbitrary"), 56),
        name="ffn_up",
    )(x, wg, wu)


CONV_TM = 512
CONV_TC = 512
CONV_TILES_PER_SEQ = SEQ // CONV_TM
CONV_PROMPT_TILES = M_PROMPT // CONV_TM


def _conv_kernel(x_ref, xs_ref, wh_ref, wb_ref, wc_ref, cw_ref, b0_ref, b1_ref,
                 bz_ref, stp_ref, s0_ref, s1_ref, wbf_ref, carry_ref):
    i = pl.program_id(1)

    @pl.when(i == 0)
    def _():
        wbf_ref[0] = wh_ref[...].astype(BF16)
        wbf_ref[1] = wb_ref[...].astype(BF16)
        wbf_ref[2] = wc_ref[...].astype(BF16)

    def project(x):
        hh = jnp.dot(x, wbf_ref[0], preferred_element_type=F32)
        bb = jnp.dot(x, wbf_ref[1], preferred_element_type=F32)
        cc = jnp.dot(x, wbf_ref[2], preferred_element_type=F32)
        return bb, cc * hh

    w = cw_ref[...]

    @pl.when(i < CONV_PROMPT_TILES)
    def _():
        @pl.when(i % CONV_TILES_PER_SEQ == 0)
        def _():
            carry_ref[...] = jnp.zeros_like(carry_ref)

        bb, u = project(x_ref[...])
        c2 = carry_ref[0:1, :]
        c1 = carry_ref[1:2, :]
        row = lax.broadcasted_iota(jnp.int32, u.shape, 0)
        u1 = jnp.where(row == 0, c1, pltpu.roll(u, 1, 0))
        u2 = jnp.where(row == 0, c2, jnp.where(row == 1, c1, pltpu.roll(u, 2, 0)))
        z = w[0:1, :] * u2
        z = z + w[1:2, :] * u1
        z = z + w[2:3, :] * u
        bz_ref[...] = (bb * z).astype(bz_ref.dtype)
        tail = u[CONV_TM - 2:CONV_TM, :]
        carry_ref[0:2, :] = tail
        stp_ref[...] = tail

    @pl.when(i == CONV_PROMPT_TILES)
    def _():
        bb, u = project(xs_ref[...])
        b1 = b1_ref[...]
        z = w[0:1, :] * b0_ref[...]
        z = z + w[1:2, :] * b1
        z = z + w[2:3, :] * u
        bz_ref[0:M_SAMPLE, :] = (bb * z).astype(bz_ref.dtype)
        s0_ref[...] = b1
        s1_ref[...] = u


def conv_mixer(h, w_in, conv_w, buf):
    tm, tc = CONV_TM, CONV_TC
    nj = D // tc
    last_tile = CONV_PROMPT_TILES - 1
    last_seq = N_PROMPT_SEQ - 1
    wspec = lambda part: pl.BlockSpec((None, D, tc), lambda j, i: (0, 0, part * nj + j))
    return pl.pallas_call(
        _conv_kernel,
        grid=(nj, CONV_PROMPT_TILES + 1),
        in_specs=[pl.BlockSpec((tm, D), lambda j, i: (jnp.minimum(i, last_tile), 0)),
                  pl.BlockSpec((M_SAMPLE, D), lambda j, i: (M_PROMPT // M_SAMPLE, 0)),
                  wspec(0), wspec(1), wspec(2),
                  pl.BlockSpec((3, tc), lambda j, i: (0, j)),
                  pl.BlockSpec((M_SAMPLE, tc), lambda j, i: (0, j)),
                  pl.BlockSpec((M_SAMPLE, tc), lambda j, i: (0, nj + j))],
        out_specs=[pl.BlockSpec((tm, tc), lambda j, i: (i, j)),
                   pl.BlockSpec((None, 2, tc),
                                lambda j, i: (jnp.minimum(i // CONV_TILES_PER_SEQ, last_seq), 0, j)),
                   pl.BlockSpec((M_SAMPLE, tc), lambda j, i: (0, j)),
                   pl.BlockSpec((M_SAMPLE, tc), lambda j, i: (0, j))],
        out_shape=[jax.ShapeDtypeStruct((M_TOK, D), BF16),
                   jax.ShapeDtypeStruct((N_PROMPT_SEQ, 2, D), F32),
                   jax.ShapeDtypeStruct((M_SAMPLE, D), F32),
                   jax.ShapeDtypeStruct((M_SAMPLE, D), F32)],
        scratch_shapes=[pltpu.VMEM((3, D, tc), BF16), pltpu.VMEM((8, tc), F32)],
        compiler_params=_params(("arbitrary", "arbitrary"), 56),
        name="conv_mixer",
    )(h, h, w_in, w_in, w_in, conv_w, buf, buf)


def _log_sigmoid(x):
    return jnp.minimum(x, 0.0) - jnp.log1p(jnp.exp(-jnp.abs(x)))


def _gla_gate_kernel(h_ref, wa_ref, wg_ref, bg_ref, g_ref):
    a = jnp.dot(h_ref[...], wa_ref[...].astype(BF16), preferred_element_type=F32)
    z = jnp.dot(a.astype(BF16), wg_ref[...].astype(BF16), preferred_element_type=F32) + bg_ref[...]
    g_ref[...] = _log_sigmoid(z) * (1.0 / GATE_TAU)


def gla_gate(h, w_a_pad, w_gate_pad, b_gate, tm=TM_DENSE):
    return pl.pallas_call(
        _gla_gate_kernel,
        grid=(M_TOK // tm,),
        in_specs=[pl.BlockSpec((tm, D), lambda i: (i, 0)),
                  pl.BlockSpec((D, LANES), lambda i: (0, 0)),
                  pl.BlockSpec((LANES, HK), lambda i: (0, 0)),
                  pl.BlockSpec((1, HK), lambda i: (0, 0))],
        out_specs=pl.BlockSpec((tm, HK), lambda i: (i, 0)),
        out_shape=jax.ShapeDtypeStruct((M_TOK, HK), F32),
        compiler_params=_params(("arbitrary",), 40),
        name="gla_gate",
    )(h, w_a_pad, w_gate_pad, b_gate)


def _row_to_cols(row):
    return jnp.transpose(jnp.broadcast_to(row, (LANES, row.shape[1])))


def _split3_bf16(x):
    x1 = x.astype(BF16)
    r1 = x - x1.astype(F32)
    x2 = r1.astype(BF16)
    x3 = (r1 - x2.astype(F32)).astype(BF16)
    return x1, x2, x3


def _gla_prompt_kernel(o_init, q_ref, k_ref, v_ref, r_ref, g_ref, gn_ref, o_ref, sout_ref, s_ref):
    del o_init
    c = pl.program_id(1)

    @pl.when(c == 0)
    def _():
        s_ref[...] = jnp.zeros_like(s_ref)

    row = lax.broadcasted_iota(jnp.int32, (CHUNK, CHUNK), 0)
    col = lax.broadcasted_iota(jnp.int32, (CHUNK, CHUNK), 1)
    tri = row >= col
    trib = tri.astype(BF16)

    g1, g2, g3 = _split3_bf16(g_ref[...])
    b = (jnp.dot(trib, g1, preferred_element_type=F32)
         + jnp.dot(trib, g2, preferred_element_type=F32)
         + jnp.dot(trib, g3, preferred_element_type=F32))
    b_last = b[CHUNK - 1:CHUNK, :]
    q = q_ref[...] * (DK ** -0.5)
    k = k_ref[...]
    q_dec = (q * jnp.exp(b)).astype(BF16)
    k_inv = (k * jnp.exp(-b)).astype(BF16)
    k_end = (k * jnp.exp(b_last - b)).astype(BF16)
    decay = jnp.exp(b_last)
    gn = gn_ref[...]

    for h in range(HEADS):
        ks = slice(h * DK, (h + 1) * DK)
        vs = slice(h * DV, (h + 1) * DV)
        vb = v_ref[:, vs].astype(BF16)
        s_old = s_ref[h]
        scores = lax.dot_general(q_dec[:, ks], k_inv[:, ks], (((1,), (1,)), ((), ())),
                                 preferred_element_type=F32)
        scores = jnp.where(tri, scores, 0.0).astype(BF16)
        o = (jnp.dot(scores, vb, preferred_element_type=F32)
             + jnp.dot(q_dec[:, ks], s_old.astype(BF16), preferred_element_type=F32))
        kv = lax.dot_general(k_end[:, ks], vb, (((0,), (0,)), ((), ())),
                             preferred_element_type=F32)
        dcol = _row_to_cols(decay[:, ks])
        s_ref[h] = jnp.concatenate(
            [s_old[:, j * LANES:(j + 1) * LANES] * dcol for j in range(DV // LANES)], axis=1) + kv
        on = _rmsnorm(o, gn)
        o_ref[:, vs] = (_silu(r_ref[:, vs]) * on).astype(o_ref.dtype)

    @pl.when(c == N_CHUNKS - 1)
    def _():
        sout_ref[...] = s_ref[...]


def gla_prompt(p, g, g_norm):
    rows = lambda n, c: n * N_CHUNKS + c
    return pl.pallas_call(
        _gla_prompt_kernel,
        grid=(N_PROMPT_SEQ, N_CHUNKS),
        in_specs=[pl.BlockSpec(memory_space=pl.ANY),
                  pl.BlockSpec((CHUNK, HK), lambda n, c: (rows(n, c), 0)),
                  pl.BlockSpec((CHUNK, HK), lambda n, c: (rows(n, c), 1)),
                  pl.BlockSpec((CHUNK, HV), lambda n, c: (rows(n, c), 1)),
                  pl.BlockSpec((CHUNK, HV), lambda n, c: (rows(n, c), 2)),
                  pl.BlockSpec((CHUNK, HK), lambda n, c: (rows(n, c), 0)),
                  pl.BlockSpec((1, DV), lambda n, c: (0, 0))],
        out_specs=[pl.BlockSpec((CHUNK, HV), lambda n, c: (rows(n, c), 0)),
                   pl.BlockSpec((None, HEADS, DK, DV), lambda n, c: (n, 0, 0, 0))],
        out_shape=[jax.ShapeDtypeStruct((M_TOK, HV), BF16),
                   jax.ShapeDtypeStruct((N_PROMPT_SEQ, HEADS, DK, DV), F32)],
        scratch_shapes=[pltpu.VMEM((HEADS, DK, DV), F32)],
        input_output_aliases={0: 0},
        compiler_params=_params(("arbitrary", "arbitrary"), 40),
        name="gla_prompt",
    )(jnp.zeros((M_TOK, HV), BF16), p, p, p, p, g, g_norm)


GLA_SAMPLE_SEQS = 2


def _gla_sample_kernel(gated_any, q_ref, k_ref, v_ref, r_ref, g_ref, gn_ref, s_ref,
                       gated_ref, sout_ref, qt_ref, kt_ref, et_ref, o_scr):
    del gated_any
    i = pl.program_id(0)

    @pl.when(i == 0)
    def _():
        q = q_ref[...] * (DK ** -0.5)
        k = k_ref[...]
        e = jnp.exp(g_ref[...])
        for h in range(HEADS):
            ks = slice(h * DK, (h + 1) * DK)
            qt_ref[h] = jnp.transpose(q[:, ks])
            kt_ref[h] = jnp.transpose(k[:, ks])
            et_ref[h] = jnp.transpose(e[:, ks])

    lane = lax.broadcasted_iota(jnp.int32, (DK, M_SAMPLE), 1)
    for s in range(GLA_SAMPLE_SEQS):
        n = i * GLA_SAMPLE_SEQS + s
        pick = lane == n

        def column(t):
            return jnp.sum(jnp.where(pick, t, 0.0), axis=1, keepdims=True)

        for h in range(HEADS):
            vs = slice(h * DV, (h + 1) * DV)
            v_row = v_ref[pl.ds(n, 1), vs]
            s_new = s_ref[s, h] * column(et_ref[h]) + column(kt_ref[h]) * v_row
            sout_ref[s, h] = s_new
            o_scr[pl.ds(n, 1), vs] = jnp.sum(column(qt_ref[h]) * s_new, axis=0, keepdims=True)

    @pl.when(i == pl.num_programs(0) - 1)
    def _():
        gn = gn_ref[...]
        for h in range(HEADS):
            vs = slice(h * DV, (h + 1) * DV)
            on = _rmsnorm(o_scr[:, vs], gn)
            gated_ref[:, vs] = (_silu(r_ref[:, vs]) * on).astype(gated_ref.dtype)


def gla_sample(gated, p, g, g_norm, state):
    rb = M_PROMPT // M_SAMPLE
    bs = GLA_SAMPLE_SEQS
    return pl.pallas_call(
        _gla_sample_kernel,
        grid=(M_SAMPLE // bs,),
        in_specs=[pl.BlockSpec(memory_space=pl.ANY),
                  pl.BlockSpec((M_SAMPLE, HK), lambda i: (rb, 0)),
                  pl.BlockSpec((M_SAMPLE, HK), lambda i: (rb, 1)),
                  pl.BlockSpec((M_SAMPLE, HV), lambda i: (rb, 1)),
                  pl.BlockSpec((M_SAMPLE, HV), lambda i: (rb, 2)),
                  pl.BlockSpec((M_SAMPLE, HK), lambda i: (rb, 0)),
                  pl.BlockSpec((1, DV), lambda i: (0, 0)),
                  pl.BlockSpec((bs, HEADS, DK, DV), lambda i: (i, 0, 0, 0))],
        out_specs=[pl.BlockSpec((M_SAMPLE, HV), lambda i: (rb, 0)),
                   pl.BlockSpec((bs, HEADS, DK, DV), lambda i: (i, 0, 0, 0))],
        out_shape=[jax.ShapeDtypeStruct((M_TOK, HV), BF16),
                   jax.ShapeDtypeStruct((M_SAMPLE, HEADS, DK, DV), F32)],
        scratch_shapes=[pltpu.VMEM((HEADS, DK, M_SAMPLE), F32),
                        pltpu.VMEM((HEADS, DK, M_SAMPLE), F32),
                        pltpu.VMEM((HEADS, DK, M_SAMPLE), F32),
                        pltpu.VMEM((M_SAMPLE, HV), F32)],
        input_output_aliases={0: 0},
        compiler_params=_params(("arbitrary",), 40),
        name="gla_sample",
    )(gated, p, p, p, p, g, g_norm, state)


def _router_kernel(x_ref, g_ref, wr_ref, route_ref, idx_ref):
    h = _rmsnorm(x_ref[...], g_ref[...])
    logits = jnp.dot(h, wr_ref[...], preferred_element_type=F32, precision=lax.Precision.HIGHEST)
    lane = lax.broadcasted_iota(jnp.int32, logits.shape, 1)
    lane_f = lane.astype(F32)
    neg = jnp.float32(-jnp.inf)
    logits = jnp.where(lane < N_EXPERTS, logits, neg)
    m1 = jnp.max(logits, axis=1, keepdims=True)
    i1 = jnp.min(jnp.where(logits == m1, lane_f, float(LANES)), axis=1, keepdims=True)
    rest = jnp.where(lane_f == i1, neg, logits)
    m2 = jnp.max(rest, axis=1, keepdims=True)
    i2 = jnp.min(jnp.where(rest == m2, lane_f, float(LANES)), axis=1, keepdims=True)
    e2 = jnp.exp(m2 - m1)
    den = 1.0 + e2
    w1 = 1.0 / den
    w2 = e2 / den
    route_ref[...] = jnp.where(lane == 0, w1, jnp.where(lane == 1, w2, 0.0))
    idx_ref[...] = jnp.where(lane == 0, i1, jnp.where(lane == 1, i2, 0.0)).astype(jnp.int32)


def router(x, g, w_router_pad, tm=416):
    return pl.pallas_call(
        _router_kernel,
        grid=(M_TOK // tm,),
        in_specs=[pl.BlockSpec((tm, D), lambda i: (i, 0)),
                  pl.BlockSpec((1, D), lambda i: (0, 0)),
                  pl.BlockSpec((D, LANES), lambda i: (0, 0))],
        out_specs=[pl.BlockSpec((tm, LANES), lambda i: (i, 0)),
                   pl.BlockSpec((tm, LANES), lambda i: (i, 0))],
        out_shape=[jax.ShapeDtypeStruct((M_TOK, LANES), F32),
                   jax.ShapeDtypeStruct((M_TOK, LANES), jnp.int32)],
        compiler_params=_params(("arbitrary",), 40),
        name="router",
    )(x, g.reshape(1, D), w_router_pad)


def _grouped_kernel(*refs, n_w, n_col, total_tiles, gather):
    if gather:
        tstart, ntiles, tok, src_hbm, gn_ref = refs[:5]
        w_hbm = refs[5:5 + n_w]
        o_hbm, x_hbm = refs[5 + n_w:7 + n_w]
        wbf, stage, xbuf, obuf, xsem, osem, wsem, gbuf, gsem, xwsem = refs[7 + n_w:]
    else:
        tstart, ntiles, x_hbm = refs[:3]
        w_hbm = refs[3:3 + n_w]
        o_hbm = refs[3 + n_w]
        wbf, stage, xbuf, obuf, xsem, osem, wsem = refs[4 + n_w:]
    s = pl.program_id(0)
    n_items = pl.num_programs(0)
    e = s // n_col
    j = lax.rem(s, n_col)
    p = lax.rem(s, 2)
    tm = xbuf.shape[1]
    tn = obuf.shape[2]
    ck = stage.shape[2]
    n_chunks = wbf.shape[2] // ck
    nt = ntiles[e]
    t0 = tstart[e]
    col = pl.multiple_of(j * tn, tn)
    has_next = s + 1 < n_items

    def w_copies(item, c, q):
        row = pl.multiple_of(c * ck, ck)
        wcol = pl.multiple_of(lax.rem(item, n_col) * tn, tn)
        return [pltpu.make_async_copy(w.at[0, item // n_col, pl.ds(row, ck), pl.ds(wcol, tn)],
                                      stage.at[q, i], wsem.at[q, i])
                for i, w in enumerate(w_hbm)]

    def w_start(item, c, q):
        for cp in w_copies(item, c, q):
            cp.start(priority=1)

    def w_chunk(item, c, dst):
        q = lax.rem(c, 2)

        @pl.when(c + 1 < n_chunks)
        def _():
            w_start(item, c + 1, 1 - q)

        row = pl.multiple_of(c * ck, ck)
        for i, cp in enumerate(w_copies(item, c, q)):
            cp.wait()
            wbf[dst, i, pl.ds(row, ck), :] = stage[q, i].astype(BF16)

    @pl.when(s == 0)
    def _():
        w_start(0, 0, 0)

        def first(c, carry):
            w_chunk(0, c, 0)
            return carry

        lax.fori_loop(0, n_chunks, first, 0)

    @pl.when(has_next)
    def _():
        w_start(s + 1, 0, 0)

    def x_copy(tile, slot):
        row = pl.multiple_of(tile * tm, tm)
        return pltpu.make_async_copy(x_hbm.at[pl.ds(row, tm)], xbuf.at[slot], xsem.at[slot])

    def o_copy(tile, slot):
        row = pl.multiple_of(tile * tm, tm)
        return pltpu.make_async_copy(obuf.at[slot], o_hbm.at[pl.ds(row, tm), pl.ds(col, tn)],
                                     osem.at[slot])

    nx = xbuf.shape[0]

    def when_fetching(fn):
        if gather:
            pl.when(j > 0)(fn)
        else:
            fn()

    def g_issue(t, gslot):
        base = (t0 + t) * tm

        def rows(r, carry):
            pltpu.make_async_copy(src_hbm.at[pl.ds(tok[base + r], 1)],
                                  gbuf.at[gslot, pl.ds(r, 1)], gsem.at[gslot]).start()
            return carry

        lax.fori_loop(0, tm, rows, 0, unroll=8)

    def xw_copy(tile, xslot):
        row = pl.multiple_of(tile * tm, tm)
        return pltpu.make_async_copy(xbuf.at[xslot], x_hbm.at[pl.ds(row, tm)], xwsem.at[xslot])

    @when_fetching
    def _():
        for d in range(nx - 1):
            @pl.when(d < nt)
            def _():
                x_copy(t0 + d, d).start()

    if gather:
        ng = gbuf.shape[0]
        for d in range(ng - 1):
            @pl.when((j == 0) & (d < nt))
            def _():
                g_issue(d, d)

    def body(t, chunks_done):
        slot = lax.rem(t, 2)
        xslot = lax.rem(t, nx)

        @when_fetching
        def _():
            x_copy(t0 + t, xslot).wait()
            ahead = t + (nx - 1)

            @pl.when(ahead < nt)
            def _():
                x_copy(t0 + ahead, lax.rem(ahead, nx)).start()

        if gather:
            @pl.when(j == 0)
            def _():
                gslot = lax.rem(t, ng)
                g_ahead = t + (ng - 1)

                @pl.when(g_ahead < nt)
                def _():
                    g_issue(g_ahead, lax.rem(g_ahead, ng))

                pltpu.make_async_copy(src_hbm.at[pl.ds(0, tm)], gbuf.at[gslot], gsem.at[gslot]).wait()

                @pl.when(t >= nx)
                def _():
                    xw_copy(t0 + t - nx, xslot).wait()

                xbuf[xslot] = _rmsnorm(gbuf[gslot], gn_ref[...]).astype(xbuf.dtype)
                xw_copy(t0 + t, xslot).start()

        @pl.when(t >= 2)
        def _():
            o_copy(t0 + t - 2, slot).wait()

        x = xbuf[xslot]
        if n_w == 2:
            g = jnp.dot(x, wbf[p, 0], preferred_element_type=F32)
            u = jnp.dot(x, wbf[p, 1], preferred_element_type=F32)
            obuf[slot] = (_silu(g) * u).astype(obuf.dtype)
        else:
            obuf[slot] = jnp.dot(x, wbf[p, 0], preferred_element_type=F32).astype(obuf.dtype)
        o_copy(t0 + t, slot).start()

        stream = has_next & (chunks_done < n_chunks)

        @pl.when(stream)
        def _():
            w_chunk(s + 1, chunks_done, 1 - p)

        return chunks_done + stream.astype(jnp.int32)

    chunks_done = lax.fori_loop(0, nt, body, jnp.int32(0))

    @pl.when(has_next)
    def _():
        def rest(c, carry):
            w_chunk(s + 1, c, 1 - p)
            return carry

        lax.fori_loop(chunks_done, n_chunks, rest, 0)

    @pl.when(nt >= 2)
    def _():
        o_copy(t0 + nt - 2, lax.rem(nt, 2)).wait()

    @pl.when(nt >= 1)
    def _():
        o_copy(t0 + nt - 1, lax.rem(nt + 1, 2)).wait()

    if gather:
        @pl.when(j == 0)
        def _():
            for d in range(nx):
                @pl.when(nt > d)
                def _():
                    tile = nt - 1 - d
                    xw_copy(t0 + tile, lax.rem(tile, nx)).wait()

    @pl.when(e == N_EXPERTS - 1)
    def _():
        obuf[0] = jnp.zeros(obuf.shape[1:], obuf.dtype)

        def zero_tile(tile, carry):
            cp = o_copy(tile, 0)
            cp.start()
            cp.wait()
            return carry

        lax.fori_loop(t0 + nt, total_tiles, zero_tile, 0)

        if gather:
            @pl.when(j == 0)
            def _():
                xbuf[0] = jnp.zeros(xbuf.shape[1:], xbuf.dtype)

                def zero_x(tile, carry):
                    cp = xw_copy(tile, 0)
                    cp.start()
                    cp.wait()
                    return carry

                lax.fori_loop(t0 + nt, total_tiles, zero_x, 0)


def grouped_matmul(x, ws, tstart, ntiles, tm, tn, ck, out_dtype, vmem_mb, name, gather_from=None):
    n_x = 3
    n = ws[0].shape[-1]
    n_w = len(ws)
    n_col = n // tn
    any_spec = pl.BlockSpec(memory_space=pl.ANY)
    gather = gather_from is not None
    if gather:
        row_token, src, g = gather_from
        k = src.shape[1]
        prefetch = (tstart, ntiles, row_token)
        inputs = (src, g.reshape(1, k)) + tuple(ws)
        in_specs = [any_spec, pl.BlockSpec((1, k), lambda s, *_: (0, 0))] + [any_spec] * n_w
        out_specs = [any_spec, any_spec]
        out_shape = [jax.ShapeDtypeStruct((R_PAD, n), out_dtype),
                     jax.ShapeDtypeStruct((R_PAD, k), BF16)]
        extra_scratch = [pltpu.VMEM((3, tm, k), F32), pltpu.SemaphoreType.DMA((3,)),
                         pltpu.SemaphoreType.DMA((n_x,))]
    else:
        k = x.shape[1]
        prefetch = (tstart, ntiles)
        inputs = (x,) + tuple(ws)
        in_specs = [any_spec] * (1 + n_w)
        out_specs = any_spec
        out_shape = jax.ShapeDtypeStruct((R_PAD, n), out_dtype)
        extra_scratch = []
    return pl.pallas_call(
        functools.partial(_grouped_kernel, n_w=n_w, n_col=n_col, total_tiles=R_PAD // tm,
                          gather=gather),
        grid_spec=pltpu.PrefetchScalarGridSpec(
            num_scalar_prefetch=len(prefetch),
            grid=(N_EXPERTS * n_col,),
            in_specs=in_specs,
            out_specs=out_specs,
            scratch_shapes=[pltpu.VMEM((2, n_w, k, tn), BF16), pltpu.VMEM((2, n_w, ck, tn), F32),
                            pltpu.VMEM((n_x, tm, k), BF16), pltpu.VMEM((2, tm, tn), out_dtype),
                            pltpu.SemaphoreType.DMA((n_x,)), pltpu.SemaphoreType.DMA((2,)),
                            pltpu.SemaphoreType.DMA((2, n_w))] + extra_scratch),
        out_shape=out_shape,
        compiler_params=_params(("arbitrary",), vmem_mb),
        name=name,
    )(*prefetch, *inputs)


COMBINE_ROWS = 128
COMBINE_PROMPT_STEPS = M_PROMPT // COMBINE_ROWS


def _combine_kernel(p1_ref, p2_ref, x_ref, route_ref, g_ref, y_hbm, op_ref, os_ref, b1, b2, sem):
    i = pl.program_id(0)
    tm = b1.shape[1]
    slot = lax.rem(i, 2)

    def issue(tile, dst_slot):
        base = tile * tm

        def body(r, carry):
            pltpu.make_async_copy(y_hbm.at[pl.ds(p1_ref[base + r], 1)], b1.at[dst_slot, pl.ds(r, 1)],
                                  sem.at[0, dst_slot]).start(priority=0)
            pltpu.make_async_copy(y_hbm.at[pl.ds(p2_ref[base + r], 1)], b2.at[dst_slot, pl.ds(r, 1)],
                                  sem.at[1, dst_slot]).start(priority=1)
            return carry

        lax.fori_loop(0, tm, body, 0, unroll=8)

    @pl.when(i == 0)
    def _():
        issue(0, 0)

    @pl.when(i + 1 < pl.num_programs(0))
    def _():
        issue(i + 1, 1 - slot)

    pltpu.make_async_copy(y_hbm.at[pl.ds(0, tm)], b1.at[slot], sem.at[0, slot]).wait()
    pltpu.make_async_copy(y_hbm.at[pl.ds(0, tm)], b2.at[slot], sem.at[1, slot]).wait()
    route = route_ref[...]
    w1 = route[:, 0:1]
    w2 = route[:, 1:2]
    x = x_ref[...] + (w1 * b1[slot] + w2 * b2[slot])
    out = _rmsnorm(x, g_ref[...])

    @pl.when(i < COMBINE_PROMPT_STEPS)
    def _():
        op_ref[...] = out

    @pl.when(i == COMBINE_PROMPT_STEPS)
    def _():
        os_ref[...] = out


def moe_combine(pos1, pos2, x, route, g_final, y):
    tm = COMBINE_ROWS
    last_prompt = COMBINE_PROMPT_STEPS - 1
    return pl.pallas_call(
        _combine_kernel,
        grid_spec=pltpu.PrefetchScalarGridSpec(
            num_scalar_prefetch=2,
            grid=(M_TOK // tm,),
            in_specs=[pl.BlockSpec((tm, D), lambda i, p1, p2: (i, 0)),
                      pl.BlockSpec((tm, LANES), lambda i, p1, p2: (i, 0)),
                      pl.BlockSpec((1, D), lambda i, p1, p2: (0, 0)),
                      pl.BlockSpec(memory_space=pl.ANY)],
            out_specs=[pl.BlockSpec((tm, D), lambda i, p1, p2: (jnp.minimum(i, last_prompt), 0)),
                       pl.BlockSpec((M_SAMPLE, D), lambda i, p1, p2: (0, 0))],
            scratch_shapes=[pltpu.VMEM((2, tm, D), F32), pltpu.VMEM((2, tm, D), F32),
                            pltpu.SemaphoreType.DMA((2, 2))]),
        out_shape=[jax.ShapeDtypeStruct((M_PROMPT, D), F32),
                   jax.ShapeDtypeStruct((M_SAMPLE, D), F32)],
        compiler_params=_params(("arbitrary",), 40),
        name="moe_combine",
    )(pos1, pos2, x, route, g_final.reshape(1, D), y)


def _group_tables(idx):
    e_flat = jnp.concatenate([idx[:, 0], idx[:, 1]])
    onehot = (e_flat[:, None] == jnp.arange(N_EXPERTS, dtype=jnp.int32)[None, :]).astype(jnp.int32)
    csum = jnp.cumsum(onehot, axis=0)
    counts = csum[-1]
    rank = jnp.sum(csum * onehot, axis=1) - 1
    ntiles = (counts + GROUP_ROWS - 1) // GROUP_ROWS
    tile_end = jnp.cumsum(ntiles)
    tstart = tile_end - ntiles
    dest = jnp.sum(onehot * (tstart * GROUP_ROWS)[None, :], axis=1) + rank
    token = jnp.concatenate([jnp.arange(M_TOK, dtype=jnp.int32)] * 2)
    row_token = jnp.zeros((R_PAD,), jnp.int32).at[dest].set(token)
    return row_token, dest[:M_TOK], dest[M_TOK:], tstart, ntiles


def kernel(x_prompt, x_sample, state_conv, state_gla, norm_mix, norm_ffn, norm_final,
           conv_w_in, conv_w, conv_w_out, gla_w_in, gla_w_gate, gla_b_gate, gla_norm, gla_w_out,
           ffn_w_gate, ffn_w_up, ffn_w_down, moe_w_router, moe_w_gate, moe_w_up, moe_w_down):
    x0, h = stack_norm(x_prompt.reshape(M_PROMPT, D), x_sample.reshape(M_SAMPLE, D), norm_mix[0])
    bz, conv_prompt_state, s0, s1 = conv_mixer(
        h, conv_w_in, conv_w[0], state_conv[0].reshape(M_SAMPLE, 2 * D))
    conv_sample_state = jnp.stack([s0, s1], axis=1).reshape(1, M_SAMPLE, 2, D)
    x1, h = linear_res_norm(bz, conv_w_out, 0, x0, norm_ffn[0], name="conv_out")

    a = swiglu_up(h, ffn_w_gate, ffn_w_up, 0)
    x2 = linear(a, ffn_w_down, 0, D, tn=512, tm=640, res=x1, name="ffn_down")

    h = rmsnorm_bf16(x2, norm_mix[1])
    p = linear(h, jnp.swapaxes(gla_w_in, 1, 2), 0, 2 * HK + 2 * HV, tn=1024, tm=TM_DENSE,
               name="gla_in", w_is_nk=True)
    w_gate_pad = jnp.pad(gla_w_gate[0], ((0, LANES - GATE_RANK), (0, 0)))
    w_a_pad = jnp.pad(gla_w_in[0, :, 2 * HK + 2 * HV:], ((0, 0), (0, LANES - GATE_RANK)))
    g = gla_gate(h, w_a_pad, w_gate_pad, gla_b_gate[0].reshape(1, HK))
    gn = gla_norm[0].reshape(1, DV)
    gated, gla_prompt_state = gla_prompt(p, g, gn)
    gated, gla_sample_state = gla_sample(gated, p, g, gn, state_gla[0])
    x3 = linear(gated, gla_w_out, 0, D, tn=1024, tm=TM_DENSE, res=x2, name="gla_out")

    w_router_pad = jnp.pad(moe_w_router[0], ((0, 0), (0, LANES - N_EXPERTS)))
    route, idx = router(x3, norm_ffn[1], w_router_pad)
    row_token, pos1, pos2, tstart, ntiles = _group_tables(idx)
    act, _ = grouped_matmul(None, (moe_w_gate, moe_w_up), tstart, ntiles, tm=GROUP_ROWS, tn=1792,
                            ck=256, out_dtype=BF16, vmem_mb=56, name="moe_up",
                            gather_from=(row_token, x3, norm_ffn[1]))
    y = grouped_matmul(act, (moe_w_down,), tstart, ntiles, tm=GROUP_ROWS, tn=1024, ck=896,
                       out_dtype=F32, vmem_mb=56, name="moe_down")
    y_prompt, y_sample = moe_combine(pos1, pos2, x3, route, norm_final, y)

    y_prompt = y_prompt.reshape(N_PROMPT_SEQ, SEQ, D)
    y_sample = y_sample.reshape(M_SAMPLE, 1, D)
    return (y_prompt, y_sample,
            conv_prompt_state.reshape(1, N_PROMPT_SEQ, 2, D), conv_sample_state,
            gla_prompt_state.reshape(1, N_PROMPT_SEQ, HEADS, DK, DV),
            gla_sample_state.reshape(1, M_SAMPLE, HEADS, DK, DV))
```

```python
import functools

import jax
import jax.numpy as jnp
from jax import lax
from jax.experimental import pallas as pl
from jax.experimental.pallas import tpu as pltpu

F32 = jnp.float32
BF16 = jnp.bfloat16

D = 2048
N_PROMPT_SEQ = 4
SEQ = 2048
M_PROMPT = N_PROMPT_SEQ * SEQ
M_SAMPLE = 128
M_TOK = M_PROMPT + M_SAMPLE
HEADS = 4
DK = 256
DV = 512
HK = HEADS * DK
HV = HEADS * DV
GATE_RANK = 16
GATE_TAU = 16.0
CHUNK = 64
N_CHUNKS = SEQ // CHUNK
D_FF = 5632
N_EXPERTS = 8
D_FF_EXPERT = 7168
EPS = 1e-6
LANES = 128

TM_DENSE = 1040
GROUP_ROWS = 256
N_ASSIGN = 2 * M_TOK
N_GROUP_TILES = N_ASSIGN // GROUP_ROWS + N_EXPERTS
R_PAD = N_GROUP_TILES * GROUP_ROWS


def _params(sem, vmem_mb):
    return pltpu.CompilerParams(dimension_semantics=sem,
                                vmem_limit_bytes=vmem_mb * 1024 * 1024)


def _rmsnorm(x, g):
    return x * lax.rsqrt(jnp.mean(x * x, axis=-1, keepdims=True) + EPS) * g


def _silu(x):
    return x * jax.nn.sigmoid(x)


def _norm_kernel(x_ref, g_ref, o_ref):
    o_ref[...] = _rmsnorm(x_ref[...], g_ref[...]).astype(o_ref.dtype)


def rmsnorm_bf16(x, g, tr=832):
    m = x.shape[0]
    return pl.pallas_call(
        _norm_kernel,
        grid=(m // tr,),
        in_specs=[pl.BlockSpec((tr, D), lambda i: (i, 0)),
                  pl.BlockSpec((1, D), lambda i: (0, 0))],
        out_specs=pl.BlockSpec((tr, D), lambda i: (i, 0)),
        out_shape=jax.ShapeDtypeStruct((m, D), BF16),
        compiler_params=_params(("arbitrary",), 40),
        name="rmsnorm",
    )(x, g.reshape(1, D))


STACK_ROWS = 512
STACK_PROMPT_STEPS = M_PROMPT // STACK_ROWS


def _stack_norm_kernel(xp_ref, xs_ref, g_ref, x_ref, h_ref):
    i = pl.program_id(0)
    g = g_ref[...]

    @pl.when(i < STACK_PROMPT_STEPS)
    def _():
        x = xp_ref[...]
        x_ref[...] = x
        h_ref[...] = _rmsnorm(x, g).astype(h_ref.dtype)

    @pl.when(i == STACK_PROMPT_STEPS)
    def _():
        x = xs_ref[...]
        x_ref[0:M_SAMPLE, :] = x
        h_ref[0:M_SAMPLE, :] = _rmsnorm(x, g).astype(h_ref.dtype)


def stack_norm(xp, xs, g):
    last_prompt = STACK_PROMPT_STEPS - 1
    return pl.pallas_call(
        _stack_norm_kernel,
        grid=(STACK_PROMPT_STEPS + 1,),
        in_specs=[pl.BlockSpec((STACK_ROWS, D), lambda i: (jnp.minimum(i, last_prompt), 0)),
                  pl.BlockSpec((M_SAMPLE, D), lambda i: (0, 0)),
                  pl.BlockSpec((1, D), lambda i: (0, 0))],
        out_specs=[pl.BlockSpec((STACK_ROWS, D), lambda i: (i, 0)),
                   pl.BlockSpec((STACK_ROWS, D), lambda i: (i, 0))],
        out_shape=[jax.ShapeDtypeStruct((M_TOK, D), F32),
                   jax.ShapeDtypeStruct((M_TOK, D), BF16)],
        compiler_params=_params(("arbitrary",), 40),
        name="stack_norm",
    )(xp, xs, g.reshape(1, D))


def _linear_kernel(*refs, has_res, w_is_nk):
    if has_res:
        x_ref, w_ref, r_ref, o_ref, wb_ref = refs
    else:
        x_ref, w_ref, o_ref, wb_ref = refs

    @pl.when(pl.program_id(1) == 0)
    def _():
        wb_ref[...] = w_ref[...].astype(BF16)

    contract_w = 1 if w_is_nk else 0
    acc = lax.dot_general(x_ref[...], wb_ref[...], (((1,), (contract_w,)), ((), ())),
                          preferred_element_type=F32)
    if has_res:
        acc = acc + r_ref[...]
    o_ref[...] = acc.astype(o_ref.dtype)


def linear(x, w, layer, n_out, tn, tm, res=None, out_dtype=F32, vmem_mb=56, name="linear",
           w_is_nk=False):
    m, k = x.shape
    wblock = (None, tn, k) if w_is_nk else (None, k, tn)
    wmap = (lambda j, i: (layer, j, 0)) if w_is_nk else (lambda j, i: (layer, 0, j))
    in_specs = [pl.BlockSpec((tm, k), lambda j, i: (i, 0)),
                pl.BlockSpec(wblock, wmap)]
    args = [x, w]
    if res is not None:
        in_specs.append(pl.BlockSpec((tm, tn), lambda j, i: (i, j)))
        args.append(res)
    return pl.pallas_call(
        functools.partial(_linear_kernel, has_res=res is not None, w_is_nk=w_is_nk),
        grid=(n_out // tn, m // tm),
        in_specs=in_specs,
        out_specs=pl.BlockSpec((tm, tn), lambda j, i: (i, j)),
        out_shape=jax.ShapeDtypeStruct((m, n_out), out_dtype),
        scratch_shapes=[pltpu.VMEM(wblock[1:], BF16)],
        compiler_params=_params(("arbitrary", "arbitrary"), vmem_mb),
        name=name,
    )(*args)


def _linear_res_norm_kernel(x_ref, w_ref, r_ref, g_ref, o_ref, h_ref, wb_ref):
    @pl.when(pl.program_id(0) == 0)
    def _():
        wb_ref[...] = w_ref[...].astype(BF16)

    y = jnp.dot(x_ref[...], wb_ref[...], preferred_element_type=F32) + r_ref[...]
    o_ref[...] = y
    h_ref[...] = _rmsnorm(y, g_ref[...]).astype(h_ref.dtype)


def linear_res_norm(x, w, layer, res, g, tm=416, name="linear_res_norm"):
    m, k = x.shape
    return pl.pallas_call(
        _linear_res_norm_kernel,
        grid=(m // tm,),
        in_specs=[pl.BlockSpec((tm, k), lambda i: (i, 0)),
                  pl.BlockSpec((None, k, D), lambda i: (layer, 0, 0), pipeline_mode=pl.Buffered(1)),
                  pl.BlockSpec((tm, D), lambda i: (i, 0)),
                  pl.BlockSpec((1, D), lambda i: (0, 0))],
        out_specs=[pl.BlockSpec((tm, D), lambda i: (i, 0)),
                   pl.BlockSpec((tm, D), lambda i: (i, 0))],
        out_shape=[jax.ShapeDtypeStruct((m, D), F32),
                   jax.ShapeDtypeStruct((m, D), BF16)],
        scratch_shapes=[pltpu.VMEM((k, D), BF16)],
        compiler_params=_params(("arbitrary",), 56),
        name=name,
    )(x, w, res, g.reshape(1, D))


def _swiglu_kernel(x_ref, wg_ref, wu_ref, o_ref, wgb_ref, wub_ref):
    @pl.when(pl.program_id(1) == 0)
    def _():
        wgb_ref[...] = wg_ref[...].astype(BF16)
        wub_ref[...] = wu_ref[...].astype(BF16)

    x = x_ref[...]
    g = jnp.dot(x, wgb_ref[...], preferred_element_type=F32)
    u = jnp.dot(x, wub_ref[...], preferred_element_type=F32)
    o_ref[...] = (_silu(g) * u).astype(o_ref.dtype)


def swiglu_up(x, wg, wu, layer, tn=512, tm=TM_DENSE):
    m, k = x.shape
    f = wg.shape[-1]
    wspec = pl.BlockSpec((None, k, tn), lambda j, i: (layer, 0, j))
    return pl.pallas_call(
        _swiglu_kernel,
        grid=(f // tn, m // tm),
        in_specs=[pl.BlockSpec((tm, k), lambda j, i: (i, 0)), wspec, wspec],
        out_specs=pl.BlockSpec((tm, tn), lambda j, i: (i, j)),
        out_shape=jax.ShapeDtypeStruct((m, f), BF16),
        scratch_shapes=[pltpu.VMEM((k, tn), BF16), pltpu.VMEM((k, tn), BF16)],
        compiler_params=_params(("arbitrary", "arbitrary"), 56),
        name="ffn_up",
    )(x, wg, wu)


CONV_TM = 512
CONV_TC = 512
CONV_TILES_PER_SEQ = SEQ // CONV_TM
CONV_PROMPT_TILES = M_PROMPT // CONV_TM


def _conv_kernel(x_ref, xs_ref, wh_ref, wb_ref, wc_ref, cw_ref, b0_ref, b1_ref,
                 bz_ref, stp_ref, s0_ref, s1_ref, wbf_ref, carry_ref):
    i = pl.program_id(1)

    @pl.when(i == 0)
    def _():
        wbf_ref[0] = wh_ref[...].astype(BF16)
        wbf_ref[1] = wb_ref[...].astype(BF16)
        wbf_ref[2] = wc_ref[...].astype(BF16)

    def project(x):
        hh = jnp.dot(x, wbf_ref[0], preferred_element_type=F32)
        bb = jnp.dot(x, wbf_ref[1], preferred_element_type=F32)
        cc = jnp.dot(x, wbf_ref[2], preferred_element_type=F32)
        return bb, cc * hh

    w = cw_ref[...]

    @pl.when(i < CONV_PROMPT_TILES)
    def _():
        @pl.when(i % CONV_TILES_PER_SEQ == 0)
        def _():
            carry_ref[...] = jnp.zeros_like(carry_ref)

        bb, u = project(x_ref[...])
        c2 = carry_ref[0:1, :]
        c1 = carry_ref[1:2, :]
        row = lax.broadcasted_iota(jnp.int32, u.shape, 0)
        u1 = jnp.where(row == 0, c1, pltpu.roll(u, 1, 0))
        u2 = jnp.where(row == 0, c2, jnp.where(row == 1, c1, pltpu.roll(u, 2, 0)))
        z = w[0:1, :] * u2
        z = z + w[1:2, :] * u1
        z = z + w[2:3, :] * u
        bz_ref[...] = (bb * z).astype(bz_ref.dtype)
        tail = u[CONV_TM - 2:CONV_TM, :]
        carry_ref[0:2, :] = tail
        stp_ref[...] = tail

    @pl.when(i == CONV_PROMPT_TILES)
    def _():
        bb, u = project(xs_ref[...])
        b1 = b1_ref[...]
        z = w[0:1, :] * b0_ref[...]
        z = z + w[1:2, :] * b1
        z = z + w[2:3, :] * u
        bz_ref[0:M_SAMPLE, :] = (bb * z).astype(bz_ref.dtype)
        s0_ref[...] = b1
        s1_ref[...] = u


def conv_mixer(h, w_in, conv_w, buf):
    tm, tc = CONV_TM, CONV_TC
    nj = D // tc
    last_tile = CONV_PROMPT_TILES - 1
    last_seq = N_PROMPT_SEQ - 1
    wspec = lambda part: pl.BlockSpec((None, D, tc), lambda j, i: (0, 0, part * nj + j))
    return pl.pallas_call(
        _conv_kernel,
        grid=(nj, CONV_PROMPT_TILES + 1),
        in_specs=[pl.BlockSpec((tm, D), lambda j, i: (jnp.minimum(i, last_tile), 0)),
                  pl.BlockSpec((M_SAMPLE, D), lambda j, i: (M_PROMPT // M_SAMPLE, 0)),
                  wspec(0), wspec(1), wspec(2),
                  pl.BlockSpec((3, tc), lambda j, i: (0, j)),
                  pl.BlockSpec((M_SAMPLE, tc), lambda j, i: (0, j)),
                  pl.BlockSpec((M_SAMPLE, tc), lambda j, i: (0, nj + j))],
        out_specs=[pl.BlockSpec((tm, tc), lambda j, i: (i, j)),
                   pl.BlockSpec((None, 2, tc),
                                lambda j, i: (jnp.minimum(i // CONV_TILES_PER_SEQ, last_seq), 0, j)),
                   pl.BlockSpec((M_SAMPLE, tc), lambda j, i: (0, j)),
                   pl.BlockSpec((M_SAMPLE, tc), lambda j, i: (0, j))],
        out_shape=[jax.ShapeDtypeStruct((M_TOK, D), BF16),
                   jax.ShapeDtypeStruct((N_PROMPT_SEQ, 2, D), F32),
                   jax.ShapeDtypeStruct((M_SAMPLE, D), F32),
                   jax.ShapeDtypeStruct((M_SAMPLE, D), F32)],
        scratch_shapes=[pltpu.VMEM((3, D, tc), BF16), pltpu.VMEM((8, tc), F32)],
        compiler_params=_params(("arbitrary", "arbitrary"), 56),
        name="conv_mixer",
    )(h, h, w_in, w_in, w_in, conv_w, buf, buf)


def _log_sigmoid(x):
    return jnp.minimum(x, 0.0) - jnp.log1p(jnp.exp(-jnp.abs(x)))


def _gla_gate_kernel(h_ref, wa_ref, wg_ref, bg_ref, g_ref):
    a = jnp.dot(h_ref[...], wa_ref[...].astype(BF16), preferred_element_type=F32)
    z = jnp.dot(a.astype(BF16), wg_ref[...].astype(BF16), preferred_element_type=F32) + bg_ref[...]
    g_ref[...] = _log_sigmoid(z) * (1.0 / GATE_TAU)


def gla_gate(h, w_a_pad, w_gate_pad, b_gate, tm=TM_DENSE):
    return pl.pallas_call(
        _gla_gate_kernel,
        grid=(M_TOK // tm,),
        in_specs=[pl.BlockSpec((tm, D), lambda i: (i, 0)),
                  pl.BlockSpec((D, LANES), lambda i: (0, 0)),
                  pl.BlockSpec((LANES, HK), lambda i: (0, 0)),
                  pl.BlockSpec((1, HK), lambda i: (0, 0))],
        out_specs=pl.BlockSpec((tm, HK), lambda i: (i, 0)),
        out_shape=jax.ShapeDtypeStruct((M_TOK, HK), F32),
        compiler_params=_params(("arbitrary",), 40),
        name="gla_gate",
    )(h, w_a_pad, w_gate_pad, b_gate)


def _row_to_cols(row):
    return jnp.transpose(jnp.broadcast_to(row, (LANES, row.shape[1])))


def _split3_bf16(x):
    x1 = x.astype(BF16)
    r1 = x - x1.astype(F32)
    x2 = r1.astype(BF16)
    x3 = (r1 - x2.astype(F32)).astype(BF16)
    return x1, x2, x3


def _gla_prompt_kernel(o_init, q_ref, k_ref, v_ref, r_ref, g_ref, gn_ref, o_ref, sout_ref, s_ref):
    del o_init
    c = pl.program_id(1)

    @pl.when(c == 0)
    def _():
        s_ref[...] = jnp.zeros_like(s_ref)

    row = lax.broadcasted_iota(jnp.int32, (CHUNK, CHUNK), 0)
    col = lax.broadcasted_iota(jnp.int32, (CHUNK, CHUNK), 1)
    tri = row >= col
    trib = tri.astype(BF16)

    g1, g2, g3 = _split3_bf16(g_ref[...])
    b = (jnp.dot(trib, g1, preferred_element_type=F32)
         + jnp.dot(trib, g2, preferred_element_type=F32)
         + jnp.dot(trib, g3, preferred_element_type=F32))
    b_last = b[CHUNK - 1:CHUNK, :]
    q = q_ref[...] * (DK ** -0.5)
    k = k_ref[...]
    q_dec = (q * jnp.exp(b)).astype(BF16)
    k_inv = (k * jnp.exp(-b)).astype(BF16)
    k_end = (k * jnp.exp(b_last - b)).astype(BF16)
    decay = jnp.exp(b_last)
    gn = gn_ref[...]

    for h in range(HEADS):
        ks = slice(h * DK, (h + 1) * DK)
        vs = slice(h * DV, (h + 1) * DV)
        vb = v_ref[:, vs].astype(BF16)
        s_old = s_ref[h]
        scores = lax.dot_general(q_dec[:, ks], k_inv[:, ks], (((1,), (1,)), ((), ())),
                                 preferred_element_type=F32)
        scores = jnp.where(tri, scores, 0.0).astype(BF16)
        o = (jnp.dot(scores, vb, preferred_element_type=F32)
             + jnp.dot(q_dec[:, ks], s_old.astype(BF16), preferred_element_type=F32))
        kv = lax.dot_general(k_end[:, ks], vb, (((0,), (0,)), ((), ())),
                             preferred_element_type=F32)
        dcol = _row_to_cols(decay[:, ks])
        s_ref[h] = jnp.concatenate(
            [s_old[:, j * LANES:(j + 1) * LANES] * dcol for j in range(DV // LANES)], axis=1) + kv
        on = _rmsnorm(o, gn)
        o_ref[:, vs] = (_silu(r_ref[:, vs]) * on).astype(o_ref.dtype)

    @pl.when(c == N_CHUNKS - 1)
    def _():
        sout_ref[...] = s_ref[...]


def gla_prompt(p, g, g_norm):
    rows = lambda n, c: n * N_CHUNKS + c
    return pl.pallas_call(
        _gla_prompt_kernel,
        grid=(N_PROMPT_SEQ, N_CHUNKS),
        in_specs=[pl.BlockSpec(memory_space=pl.ANY),
                  pl.BlockSpec((CHUNK, HK), lambda n, c: (rows(n, c), 0)),
                  pl.BlockSpec((CHUNK, HK), lambda n, c: (rows(n, c), 1)),
                  pl.BlockSpec((CHUNK, HV), lambda n, c: (rows(n, c), 1)),
                  pl.BlockSpec((CHUNK, HV), lambda n, c: (rows(n, c), 2)),
                  pl.BlockSpec((CHUNK, HK), lambda n, c: (rows(n, c), 0)),
                  pl.BlockSpec((1, DV), lambda n, c: (0, 0))],
        out_specs=[pl.BlockSpec((CHUNK, HV), lambda n, c: (rows(n, c), 0)),
                   pl.BlockSpec((None, HEADS, DK, DV), lambda n, c: (n, 0, 0, 0))],
        out_shape=[jax.ShapeDtypeStruct((M_TOK, HV), BF16),
                   jax.ShapeDtypeStruct((N_PROMPT_SEQ, HEADS, DK, DV), F32)],
        scratch_shapes=[pltpu.VMEM((HEADS, DK, DV), F32)],
        input_output_aliases={0: 0},
        compiler_params=_params(("arbitrary", "arbitrary"), 40),
        name="gla_prompt",
    )(jnp.zeros((M_TOK, HV), BF16), p, p, p, p, g, g_norm)


GLA_SAMPLE_SEQS = 2


def _gla_sample_kernel(gated_any, q_ref, k_ref, v_ref, r_ref, g_ref, gn_ref, s_ref,
                       gated_ref, sout_ref, qt_ref, kt_ref, et_ref, o_scr):
    del gated_any
    i = pl.program_id(0)

    @pl.when(i == 0)
    def _():
        q = q_ref[...] * (DK ** -0.5)
        k = k_ref[...]
        e = jnp.exp(g_ref[...])
        for h in range(HEADS):
            ks = slice(h * DK, (h + 1) * DK)
            qt_ref[h] = jnp.transpose(q[:, ks])
            kt_ref[h] = jnp.transpose(k[:, ks])
            et_ref[h] = jnp.transpose(e[:, ks])

    lane = lax.broadcasted_iota(jnp.int32, (DK, M_SAMPLE), 1)
    for s in range(GLA_SAMPLE_SEQS):
        n = i * GLA_SAMPLE_SEQS + s
        pick = lane == n

        def column(t):
            return jnp.sum(jnp.where(pick, t, 0.0), axis=1, keepdims=True)

        for h in range(HEADS):
            vs = slice(h * DV, (h + 1) * DV)
            v_row = v_ref[pl.ds(n, 1), vs]
            s_new = s_ref[s, h] * column(et_ref[h]) + column(kt_ref[h]) * v_row
            sout_ref[s, h] = s_new
            o_scr[pl.ds(n, 1), vs] = jnp.sum(column(qt_ref[h]) * s_new, axis=0, keepdims=True)

    @pl.when(i == pl.num_programs(0) - 1)
    def _():
        gn = gn_ref[...]
        for h in range(HEADS):
            vs = slice(h * DV, (h + 1) * DV)
            on = _rmsnorm(o_scr[:, vs], gn)
            gated_ref[:, vs] = (_silu(r_ref[:, vs]) * on).astype(gated_ref.dtype)


def gla_sample(gated, p, g, g_norm, state):
    rb = M_PROMPT // M_SAMPLE
    bs = GLA_SAMPLE_SEQS
    return pl.pallas_call(
        _gla_sample_kernel,
        grid=(M_SAMPLE // bs,),
        in_specs=[pl.BlockSpec(memory_space=pl.ANY),
                  pl.BlockSpec((M_SAMPLE, HK), lambda i: (rb, 0)),
                  pl.BlockSpec((M_SAMPLE, HK), lambda i: (rb, 1)),
                  pl.BlockSpec((M_SAMPLE, HV), lambda i: (rb, 1)),
                  pl.BlockSpec((M_SAMPLE, HV), lambda i: (rb, 2)),
                  pl.BlockSpec((M_SAMPLE, HK), lambda i: (rb, 0)),
                  pl.BlockSpec((1, DV), lambda i: (0, 0)),
                  pl.BlockSpec((bs, HEADS, DK, DV), lambda i: (i, 0, 0, 0))],
        out_specs=[pl.BlockSpec((M_SAMPLE, HV), lambda i: (rb, 0)),
                   pl.BlockSpec((bs, HEADS, DK, DV), lambda i: (i, 0, 0, 0))],
        out_shape=[jax.ShapeDtypeStruct((M_TOK, HV), BF16),
                   jax.ShapeDtypeStruct((M_SAMPLE, HEADS, DK, DV), F32)],
        scratch_shapes=[pltpu.VMEM((HEADS, DK, M_SAMPLE), F32),
                        pltpu.VMEM((HEADS, DK, M_SAMPLE), F32),
                        pltpu.VMEM((HEADS, DK, M_SAMPLE), F32),
                        pltpu.VMEM((M_SAMPLE, HV), F32)],
        input_output_aliases={0: 0},
        compiler_params=_params(("arbitrary",), 40),
        name="gla_sample",
    )(gated, p, p, p, p, g, g_norm, state)


def _router_kernel(x_ref, g_ref, wr_ref, route_ref, idx_ref):
    h = _rmsnorm(x_ref[...], g_ref[...])
    logits = jnp.dot(h, wr_ref[...], preferred_element_type=F32, precision=lax.Precision.HIGHEST)
    lane = lax.broadcasted_iota(jnp.int32, logits.shape, 1)
    lane_f = lane.astype(F32)
    neg = jnp.float32(-jnp.inf)
    logits = jnp.where(lane < N_EXPERTS, logits, neg)
    m1 = jnp.max(logits, axis=1, keepdims=True)
    i1 = jnp.min(jnp.where(logits == m1, lane_f, float(LANES)), axis=1, keepdims=True)
    rest = jnp.where(lane_f == i1, neg, logits)
    m2 = jnp.max(rest, axis=1, keepdims=True)
    i2 = jnp.min(jnp.where(rest == m2, lane_f, float(LANES)), axis=1, keepdims=True)
    e2 = jnp.exp(m2 - m1)
    den = 1.0 + e2
    w1 = 1.0 / den
    w2 = e2 / den
    route_ref[...] = jnp.where(lane == 0, w1, jnp.where(lane == 1, w2, 0.0))
    idx_ref[...] = jnp.where(lane == 0, i1, jnp.where(lane == 1, i2, 0.0)).astype(jnp.int32)


def router(x, g, w_router_pad, tm=416):
    return pl.pallas_call(
        _router_kernel,
        grid=(M_TOK // tm,),
        in_specs=[pl.BlockSpec((tm, D), lambda i: (i, 0)),
                  pl.BlockSpec((1, D), lambda i: (0, 0)),
                  pl.BlockSpec((D, LANES), lambda i: (0, 0))],
        out_specs=[pl.BlockSpec((tm, LANES), lambda i: (i, 0)),
                   pl.BlockSpec((tm, LANES), lambda i: (i, 0))],
        out_shape=[jax.ShapeDtypeStruct((M_TOK, LANES), F32),
                   jax.ShapeDtypeStruct((M_TOK, LANES), jnp.int32)],
        compiler_params=_params(("arbitrary",), 40),
        name="router",
    )(x, g.reshape(1, D), w_router_pad)


def _grouped_kernel(*refs, n_w, n_col, total_tiles, gather):
    if gather:
        tstart, ntiles, dest1, dest2, src_hbm, gn_ref = refs[:6]
        w_hbm = refs[6:6 + n_w]
        o_hbm, x_hbm = refs[6 + n_w:8 + n_w]
        wbf, stage, xbuf, obuf, xsem, osem, wsem, gbuf, gsem, xwsem, tok = refs[8 + n_w:]
    else:
        tstart, ntiles, x_hbm = refs[:3]
        w_hbm = refs[3:3 + n_w]
        o_hbm = refs[3 + n_w]
        wbf, stage, xbuf, obuf, xsem, osem, wsem = refs[4 + n_w:]
    s = pl.program_id(0)
    n_items = pl.num_programs(0)
    e = s // n_col
    j = lax.rem(s, n_col)
    p = lax.rem(s, 2)
    tm = xbuf.shape[1]
    tn = obuf.shape[2]
    ck = stage.shape[2]
    n_chunks = wbf.shape[2] // ck
    nt = ntiles[e]
    t0 = tstart[e]
    col = pl.multiple_of(j * tn, tn)
    has_next = s + 1 < n_items

    def w_copies(item, c, q):
        row = pl.multiple_of(c * ck, ck)
        wcol = pl.multiple_of(lax.rem(item, n_col) * tn, tn)
        return [pltpu.make_async_copy(w.at[0, item // n_col, pl.ds(row, ck), pl.ds(wcol, tn)],
                                      stage.at[q, i], wsem.at[q, i])
                for i, w in enumerate(w_hbm)]

    def w_start(item, c, q):
        for cp in w_copies(item, c, q):
            cp.start(priority=1)

    def w_chunk(item, c, dst):
        q = lax.rem(c, 2)

        @pl.when(c + 1 < n_chunks)
        def _():
            w_start(item, c + 1, 1 - q)

        row = pl.multiple_of(c * ck, ck)
        for i, cp in enumerate(w_copies(item, c, q)):
            cp.wait()
            wbf[dst, i, pl.ds(row, ck), :] = stage[q, i].astype(BF16)

    @pl.when(s == 0)
    def _():
        w_start(0, 0, 0)

        if gather:
            def clear(r, carry):
                tok[r] = 0
                return carry

            lax.fori_loop(0, tok.shape[0], clear, 0, unroll=8)

            def invert(t, carry):
                tok[dest1[t]] = t
                tok[dest2[t]] = t
                return carry

            lax.fori_loop(0, dest1.shape[0], invert, 0, unroll=8)

        def first(c, carry):
            w_chunk(0, c, 0)
            return carry

        lax.fori_loop(0, n_chunks, first, 0)

    @pl.when(has_next)
    def _():
        w_start(s + 1, 0, 0)

    def x_copy(tile, slot):
        row = pl.multiple_of(tile * tm, tm)
        return pltpu.make_async_copy(x_hbm.at[pl.ds(row, tm)], xbuf.at[slot], xsem.at[slot])

    def o_copy(tile, slot):
        row = pl.multiple_of(tile * tm, tm)
        return pltpu.make_async_copy(obuf.at[slot], o_hbm.at[pl.ds(row, tm), pl.ds(col, tn)],
                                     osem.at[slot])

    nx = xbuf.shape[0]

    def when_fetching(fn):
        if gather:
            pl.when(j > 0)(fn)
        else:
            fn()

    def g_issue(t, gslot):
        base = (t0 + t) * tm

        def rows(r, carry):
            pltpu.make_async_copy(src_hbm.at[pl.ds(tok[base + r], 1)],
                                  gbuf.at[gslot, pl.ds(r, 1)], gsem.at[gslot]).start()
            return carry

        lax.fori_loop(0, tm, rows, 0, unroll=8)

    def xw_copy(tile, xslot):
        row = pl.multiple_of(tile * tm, tm)
        return pltpu.make_async_copy(xbuf.at[xslot], x_hbm.at[pl.ds(row, tm)], xwsem.at[xslot])

    @when_fetching
    def _():
        for d in range(nx - 1):
            @pl.when(d < nt)
            def _():
                x_copy(t0 + d, d).start()

    if gather:
        ng = gbuf.shape[0]
        for d in range(ng - 1):
            @pl.when((j == 0) & (d < nt))
            def _():
                g_issue(d, d)

    def body(t, chunks_done):
        slot = lax.rem(t, 2)
        xslot = lax.rem(t, nx)

        @when_fetching
        def _():
            x_copy(t0 + t, xslot).wait()
            ahead = t + (nx - 1)

            @pl.when(ahead < nt)
            def _():
                x_copy(t0 + ahead, lax.rem(ahead, nx)).start()

        if gather:
            @pl.when(j == 0)
            def _():
                gslot = lax.rem(t, ng)
                g_ahead = t + (ng - 1)

                @pl.when(g_ahead < nt)
                def _():
                    g_issue(g_ahead, lax.rem(g_ahead, ng))

                pltpu.make_async_copy(src_hbm.at[pl.ds(0, tm)], gbuf.at[gslot], gsem.at[gslot]).wait()

                @pl.when(t >= nx)
                def _():
                    xw_copy(t0 + t - nx, xslot).wait()

                xbuf[xslot] = _rmsnorm(gbuf[gslot], gn_ref[...]).astype(xbuf.dtype)
                xw_copy(t0 + t, xslot).start()

        @pl.when(t >= 2)
        def _():
            o_copy(t0 + t - 2, slot).wait()

        x = xbuf[xslot]
        if n_w == 2:
            g = jnp.dot(x, wbf[p, 0], preferred_element_type=F32)
            u = jnp.dot(x, wbf[p, 1], preferred_element_type=F32)
            obuf[slot] = (_silu(g) * u).astype(obuf.dtype)
        else:
            obuf[slot] = jnp.dot(x, wbf[p, 0], preferred_element_type=F32).astype(obuf.dtype)
        o_copy(t0 + t, slot).start()

        stream = has_next & (chunks_done < n_chunks)

        @pl.when(stream)
        def _():
            w_chunk(s + 1, chunks_done, 1 - p)

        return chunks_done + stream.astype(jnp.int32)

    chunks_done = lax.fori_loop(0, nt, body, jnp.int32(0))

    @pl.when(has_next)
    def _():
        def rest(c, carry):
            w_chunk(s + 1, c, 1 - p)
            return carry

        lax.fori_loop(chunks_done, n_chunks, rest, 0)

    @pl.when(nt >= 2)
    def _():
        o_copy(t0 + nt - 2, lax.rem(nt, 2)).wait()

    @pl.when(nt >= 1)
    def _():
        o_copy(t0 + nt - 1, lax.rem(nt + 1, 2)).wait()

    if gather:
        @pl.when(j == 0)
        def _():
            for d in range(nx):
                @pl.when(nt > d)
                def _():
                    tile = nt - 1 - d
                    xw_copy(t0 + tile, lax.rem(tile, nx)).wait()

    @pl.when(e == N_EXPERTS - 1)
    def _():
        obuf[0] = jnp.zeros(obuf.shape[1:], obuf.dtype)

        def zero_tile(tile, carry):
            cp = o_copy(tile, 0)
            cp.start()
            cp.wait()
            return carry

        lax.fori_loop(t0 + nt, total_tiles, zero_tile, 0)

        if gather:
            @pl.when(j == 0)
            def _():
                xbuf[0] = jnp.zeros(xbuf.shape[1:], xbuf.dtype)

                def zero_x(tile, carry):
                    cp = xw_copy(tile, 0)
                    cp.start()
                    cp.wait()
                    return carry

                lax.fori_loop(t0 + nt, total_tiles, zero_x, 0)


def grouped_matmul(x, ws, tstart, ntiles, tm, tn, ck, out_dtype, vmem_mb, name, gather_from=None):
    n_x = 3
    n = ws[0].shape[-1]
    n_w = len(ws)
    n_col = n // tn
    any_spec = pl.BlockSpec(memory_space=pl.ANY)
    gather = gather_from is not None
    if gather:
        dest1, dest2, src, g = gather_from
        k = src.shape[1]
        prefetch = (tstart, ntiles, dest1, dest2)
        inputs = (src, g.reshape(1, k)) + tuple(ws)
        in_specs = [any_spec, pl.BlockSpec((1, k), lambda s, *_: (0, 0))] + [any_spec] * n_w
        out_specs = [any_spec, any_spec]
        out_shape = [jax.ShapeDtypeStruct((R_PAD, n), out_dtype),
                     jax.ShapeDtypeStruct((R_PAD, k), BF16)]
        extra_scratch = [pltpu.VMEM((3, tm, k), F32), pltpu.SemaphoreType.DMA((3,)),
                         pltpu.SemaphoreType.DMA((n_x,)), pltpu.SMEM((R_PAD,), jnp.int32)]
    else:
        k = x.shape[1]
        prefetch = (tstart, ntiles)
        inputs = (x,) + tuple(ws)
        in_specs = [any_spec] * (1 + n_w)
        out_specs = any_spec
        out_shape = jax.ShapeDtypeStruct((R_PAD, n), out_dtype)
        extra_scratch = []
    return pl.pallas_call(
        functools.partial(_grouped_kernel, n_w=n_w, n_col=n_col, total_tiles=R_PAD // tm,
                          gather=gather),
        grid_spec=pltpu.PrefetchScalarGridSpec(
            num_scalar_prefetch=len(prefetch),
            grid=(N_EXPERTS * n_col,),
            in_specs=in_specs,
            out_specs=out_specs,
            scratch_shapes=[pltpu.VMEM((2, n_w, k, tn), BF16), pltpu.VMEM((2, n_w, ck, tn), F32),
                            pltpu.VMEM((n_x, tm, k), BF16), pltpu.VMEM((2, tm, tn), out_dtype),
                            pltpu.SemaphoreType.DMA((n_x,)), pltpu.SemaphoreType.DMA((2,)),
                            pltpu.SemaphoreType.DMA((2, n_w))] + extra_scratch),
        out_shape=out_shape,
        compiler_params=_params(("arbitrary",), vmem_mb),
        name=name,
    )(*prefetch, *inputs)


COMBINE_ROWS = 128
COMBINE_PROMPT_STEPS = M_PROMPT // COMBINE_ROWS


def _combine_kernel(p1_ref, p2_ref, x_ref, route_ref, g_ref, y_hbm, op_ref, os_ref, b1, b2, sem):
    i = pl.program_id(0)
    tm = b1.shape[1]
    slot = lax.rem(i, 2)

    def issue(tile, dst_slot):
        base = tile * tm

        def body(r, carry):
            pltpu.make_async_copy(y_hbm.at[pl.ds(p1_ref[base + r], 1)], b1.at[dst_slot, pl.ds(r, 1)],
                                  sem.at[0, dst_slot]).start(priority=0)
            pltpu.make_async_copy(y_hbm.at[pl.ds(p2_ref[base + r], 1)], b2.at[dst_slot, pl.ds(r, 1)],
                                  sem.at[1, dst_slot]).start(priority=1)
            return carry

        lax.fori_loop(0, tm, body, 0, unroll=8)

    @pl.when(i == 0)
    def _():
        issue(0, 0)

    @pl.when(i + 1 < pl.num_programs(0))
    def _():
        issue(i + 1, 1 - slot)

    pltpu.make_async_copy(y_hbm.at[pl.ds(0, tm)], b1.at[slot], sem.at[0, slot]).wait()
    pltpu.make_async_copy(y_hbm.at[pl.ds(0, tm)], b2.at[slot], sem.at[1, slot]).wait()
    route = route_ref[...]
    w1 = route[:, 0:1]
    w2 = route[:, 1:2]
    x = x_ref[...] + (w1 * b1[slot] + w2 * b2[slot])
    out = _rmsnorm(x, g_ref[...])

    @pl.when(i < COMBINE_PROMPT_STEPS)
    def _():
        op_ref[...] = out

    @pl.when(i == COMBINE_PROMPT_STEPS)
    def _():
        os_ref[...] = out


def moe_combine(pos1, pos2, x, route, g_final, y):
    tm = COMBINE_ROWS
    last_prompt = COMBINE_PROMPT_STEPS - 1
    return pl.pallas_call(
        _combine_kernel,
        grid_spec=pltpu.PrefetchScalarGridSpec(
            num_scalar_prefetch=2,
            grid=(M_TOK // tm,),
            in_specs=[pl.BlockSpec((tm, D), lambda i, p1, p2: (i, 0)),
                      pl.BlockSpec((tm, LANES), lambda i, p1, p2: (i, 0)),
                      pl.BlockSpec((1, D), lambda i, p1, p2: (0, 0)),
                      pl.BlockSpec(memory_space=pl.ANY)],
            out_specs=[pl.BlockSpec((tm, D), lambda i, p1, p2: (jnp.minimum(i, last_prompt), 0)),
                       pl.BlockSpec((M_SAMPLE, D), lambda i, p1, p2: (0, 0))],
            scratch_shapes=[pltpu.VMEM((2, tm, D), F32), pltpu.VMEM((2, tm, D), F32),
                            pltpu.SemaphoreType.DMA((2, 2))]),
        out_shape=[jax.ShapeDtypeStruct((M_PROMPT, D), F32),
                   jax.ShapeDtypeStruct((M_SAMPLE, D), F32)],
        compiler_params=_params(("arbitrary",), 40),
        name="moe_combine",
    )(pos1, pos2, x, route, g_final.reshape(1, D), y)


def _group_tables(idx):
    e_flat = jnp.concatenate([idx[:, 0], idx[:, 1]])
    onehot = (e_flat[:, None] == jnp.arange(N_EXPERTS, dtype=jnp.int32)[None, :]).astype(jnp.int32)
    csum = jnp.cumsum(onehot, axis=0)
    counts = csum[-1]
    rank = jnp.sum(csum * onehot, axis=1) - 1
    ntiles = (counts + GROUP_ROWS - 1) // GROUP_ROWS
    tile_end = jnp.cumsum(ntiles)
    tstart = tile_end - ntiles
    dest = jnp.sum(onehot * (tstart * GROUP_ROWS)[None, :], axis=1) + rank
    return dest[:M_TOK], dest[M_TOK:], tstart, ntiles


def kernel(x_prompt, x_sample, state_conv, state_gla, norm_mix, norm_ffn, norm_final,
           conv_w_in, conv_w, conv_w_out, gla_w_in, gla_w_gate, gla_b_gate, gla_norm, gla_w_out,
           ffn_w_gate, ffn_w_up, ffn_w_down, moe_w_router, moe_w_gate, moe_w_up, moe_w_down):
    x0, h = stack_norm(x_prompt.reshape(M_PROMPT, D), x_sample.reshape(M_SAMPLE, D), norm_mix[0])
    bz, conv_prompt_state, s0, s1 = conv_mixer(
        h, conv_w_in, conv_w[0], state_conv[0].reshape(M_SAMPLE, 2 * D))
    conv_sample_state = jnp.stack([s0, s1], axis=1).reshape(1, M_SAMPLE, 2, D)
    x1, h = linear_res_norm(bz, conv_w_out, 0, x0, norm_ffn[0], name="conv_out")

    a = swiglu_up(h, ffn_w_gate, ffn_w_up, 0)
    x2 = linear(a, ffn_w_down, 0, D, tn=512, tm=640, res=x1, name="ffn_down")

    h = rmsnorm_bf16(x2, norm_mix[1])
    p = linear(h, jnp.swapaxes(gla_w_in, 1, 2), 0, 2 * HK + 2 * HV, tn=1024, tm=TM_DENSE,
               name="gla_in", w_is_nk=True)
    w_gate_pad = jnp.pad(gla_w_gate[0], ((0, LANES - GATE_RANK), (0, 0)))
    w_a_pad = jnp.pad(gla_w_in[0, :, 2 * HK + 2 * HV:], ((0, 0), (0, LANES - GATE_RANK)))
    g = gla_gate(h, w_a_pad, w_gate_pad, gla_b_gate[0].reshape(1, HK))
    gn = gla_norm[0].reshape(1, DV)
    gated, gla_prompt_state = gla_prompt(p, g, gn)
    gated, gla_sample_state = gla_sample(gated, p, g, gn, state_gla[0])
    x3 = linear(gated, gla_w_out, 0, D, tn=1024, tm=TM_DENSE, res=x2, name="gla_out")

    w_router_pad = jnp.pad(moe_w_router[0], ((0, 0), (0, LANES - N_EXPERTS)))
    route, idx = router(x3, norm_ffn[1], w_router_pad)
    pos1, pos2, tstart, ntiles = _group_tables(idx)
    act, _ = grouped_matmul(None, (moe_w_gate, moe_w_up), tstart, ntiles, tm=GROUP_ROWS, tn=1792,
                            ck=256, out_dtype=BF16, vmem_mb=56, name="moe_up",
                            gather_from=(pos1, pos2, x3, norm_ffn[1]))
    y = grouped_matmul(act, (moe_w_down,), tstart, ntiles, tm=GROUP_ROWS, tn=1024, ck=896,
                       out_dtype=F32, vmem_mb=56, name="moe_down")
    y_prompt, y_sample = moe_combine(pos1, pos2, x3, route, norm_final, y)

    y_prompt = y_prompt.reshape(N_PROMPT_SEQ, SEQ, D)
    y_sample = y_sample.reshape(M_SAMPLE, 1, D)
    return (y_prompt, y_sample,
            conv_prompt_state.reshape(1, N_PROMPT_SEQ, 2, D), conv_sample_state,
            gla_prompt_state.reshape(1, N_PROMPT_SEQ, HEADS, DK, DV),
            gla_sample_state.reshape(1, M_SAMPLE, HEADS, DK, DV))
```

```python
import functools

import jax
import jax.numpy as jnp
from jax import lax
from jax.experimental import pallas as pl
from jax.experimental.pallas import tpu as pltpu

F32 = jnp.float32
BF16 = jnp.bfloat16

D = 2048
N_PROMPT_SEQ = 4
SEQ = 2048
M_PROMPT = N_PROMPT_SEQ * SEQ
M_SAMPLE = 128
M_TOK = M_PROMPT + M_SAMPLE
HEADS = 4
DK = 256
DV = 512
HK = HEADS * DK
HV = HEADS * DV
GATE_RANK = 16
GATE_TAU = 16.0
CHUNK = 64
N_CHUNKS = SEQ // CHUNK
D_FF = 5632
N_EXPERTS = 8
D_FF_EXPERT = 7168
EPS = 1e-6
LANES = 128

TM_DENSE = 1040
GROUP_ROWS = 256
N_ASSIGN = 2 * M_TOK
N_GROUP_TILES = N_ASSIGN // GROUP_ROWS + N_EXPERTS
R_PAD = N_GROUP_TILES * GROUP_ROWS


def _params(sem, vmem_mb):
    return pltpu.CompilerParams(dimension_semantics=sem,
                                vmem_limit_bytes=vmem_mb * 1024 * 1024)


def _rmsnorm(x, g):
    return x * lax.rsqrt(jnp.mean(x * x, axis=-1, keepdims=True) + EPS) * g


def _silu(x):
    return x * jax.nn.sigmoid(x)


def _norm_kernel(x_ref, g_ref, o_ref):
    o_ref[...] = _rmsnorm(x_ref[...], g_ref[...]).astype(o_ref.dtype)


def rmsnorm_bf16(x, g, tr=832):
    m = x.shape[0]
    return pl.pallas_call(
        _norm_kernel,
        grid=(m // tr,),
        in_specs=[pl.BlockSpec((tr, D), lambda i: (i, 0)),
                  pl.BlockSpec((1, D), lambda i: (0, 0))],
        out_specs=pl.BlockSpec((tr, D), lambda i: (i, 0)),
        out_shape=jax.ShapeDtypeStruct((m, D), BF16),
        compiler_params=_params(("arbitrary",), 40),
        name="rmsnorm",
    )(x, g.reshape(1, D))


STACK_ROWS = 512
STACK_PROMPT_STEPS = M_PROMPT // STACK_ROWS


def _stack_norm_kernel(xp_ref, xs_ref, g_ref, x_ref, h_ref):
    i = pl.program_id(0)
    g = g_ref[...]

    @pl.when(i < STACK_PROMPT_STEPS)
    def _():
        x = xp_ref[...]
        x_ref[...] = x
        h_ref[...] = _rmsnorm(x, g).astype(h_ref.dtype)

    @pl.when(i == STACK_PROMPT_STEPS)
    def _():
        x = xs_ref[...]
        x_ref[0:M_SAMPLE, :] = x
        h_ref[0:M_SAMPLE, :] = _rmsnorm(x, g).astype(h_ref.dtype)


def stack_norm(xp, xs, g):
    last_prompt = STACK_PROMPT_STEPS - 1
    return pl.pallas_call(
        _stack_norm_kernel,
        grid=(STACK_PROMPT_STEPS + 1,),
        in_specs=[pl.BlockSpec((STACK_ROWS, D), lambda i: (jnp.minimum(i, last_prompt), 0)),
                  pl.BlockSpec((M_SAMPLE, D), lambda i: (0, 0)),
                  pl.BlockSpec((1, D), lambda i: (0, 0))],
        out_specs=[pl.BlockSpec((STACK_ROWS, D), lambda i: (i, 0)),
                   pl.BlockSpec((STACK_ROWS, D), lambda i: (i, 0))],
        out_shape=[jax.ShapeDtypeStruct((M_TOK, D), F32),
                   jax.ShapeDtypeStruct((M_TOK, D), BF16)],
        compiler_params=_params(("arbitrary",), 40),
        name="stack_norm",
    )(xp, xs, g.reshape(1, D))


def _linear_kernel(*refs, has_res, w_is_nk):
    if has_res:
        x_ref, w_ref, r_ref, o_ref, wb_ref = refs
    else:
        x_ref, w_ref, o_ref, wb_ref = refs

    @pl.when(pl.program_id(1) == 0)
    def _():
        wb_ref[...] = w_ref[...].astype(BF16)

    contract_w = 1 if w_is_nk else 0
    acc = lax.dot_general(x_ref[...], wb_ref[...], (((1,), (contract_w,)), ((), ())),
                          preferred_element_type=F32)
    if has_res:
        acc = acc + r_ref[...]
    o_ref[...] = acc.astype(o_ref.dtype)


def linear(x, w, layer, n_out, tn, tm, res=None, out_dtype=F32, vmem_mb=56, name="linear",
           w_is_nk=False):
    m, k = x.shape
    wblock = (None, tn, k) if w_is_nk else (None, k, tn)
    wmap = (lambda j, i: (layer, j, 0)) if w_is_nk else (lambda j, i: (layer, 0, j))
    in_specs = [pl.BlockSpec((tm, k), lambda j, i: (i, 0)),
                pl.BlockSpec(wblock, wmap)]
    args = [x, w]
    if res is not None:
        in_specs.append(pl.BlockSpec((tm, tn), lambda j, i: (i, j)))
        args.append(res)
    return pl.pallas_call(
        functools.partial(_linear_kernel, has_res=res is not None, w_is_nk=w_is_nk),
        grid=(n_out // tn, m // tm),
        in_specs=in_specs,
        out_specs=pl.BlockSpec((tm, tn), lambda j, i: (i, j)),
        out_shape=jax.ShapeDtypeStruct((m, n_out), out_dtype),
        scratch_shapes=[pltpu.VMEM(wblock[1:], BF16)],
        compiler_params=_params(("arbitrary", "arbitrary"), vmem_mb),
        name=name,
    )(*args)


def _linear_res_norm_kernel(x_ref, w_ref, r_ref, g_ref, o_ref, h_ref, wb_ref):
    @pl.when(pl.program_id(0) == 0)
    def _():
        wb_ref[...] = w_ref[...].astype(BF16)

    y = jnp.dot(x_ref[...], wb_ref[...], preferred_element_type=F32) + r_ref[...]
    o_ref[...] = y
    h_ref[...] = _rmsnorm(y, g_ref[...]).astype(h_ref.dtype)


def linear_res_norm(x, w, layer, res, g, tm=416, name="linear_res_norm"):
    m, k = x.shape
    return pl.pallas_call(
        _linear_res_norm_kernel,
        grid=(m // tm,),
        in_specs=[pl.BlockSpec((tm, k), lambda i: (i, 0)),
                  pl.BlockSpec((None, k, D), lambda i: (layer, 0, 0), pipeline_mode=pl.Buffered(1)),
                  pl.BlockSpec((tm, D), lambda i: (i, 0)),
                  pl.BlockSpec((1, D), lambda i: (0, 0))],
        out_specs=[pl.BlockSpec((tm, D), lambda i: (i, 0)),
                   pl.BlockSpec((tm, D), lambda i: (i, 0))],
        out_shape=[jax.ShapeDtypeStruct((m, D), F32),
                   jax.ShapeDtypeStruct((m, D), BF16)],
        scratch_shapes=[pltpu.VMEM((k, D), BF16)],
        compiler_params=_params(("arbitrary",), 56),
        name=name,
    )(x, w, res, g.reshape(1, D))


def _swiglu_kernel(x_ref, wg_ref, wu_ref, o_ref, wgb_ref, wub_ref):
    @pl.when(pl.program_id(1) == 0)
    def _():
        wgb_ref[...] = wg_ref[...].astype(BF16)
        wub_ref[...] = wu_ref[...].astype(BF16)

    x = x_ref[...]
    g = jnp.dot(x, wgb_ref[...], preferred_element_type=F32)
    u = jnp.dot(x, wub_ref[...], preferred_element_type=F32)
    o_ref[...] = (_silu(g) * u).astype(o_ref.dtype)


def swiglu_up(x, wg, wu, layer, tn=512, tm=TM_DENSE):
    m, k = x.shape
    f = wg.shape[-1]
    wspec = pl.BlockSpec((None, k, tn), lambda j, i: (layer, 0, j))
    return pl.pallas_call(
        _swiglu_kernel,
        grid=(f // tn, m // tm),
        in_specs=[pl.BlockSpec((tm, k), lambda j, i: (i, 0)), wspec, wspec],
        out_specs=pl.BlockSpec((tm, tn), lambda j, i: (i, j)),
        out_shape=jax.ShapeDtypeStruct((m, f), BF16),
        scratch_shapes=[pltpu.VMEM((k, tn), BF16), pltpu.VMEM((k, tn), BF16)],
        compiler_params=_params(("arbitrary", "arbitrary"), 56),
        name="ffn_up",
    )(x, wg, wu)


CONV_TM = 512
CONV_TC = 512
CONV_TILES_PER_SEQ = SEQ // CONV_TM
CONV_PROMPT_TILES = M_PROMPT // CONV_TM


def _conv_kernel(x_ref, xs_ref, wh_ref, wb_ref, wc_ref, cw_ref, b0_ref, b1_ref,
                 bz_ref, stp_ref, s0_ref, s1_ref, wbf_ref, carry_ref):
    i = pl.program_id(1)

    @pl.when(i == 0)
    def _():
        wbf_ref[0] = wh_ref[...].astype(BF16)
        wbf_ref[1] = wb_ref[...].astype(BF16)
        wbf_ref[2] = wc_ref[...].astype(BF16)

    def project(x):
        hh = jnp.dot(x, wbf_ref[0], preferred_element_type=F32)
        bb = jnp.dot(x, wbf_ref[1], preferred_element_type=F32)
        cc = jnp.dot(x, wbf_ref[2], preferred_element_type=F32)
        return bb, cc * hh

    w = cw_ref[...]

    @pl.when(i < CONV_PROMPT_TILES)
    def _():
        @pl.when(i % CONV_TILES_PER_SEQ == 0)
        def _():
            carry_ref[...] = jnp.zeros_like(carry_ref)

        bb, u = project(x_ref[...])
        c2 = carry_ref[0:1, :]
        c1 = carry_ref[1:2, :]
        row = lax.broadcasted_iota(jnp.int32, u.shape, 0)
        u1 = jnp.where(row == 0, c1, pltpu.roll(u, 1, 0))
        u2 = jnp.where(row == 0, c2, jnp.where(row == 1, c1, pltpu.roll(u, 2, 0)))
        z = w[0:1, :] * u2
        z = z + w[1:2, :] * u1
        z = z + w[2:3, :] * u
        bz_ref[...] = (bb * z).astype(bz_ref.dtype)
        tail = u[CONV_TM - 2:CONV_TM, :]
        carry_ref[0:2, :] = tail
        stp_ref[...] = tail

    @pl.when(i == CONV_PROMPT_TILES)
    def _():
        bb, u = project(xs_ref[...])
        b1 = b1_ref[...]
        z = w[0:1, :] * b0_ref[...]
        z = z + w[1:2, :] * b1
        z = z + w[2:3, :] * u
        bz_ref[0:M_SAMPLE, :] = (bb * z).astype(bz_ref.dtype)
        s0_ref[...] = b1
        s1_ref[...] = u


def conv_mixer(h, w_in, conv_w, buf):
    tm, tc = CONV_TM, CONV_TC
    nj = D // tc
    last_tile = CONV_PROMPT_TILES - 1
    last_seq = N_PROMPT_SEQ - 1
    wspec = lambda part: pl.BlockSpec((None, D, tc), lambda j, i: (0, 0, part * nj + j))
    return pl.pallas_call(
        _conv_kernel,
        grid=(nj, CONV_PROMPT_TILES + 1),
        in_specs=[pl.BlockSpec((tm, D), lambda j, i: (jnp.minimum(i, last_tile), 0)),
                  pl.BlockSpec((M_SAMPLE, D), lambda j, i: (M_PROMPT // M_SAMPLE, 0)),
                  wspec(0), wspec(1), wspec(2),
                  pl.BlockSpec((3, tc), lambda j, i: (0, j)),
                  pl.BlockSpec((M_SAMPLE, tc), lambda j, i: (0, j)),
                  pl.BlockSpec((M_SAMPLE, tc), lambda j, i: (0, nj + j))],
        out_specs=[pl.BlockSpec((tm, tc), lambda j, i: (i, j)),
                   pl.BlockSpec((None, 2, tc),
                                lambda j, i: (jnp.minimum(i // CONV_TILES_PER_SEQ, last_seq), 0, j)),
                   pl.BlockSpec((M_SAMPLE, tc), lambda j, i: (0, j)),
                   pl.BlockSpec((M_SAMPLE, tc), lambda j, i: (0, j))],
        out_shape=[jax.ShapeDtypeStruct((M_TOK, D), BF16),
                   jax.ShapeDtypeStruct((N_PROMPT_SEQ, 2, D), F32),
                   jax.ShapeDtypeStruct((M_SAMPLE, D), F32),
                   jax.ShapeDtypeStruct((M_SAMPLE, D), F32)],
        scratch_shapes=[pltpu.VMEM((3, D, tc), BF16), pltpu.VMEM((8, tc), F32)],
        compiler_params=_params(("arbitrary", "arbitrary"), 56),
        name="conv_mixer",
    )(h, h, w_in, w_in, w_in, conv_w, buf, buf)


def _log_sigmoid(x):
    return jnp.minimum(x, 0.0) - jnp.log1p(jnp.exp(-jnp.abs(x)))


def _gla_gate_kernel(h_ref, wa_ref, wg_ref, bg_ref, g_ref):
    a = jnp.dot(h_ref[...], wa_ref[...].astype(BF16), preferred_element_type=F32)
    z = jnp.dot(a.astype(BF16), wg_ref[...].astype(BF16), preferred_element_type=F32) + bg_ref[...]
    g_ref[...] = _log_sigmoid(z) * (1.0 / GATE_TAU)


def gla_gate(h, w_a_pad, w_gate_pad, b_gate, tm=TM_DENSE):
    return pl.pallas_call(
        _gla_gate_kernel,
        grid=(M_TOK // tm,),
        in_specs=[pl.BlockSpec((tm, D), lambda i: (i, 0)),
                  pl.BlockSpec((D, LANES), lambda i: (0, 0)),
                  pl.BlockSpec((LANES, HK), lambda i: (0, 0)),
                  pl.BlockSpec((1, HK), lambda i: (0, 0))],
        out_specs=pl.BlockSpec((tm, HK), lambda i: (i, 0)),
        out_shape=jax.ShapeDtypeStruct((M_TOK, HK), F32),
        compiler_params=_params(("arbitrary",), 40),
        name="gla_gate",
    )(h, w_a_pad, w_gate_pad, b_gate)


def _row_to_cols(row):
    return jnp.transpose(jnp.broadcast_to(row, (LANES, row.shape[1])))


def _split3_bf16(x):
    x1 = x.astype(BF16)
    r1 = x - x1.astype(F32)
    x2 = r1.astype(BF16)
    x3 = (r1 - x2.astype(F32)).astype(BF16)
    return x1, x2, x3


def _gla_prompt_kernel(o_init, q_ref, k_ref, v_ref, r_ref, g_ref, gn_ref, o_ref, sout_ref, s_ref):
    del o_init
    c = pl.program_id(1)

    @pl.when(c == 0)
    def _():
        s_ref[...] = jnp.zeros_like(s_ref)

    row = lax.broadcasted_iota(jnp.int32, (CHUNK, CHUNK), 0)
    col = lax.broadcasted_iota(jnp.int32, (CHUNK, CHUNK), 1)
    tri = row >= col
    trib = tri.astype(BF16)

    g1, g2, g3 = _split3_bf16(g_ref[...])
    b = (jnp.dot(trib, g1, preferred_element_type=F32)
         + jnp.dot(trib, g2, preferred_element_type=F32)
         + jnp.dot(trib, g3, preferred_element_type=F32))
    b_last = b[CHUNK - 1:CHUNK, :]
    q = q_ref[...] * (DK ** -0.5)
    k = k_ref[...]
    q_dec = (q * jnp.exp(b)).astype(BF16)
    k_inv = (k * jnp.exp(-b)).astype(BF16)
    k_end = (k * jnp.exp(b_last - b)).astype(BF16)
    decay = jnp.exp(b_last)
    gn = gn_ref[...]

    for h in range(HEADS):
        ks = slice(h * DK, (h + 1) * DK)
        vs = slice(h * DV, (h + 1) * DV)
        vb = v_ref[:, vs].astype(BF16)
        s_old = s_ref[h]
        scores = lax.dot_general(q_dec[:, ks], k_inv[:, ks], (((1,), (1,)), ((), ())),
                                 preferred_element_type=F32)
        scores = jnp.where(tri, scores, 0.0).astype(BF16)
        o = (jnp.dot(scores, vb, preferred_element_type=F32)
             + jnp.dot(q_dec[:, ks], s_old.astype(BF16), preferred_element_type=F32))
        kv = lax.dot_general(k_end[:, ks], vb, (((0,), (0,)), ((), ())),
                             preferred_element_type=F32)
        dcol = _row_to_cols(decay[:, ks])
        s_ref[h] = jnp.concatenate(
            [s_old[:, j * LANES:(j + 1) * LANES] * dcol for j in range(DV // LANES)], axis=1) + kv
        on = _rmsnorm(o, gn)
        o_ref[:, vs] = (_silu(r_ref[:, vs]) * on).astype(o_ref.dtype)

    @pl.when(c == N_CHUNKS - 1)
    def _():
        sout_ref[...] = s_ref[...]


def gla_prompt(p, g, g_norm):
    rows = lambda n, c: n * N_CHUNKS + c
    return pl.pallas_call(
        _gla_prompt_kernel,
        grid=(N_PROMPT_SEQ, N_CHUNKS),
        in_specs=[pl.BlockSpec(memory_space=pl.ANY),
                  pl.BlockSpec((CHUNK, HK), lambda n, c: (rows(n, c), 0)),
                  pl.BlockSpec((CHUNK, HK), lambda n, c: (rows(n, c), 1)),
                  pl.BlockSpec((CHUNK, HV), lambda n, c: (rows(n, c), 1)),
                  pl.BlockSpec((CHUNK, HV), lambda n, c: (rows(n, c), 2)),
                  pl.BlockSpec((CHUNK, HK), lambda n, c: (rows(n, c), 0)),
                  pl.BlockSpec((1, DV), lambda n, c: (0, 0))],
        out_specs=[pl.BlockSpec((CHUNK, HV), lambda n, c: (rows(n, c), 0)),
                   pl.BlockSpec((None, HEADS, DK, DV), lambda n, c: (n, 0, 0, 0))],
        out_shape=[jax.ShapeDtypeStruct((M_TOK, HV), BF16),
                   jax.ShapeDtypeStruct((N_PROMPT_SEQ, HEADS, DK, DV), F32)],
        scratch_shapes=[pltpu.VMEM((HEADS, DK, DV), F32)],
        input_output_aliases={0: 0},
        compiler_params=_params(("arbitrary", "arbitrary"), 40),
        name="gla_prompt",
    )(jnp.zeros((M_TOK, HV), BF16), p, p, p, p, g, g_norm)


GLA_SAMPLE_SEQS = 2


def _gla_sample_kernel(gated_any, q_ref, k_ref, v_ref, r_ref, g_ref, gn_ref, s_ref,
                       gated_ref, sout_ref, qt_ref, kt_ref, et_ref, o_scr):
    del gated_any
    i = pl.program_id(0)

    @pl.when(i == 0)
    def _():
        q = q_ref[...] * (DK ** -0.5)
        k = k_ref[...]
        e = jnp.exp(g_ref[...])
        for h in range(HEADS):
            ks = slice(h * DK, (h + 1) * DK)
            qt_ref[h] = jnp.transpose(q[:, ks])
            kt_ref[h] = jnp.transpose(k[:, ks])
            et_ref[h] = jnp.transpose(e[:, ks])

    lane = lax.broadcasted_iota(jnp.int32, (DK, M_SAMPLE), 1)
    for s in range(GLA_SAMPLE_SEQS):
        n = i * GLA_SAMPLE_SEQS + s
        pick = lane == n

        def column(t):
            return jnp.sum(jnp.where(pick, t, 0.0), axis=1, keepdims=True)

        for h in range(HEADS):
            vs = slice(h * DV, (h + 1) * DV)
            v_row = v_ref[pl.ds(n, 1), vs]
            s_new = s_ref[s, h] * column(et_ref[h]) + column(kt_ref[h]) * v_row
            sout_ref[s, h] = s_new
            o_scr[pl.ds(n, 1), vs] = jnp.sum(column(qt_ref[h]) * s_new, axis=0, keepdims=True)

    @pl.when(i == pl.num_programs(0) - 1)
    def _():
        gn = gn_ref[...]
        for h in range(HEADS):
            vs = slice(h * DV, (h + 1) * DV)
            on = _rmsnorm(o_scr[:, vs], gn)
            gated_ref[:, vs] = (_silu(r_ref[:, vs]) * on).astype(gated_ref.dtype)


def gla_sample(gated, p, g, g_norm, state):
    rb = M_PROMPT // M_SAMPLE
    bs = GLA_SAMPLE_SEQS
    return pl.pallas_call(
        _gla_sample_kernel,
        grid=(M_SAMPLE // bs,),
        in_specs=[pl.BlockSpec(memory_space=pl.ANY),
                  pl.BlockSpec((M_SAMPLE, HK), lambda i: (rb, 0)),
                  pl.BlockSpec((M_SAMPLE, HK), lambda i: (rb, 1)),
                  pl.BlockSpec((M_SAMPLE, HV), lambda i: (rb, 1)),
                  pl.BlockSpec((M_SAMPLE, HV), lambda i: (rb, 2)),
                  pl.BlockSpec((M_SAMPLE, HK), lambda i: (rb, 0)),
                  pl.BlockSpec((1, DV), lambda i: (0, 0)),
                  pl.BlockSpec((bs, HEADS, DK, DV), lambda i: (i, 0, 0, 0))],
        out_specs=[pl.BlockSpec((M_SAMPLE, HV), lambda i: (rb, 0)),
                   pl.BlockSpec((bs, HEADS, DK, DV), lambda i: (i, 0, 0, 0))],
        out_shape=[jax.ShapeDtypeStruct((M_TOK, HV), BF16),
                   jax.ShapeDtypeStruct((M_SAMPLE, HEADS, DK, DV), F32)],
        scratch_shapes=[pltpu.VMEM((HEADS, DK, M_SAMPLE), F32),
                        pltpu.VMEM((HEADS, DK, M_SAMPLE), F32),
                        pltpu.VMEM((HEADS, DK, M_SAMPLE), F32),
                        pltpu.VMEM((M_SAMPLE, HV), F32)],
        input_output_aliases={0: 0},
        compiler_params=_params(("arbitrary",), 40),
        name="gla_sample",
    )(gated, p, p, p, p, g, g_norm, state)


def _router_kernel(x_ref, g_ref, wr_ref, route_ref, idx_ref):
    h = _rmsnorm(x_ref[...], g_ref[...])
    lane = lax.broadcasted_iota(jnp.int32, route_ref.shape, 1)
    lane_f = lane.astype(F32)
    neg = jnp.float32(-jnp.inf)
    wr = wr_ref[...]
    logits = jnp.full(route_ref.shape, neg, F32)
    for e in range(N_EXPERTS):
        logit_e = jnp.sum(h * wr[e:e + 1, :], axis=1, keepdims=True)
        logits = jnp.where(lane == e, logit_e, logits)
    m1 = jnp.max(logits, axis=1, keepdims=True)
    i1 = jnp.min(jnp.where(logits == m1, lane_f, float(LANES)), axis=1, keepdims=True)
    rest = jnp.where(lane_f == i1, neg, logits)
    m2 = jnp.max(rest, axis=1, keepdims=True)
    i2 = jnp.min(jnp.where(rest == m2, lane_f, float(LANES)), axis=1, keepdims=True)
    e2 = jnp.exp(m2 - m1)
    den = 1.0 + e2
    w1 = 1.0 / den
    w2 = e2 / den
    route_ref[...] = jnp.where(lane == 0, w1, jnp.where(lane == 1, w2, 0.0))
    idx_ref[...] = jnp.where(lane == 0, i1, jnp.where(lane == 1, i2, 0.0)).astype(jnp.int32)


def router(x, g, w_router_t, tm=416):
    return pl.pallas_call(
        _router_kernel,
        grid=(M_TOK // tm,),
        in_specs=[pl.BlockSpec((tm, D), lambda i: (i, 0)),
                  pl.BlockSpec((1, D), lambda i: (0, 0)),
                  pl.BlockSpec((N_EXPERTS, D), lambda i: (0, 0))],
        out_specs=[pl.BlockSpec((tm, LANES), lambda i: (i, 0)),
                   pl.BlockSpec((tm, LANES), lambda i: (i, 0))],
        out_shape=[jax.ShapeDtypeStruct((M_TOK, LANES), F32),
                   jax.ShapeDtypeStruct((M_TOK, LANES), jnp.int32)],
        compiler_params=_params(("arbitrary",), 40),
        name="router",
    )(x, g.reshape(1, D), w_router_t)


def _grouped_kernel(*refs, n_w, n_col, total_tiles, gather):
    if gather:
        tstart, ntiles, dest1, dest2, src_hbm, gn_ref = refs[:6]
        w_hbm = refs[6:6 + n_w]
        o_hbm, x_hbm = refs[6 + n_w:8 + n_w]
        wbf, stage, xbuf, obuf, xsem, osem, wsem, gbuf, gsem, xwsem, tok = refs[8 + n_w:]
    else:
        tstart, ntiles, x_hbm = refs[:3]
        w_hbm = refs[3:3 + n_w]
        o_hbm = refs[3 + n_w]
        wbf, stage, xbuf, obuf, xsem, osem, wsem = refs[4 + n_w:]
    s = pl.program_id(0)
    n_items = pl.num_programs(0)
    e = s // n_col
    j = lax.rem(s, n_col)
    p = lax.rem(s, 2)
    tm = xbuf.shape[1]
    tn = obuf.shape[2]
    ck = stage.shape[2]
    n_chunks = wbf.shape[2] // ck
    nt = ntiles[e]
    t0 = tstart[e]
    col = pl.multiple_of(j * tn, tn)
    has_next = s + 1 < n_items

    def w_copies(item, c, q):
        row = pl.multiple_of(c * ck, ck)
        wcol = pl.multiple_of(lax.rem(item, n_col) * tn, tn)
        return [pltpu.make_async_copy(w.at[0, item // n_col, pl.ds(row, ck), pl.ds(wcol, tn)],
                                      stage.at[q, i], wsem.at[q, i])
                for i, w in enumerate(w_hbm)]

    def w_start(item, c, q):
        for cp in w_copies(item, c, q):
            cp.start(priority=1)

    def w_chunk(item, c, dst):
        q = lax.rem(c, 2)

        @pl.when(c + 1 < n_chunks)
        def _():
            w_start(item, c + 1, 1 - q)

        row = pl.multiple_of(c * ck, ck)
        for i, cp in enumerate(w_copies(item, c, q)):
            cp.wait()
            wbf[dst, i, pl.ds(row, ck), :] = stage[q, i].astype(BF16)

    @pl.when(s == 0)
    def _():
        w_start(0, 0, 0)

        if gather:
            def clear(r, carry):
                tok[r] = 0
                return carry

            lax.fori_loop(0, tok.shape[0], clear, 0, unroll=8)

            def invert(t, carry):
                tok[dest1[t]] = t
                tok[dest2[t]] = t
                return carry

            lax.fori_loop(0, dest1.shape[0], invert, 0, unroll=8)

        def first(c, carry):
            w_chunk(0, c, 0)
            return carry

        lax.fori_loop(0, n_chunks, first, 0)

    @pl.when(has_next)
    def _():
        w_start(s + 1, 0, 0)

    def x_copy(tile, slot):
        row = pl.multiple_of(tile * tm, tm)
        return pltpu.make_async_copy(x_hbm.at[pl.ds(row, tm)], xbuf.at[slot], xsem.at[slot])

    def o_copy(tile, slot):
        row = pl.multiple_of(tile * tm, tm)
        return pltpu.make_async_copy(obuf.at[slot], o_hbm.at[pl.ds(row, tm), pl.ds(col, tn)],
                                     osem.at[slot])

    nx = xbuf.shape[0]

    def when_fetching(fn):
        if gather:
            pl.when(j > 0)(fn)
        else:
            fn()

    def g_issue(t, gslot):
        base = (t0 + t) * tm

        def rows(r, carry):
            pltpu.make_async_copy(src_hbm.at[pl.ds(tok[base + r], 1)],
                                  gbuf.at[gslot, pl.ds(r, 1)], gsem.at[gslot]).start()
            return carry

        lax.fori_loop(0, tm, rows, 0, unroll=8)

    def xw_copy(tile, xslot):
        row = pl.multiple_of(tile * tm, tm)
        return pltpu.make_async_copy(xbuf.at[xslot], x_hbm.at[pl.ds(row, tm)], xwsem.at[xslot])

    @when_fetching
    def _():
        for d in range(nx - 1):
            @pl.when(d < nt)
            def _():
                x_copy(t0 + d, d).start()

    if gather:
        ng = gbuf.shape[0]
        for d in range(ng - 1):
            @pl.when((j == 0) & (d < nt))
            def _():
                g_issue(d, d)

    def body(t, chunks_done):
        slot = lax.rem(t, 2)
        xslot = lax.rem(t, nx)

        @when_fetching
        def _():
            x_copy(t0 + t, xslot).wait()
            ahead = t + (nx - 1)

            @pl.when(ahead < nt)
            def _():
                x_copy(t0 + ahead, lax.rem(ahead, nx)).start()

        if gather:
            @pl.when(j == 0)
            def _():
                gslot = lax.rem(t, ng)
                g_ahead = t + (ng - 1)

                @pl.when(g_ahead < nt)
                def _():
                    g_issue(g_ahead, lax.rem(g_ahead, ng))

                pltpu.make_async_copy(src_hbm.at[pl.ds(0, tm)], gbuf.at[gslot], gsem.at[gslot]).wait()

                @pl.when(t >= nx)
                def _():
                    xw_copy(t0 + t - nx, xslot).wait()

                xbuf[xslot] = _rmsnorm(gbuf[gslot], gn_ref[...]).astype(xbuf.dtype)
                xw_copy(t0 + t, xslot).start()

        @pl.when(t >= 2)
        def _():
            o_copy(t0 + t - 2, slot).wait()

        x = xbuf[xslot]
        if n_w == 2:
            g = jnp.dot(x, wbf[p, 0], preferred_element_type=F32)
            u = jnp.dot(x, wbf[p, 1], preferred_element_type=F32)
            obuf[slot] = (_silu(g) * u).astype(obuf.dtype)
        else:
            obuf[slot] = jnp.dot(x, wbf[p, 0], preferred_element_type=F32).astype(obuf.dtype)
        o_copy(t0 + t, slot).start()

        stream = has_next & (chunks_done < n_chunks)

        @pl.when(stream)
        def _():
            w_chunk(s + 1, chunks_done, 1 - p)

        return chunks_done + stream.astype(jnp.int32)

    chunks_done = lax.fori_loop(0, nt, body, jnp.int32(0))

    @pl.when(has_next)
    def _():
        def rest(c, carry):
            w_chunk(s + 1, c, 1 - p)
            return carry

        lax.fori_loop(chunks_done, n_chunks, rest, 0)

    @pl.when(nt >= 2)
    def _():
        o_copy(t0 + nt - 2, lax.rem(nt, 2)).wait()

    @pl.when(nt >= 1)
    def _():
        o_copy(t0 + nt - 1, lax.rem(nt + 1, 2)).wait()

    if gather:
        @pl.when(j == 0)
        def _():
            for d in range(nx):
                @pl.when(nt > d)
                def _():
                    tile = nt - 1 - d
                    xw_copy(t0 + tile, lax.rem(tile, nx)).wait()

    @pl.when(e == N_EXPERTS - 1)
    def _():
        obuf[0] = jnp.zeros(obuf.shape[1:], obuf.dtype)

        def zero_tile(tile, carry):
            cp = o_copy(tile, 0)
            cp.start()
            cp.wait()
            return carry

        lax.fori_loop(t0 + nt, total_tiles, zero_tile, 0)

        if gather:
            @pl.when(j == 0)
            def _():
                xbuf[0] = jnp.zeros(xbuf.shape[1:], xbuf.dtype)

                def zero_x(tile, carry):
                    cp = xw_copy(tile, 0)
                    cp.start()
                    cp.wait()
                    return carry

                lax.fori_loop(t0 + nt, total_tiles, zero_x, 0)


def grouped_matmul(x, ws, tstart, ntiles, tm, tn, ck, out_dtype, vmem_mb, name, gather_from=None):
    n_x = 3
    n = ws[0].shape[-1]
    n_w = len(ws)
    n_col = n // tn
    any_spec = pl.BlockSpec(memory_space=pl.ANY)
    gather = gather_from is not None
    if gather:
        dest1, dest2, src, g = gather_from
        k = src.shape[1]
        prefetch = (tstart, ntiles, dest1, dest2)
        inputs = (src, g.reshape(1, k)) + tuple(ws)
        in_specs = [any_spec, pl.BlockSpec((1, k), lambda s, *_: (0, 0))] + [any_spec] * n_w
        out_specs = [any_spec, any_spec]
        out_shape = [jax.ShapeDtypeStruct((R_PAD, n), out_dtype),
                     jax.ShapeDtypeStruct((R_PAD, k), BF16)]
        extra_scratch = [pltpu.VMEM((3, tm, k), F32), pltpu.SemaphoreType.DMA((3,)),
                         pltpu.SemaphoreType.DMA((n_x,)), pltpu.SMEM((R_PAD,), jnp.int32)]
    else:
        k = x.shape[1]
        prefetch = (tstart, ntiles)
        inputs = (x,) + tuple(ws)
        in_specs = [any_spec] * (1 + n_w)
        out_specs = any_spec
        out_shape = jax.ShapeDtypeStruct((R_PAD, n), out_dtype)
        extra_scratch = []
    return pl.pallas_call(
        functools.partial(_grouped_kernel, n_w=n_w, n_col=n_col, total_tiles=R_PAD // tm,
                          gather=gather),
        grid_spec=pltpu.PrefetchScalarGridSpec(
            num_scalar_prefetch=len(prefetch),
            grid=(N_EXPERTS * n_col,),
            in_specs=in_specs,
            out_specs=out_specs,
            scratch_shapes=[pltpu.VMEM((2, n_w, k, tn), BF16), pltpu.VMEM((2, n_w, ck, tn), F32),
                            pltpu.VMEM((n_x, tm, k), BF16), pltpu.VMEM((2, tm, tn), out_dtype),
                            pltpu.SemaphoreType.DMA((n_x,)), pltpu.SemaphoreType.DMA((2,)),
                            pltpu.SemaphoreType.DMA((2, n_w))] + extra_scratch),
        out_shape=out_shape,
        compiler_params=_params(("arbitrary",), vmem_mb),
        name=name,
    )(*prefetch, *inputs)


COMBINE_ROWS = 128
COMBINE_PROMPT_STEPS = M_PROMPT // COMBINE_ROWS


def _combine_kernel(p1_ref, p2_ref, x_ref, route_ref, g_ref, y_hbm, op_ref, os_ref, b1, b2, sem):
    i = pl.program_id(0)
    tm = b1.shape[1]
    slot = lax.rem(i, 2)

    def issue(tile, dst_slot):
        base = tile * tm

        def body(r, carry):
            pltpu.make_async_copy(y_hbm.at[pl.ds(p1_ref[base + r], 1)], b1.at[dst_slot, pl.ds(r, 1)],
                                  sem.at[0, dst_slot]).start(priority=0)
            pltpu.make_async_copy(y_hbm.at[pl.ds(p2_ref[base + r], 1)], b2.at[dst_slot, pl.ds(r, 1)],
                                  sem.at[1, dst_slot]).start(priority=1)
            return carry

        lax.fori_loop(0, tm, body, 0, unroll=8)

    @pl.when(i == 0)
    def _():
        issue(0, 0)

    @pl.when(i + 1 < pl.num_programs(0))
    def _():
        issue(i + 1, 1 - slot)

    pltpu.make_async_copy(y_hbm.at[pl.ds(0, tm)], b1.at[slot], sem.at[0, slot]).wait()
    pltpu.make_async_copy(y_hbm.at[pl.ds(0, tm)], b2.at[slot], sem.at[1, slot]).wait()
    route = route_ref[...]
    w1 = route[:, 0:1]
    w2 = route[:, 1:2]
    x = x_ref[...] + (w1 * b1[slot] + w2 * b2[slot])
    out = _rmsnorm(x, g_ref[...])

    @pl.when(i < COMBINE_PROMPT_STEPS)
    def _():
        op_ref[...] = out

    @pl.when(i == COMBINE_PROMPT_STEPS)
    def _():
        os_ref[...] = out


def moe_combine(pos1, pos2, x, route, g_final, y):
    tm = COMBINE_ROWS
    last_prompt = COMBINE_PROMPT_STEPS - 1
    return pl.pallas_call(
        _combine_kernel,
        grid_spec=pltpu.PrefetchScalarGridSpec(
            num_scalar_prefetch=2,
            grid=(M_TOK // tm,),
            in_specs=[pl.BlockSpec((tm, D), lambda i, p1, p2: (i, 0)),
                      pl.BlockSpec((tm, LANES), lambda i, p1, p2: (i, 0)),
                      pl.BlockSpec((1, D), lambda i, p1, p2: (0, 0)),
                      pl.BlockSpec(memory_space=pl.ANY)],
            out_specs=[pl.BlockSpec((tm, D), lambda i, p1, p2: (jnp.minimum(i, last_prompt), 0)),
                       pl.BlockSpec((M_SAMPLE, D), lambda i, p1, p2: (0, 0))],
            scratch_shapes=[pltpu.VMEM((2, tm, D), F32), pltpu.VMEM((2, tm, D), F32),
                            pltpu.SemaphoreType.DMA((2, 2))]),
        out_shape=[jax.ShapeDtypeStruct((M_PROMPT, D), F32),
                   jax.ShapeDtypeStruct((M_SAMPLE, D), F32)],
        compiler_params=_params(("arbitrary",), 40),
        name="moe_combine",
    )(pos1, pos2, x, route, g_final.reshape(1, D), y)


def _group_tables(idx):
    e_flat = jnp.concatenate([idx[:, 0], idx[:, 1]])
    onehot = (e_flat[:, None] == jnp.arange(N_EXPERTS, dtype=jnp.int32)[None, :]).astype(jnp.int32)
    csum = jnp.cumsum(onehot, axis=0)
    counts = csum[-1]
    rank = jnp.sum(csum * onehot, axis=1) - 1
    ntiles = (counts + GROUP_ROWS - 1) // GROUP_ROWS
    tile_end = jnp.cumsum(ntiles)
    tstart = tile_end - ntiles
    dest = jnp.sum(onehot * (tstart * GROUP_ROWS)[None, :], axis=1) + rank
    return dest[:M_TOK], dest[M_TOK:], tstart, ntiles


def kernel(x_prompt, x_sample, state_conv, state_gla, norm_mix, norm_ffn, norm_final,
           conv_w_in, conv_w, conv_w_out, gla_w_in, gla_w_gate, gla_b_gate, gla_norm, gla_w_out,
           ffn_w_gate, ffn_w_up, ffn_w_down, moe_w_router, moe_w_gate, moe_w_up, moe_w_down):
    x0, h = stack_norm(x_prompt.reshape(M_PROMPT, D), x_sample.reshape(M_SAMPLE, D), norm_mix[0])
    bz, conv_prompt_state, s0, s1 = conv_mixer(
        h, conv_w_in, conv_w[0], state_conv[0].reshape(M_SAMPLE, 2 * D))
    conv_sample_state = jnp.stack([s0, s1], axis=1).reshape(1, M_SAMPLE, 2, D)
    x1, h = linear_res_norm(bz, conv_w_out, 0, x0, norm_ffn[0], name="conv_out")

    a = swiglu_up(h, ffn_w_gate, ffn_w_up, 0)
    x2 = linear(a, ffn_w_down, 0, D, tn=512, tm=640, res=x1, name="ffn_down")

    h = rmsnorm_bf16(x2, norm_mix[1])
    p = linear(h, jnp.swapaxes(gla_w_in, 1, 2), 0, 2 * HK + 2 * HV, tn=1024, tm=TM_DENSE,
               name="gla_in", w_is_nk=True)
    w_gate_pad = jnp.pad(gla_w_gate[0], ((0, LANES - GATE_RANK), (0, 0)))
    w_a_pad = jnp.pad(gla_w_in[0, :, 2 * HK + 2 * HV:], ((0, 0), (0, LANES - GATE_RANK)))
    g = gla_gate(h, w_a_pad, w_gate_pad, gla_b_gate[0].reshape(1, HK))
    gn = gla_norm[0].reshape(1, DV)
    gated, gla_prompt_state = gla_prompt(p, g, gn)
    gated, gla_sample_state = gla_sample(gated, p, g, gn, state_gla[0])
    x3 = linear(gated, gla_w_out, 0, D, tn=1024, tm=TM_DENSE, res=x2, name="gla_out")

    route, idx = router(x3, norm_ffn[1], jnp.transpose(moe_w_router[0]))
    pos1, pos2, tstart, ntiles = _group_tables(idx)
    act, _ = grouped_matmul(None, (moe_w_gate, moe_w_up), tstart, ntiles, tm=GROUP_ROWS, tn=1792,
                            ck=256, out_dtype=BF16, vmem_mb=56, name="moe_up",
                            gather_from=(pos1, pos2, x3, norm_ffn[1]))
    y = grouped_matmul(act, (moe_w_down,), tstart, ntiles, tm=GROUP_ROWS, tn=1024, ck=896,
                       out_dtype=F32, vmem_mb=56, name="moe_down")
    y_prompt, y_sample = moe_combine(pos1, pos2, x3, route, norm_final, y)

    y_prompt = y_prompt.reshape(N_PROMPT_SEQ, SEQ, D)
    y_sample = y_sample.reshape(M_SAMPLE, 1, D)
    return (y_prompt, y_sample,
            conv_prompt_state.reshape(1, N_PROMPT_SEQ, 2, D), conv_sample_state,
            gla_prompt_state.reshape(1, N_PROMPT_SEQ, HEADS, DK, DV),
            gla_sample_state.reshape(1, M_SAMPLE, HEADS, DK, DV))
```

```python
import functools

import jax
import jax.numpy as jnp
from jax import lax
from jax.experimental import pallas as pl
from jax.experimental.pallas import tpu as pltpu

F32 = jnp.float32
BF16 = jnp.bfloat16

D = 2048
N_PROMPT_SEQ = 4
SEQ = 2048
M_PROMPT = N_PROMPT_SEQ * SEQ
M_SAMPLE = 128
M_TOK = M_PROMPT + M_SAMPLE
HEADS = 4
DK = 256
DV = 512
HK = HEADS * DK
HV = HEADS * DV
GATE_RANK = 16
GATE_TAU = 16.0
CHUNK = 64
N_CHUNKS = SEQ // CHUNK
D_FF = 5632
N_EXPERTS = 8
D_FF_EXPERT = 7168
EPS = 1e-6
LANES = 128

TM_DENSE = 1040
GROUP_ROWS = 256
N_ASSIGN = 2 * M_TOK
N_GROUP_TILES = N_ASSIGN // GROUP_ROWS + N_EXPERTS
R_PAD = N_GROUP_TILES * GROUP_ROWS


def _params(sem, vmem_mb):
    return pltpu.CompilerParams(dimension_semantics=sem,
                                vmem_limit_bytes=vmem_mb * 1024 * 1024)


def _rmsnorm(x, g):
    return x * lax.rsqrt(jnp.mean(x * x, axis=-1, keepdims=True) + EPS) * g


def _silu(x):
    return x * jax.nn.sigmoid(x)


STACK_ROWS = 512
STACK_PROMPT_STEPS = M_PROMPT // STACK_ROWS


def _stack_norm_kernel(xp_ref, xs_ref, g_ref, x_ref, h_ref):
    i = pl.program_id(0)
    g = g_ref[...]

    @pl.when(i < STACK_PROMPT_STEPS)
    def _():
        x = xp_ref[...]
        x_ref[...] = x
        h_ref[...] = _rmsnorm(x, g).astype(h_ref.dtype)

    @pl.when(i == STACK_PROMPT_STEPS)
    def _():
        x = xs_ref[...]
        x_ref[0:M_SAMPLE, :] = x
        h_ref[0:M_SAMPLE, :] = _rmsnorm(x, g).astype(h_ref.dtype)


def stack_norm(xp, xs, g):
    last_prompt = STACK_PROMPT_STEPS - 1
    return pl.pallas_call(
        _stack_norm_kernel,
        grid=(STACK_PROMPT_STEPS + 1,),
        in_specs=[pl.BlockSpec((STACK_ROWS, D), lambda i: (jnp.minimum(i, last_prompt), 0)),
                  pl.BlockSpec((M_SAMPLE, D), lambda i: (0, 0)),
                  pl.BlockSpec((1, D), lambda i: (0, 0))],
        out_specs=[pl.BlockSpec((STACK_ROWS, D), lambda i: (i, 0)),
                   pl.BlockSpec((STACK_ROWS, D), lambda i: (i, 0))],
        out_shape=[jax.ShapeDtypeStruct((M_TOK, D), F32),
                   jax.ShapeDtypeStruct((M_TOK, D), BF16)],
        compiler_params=_params(("arbitrary",), 40),
        name="stack_norm",
    )(xp, xs, g.reshape(1, D))


def _linear_kernel(*refs, has_res, w_is_nk):
    if has_res:
        x_ref, w_ref, r_ref, o_ref, wb_ref = refs
    else:
        x_ref, w_ref, o_ref, wb_ref = refs

    @pl.when(pl.program_id(1) == 0)
    def _():
        wb_ref[...] = w_ref[...].astype(BF16)

    contract_w = 1 if w_is_nk else 0
    acc = lax.dot_general(x_ref[...], wb_ref[...], (((1,), (contract_w,)), ((), ())),
                          preferred_element_type=F32)
    if has_res:
        acc = acc + r_ref[...]
    o_ref[...] = acc.astype(o_ref.dtype)


def linear(x, w, layer, n_out, tn, tm, res=None, out_dtype=F32, vmem_mb=56, name="linear",
           w_is_nk=False):
    m, k = x.shape
    wblock = (None, tn, k) if w_is_nk else (None, k, tn)
    wmap = (lambda j, i: (layer, j, 0)) if w_is_nk else (lambda j, i: (layer, 0, j))
    in_specs = [pl.BlockSpec((tm, k), lambda j, i: (i, 0)),
                pl.BlockSpec(wblock, wmap)]
    args = [x, w]
    if res is not None:
        in_specs.append(pl.BlockSpec((tm, tn), lambda j, i: (i, j)))
        args.append(res)
    return pl.pallas_call(
        functools.partial(_linear_kernel, has_res=res is not None, w_is_nk=w_is_nk),
        grid=(n_out // tn, m // tm),
        in_specs=in_specs,
        out_specs=pl.BlockSpec((tm, tn), lambda j, i: (i, j)),
        out_shape=jax.ShapeDtypeStruct((m, n_out), out_dtype),
        scratch_shapes=[pltpu.VMEM(wblock[1:], BF16)],
        compiler_params=_params(("arbitrary", "arbitrary"), vmem_mb),
        name=name,
    )(*args)


def _linear_res_norm_kernel(x_ref, w_ref, r_ref, g_ref, o_ref, h_ref, wb_ref):
    @pl.when(pl.program_id(0) == 0)
    def _():
        wb_ref[...] = w_ref[...].astype(BF16)

    y = jnp.dot(x_ref[...], wb_ref[...], preferred_element_type=F32) + r_ref[...]
    o_ref[...] = y
    h_ref[...] = _rmsnorm(y, g_ref[...]).astype(h_ref.dtype)


def linear_res_norm(x, w, layer, res, g, tm=416, name="linear_res_norm"):
    m, k = x.shape
    return pl.pallas_call(
        _linear_res_norm_kernel,
        grid=(m // tm,),
        in_specs=[pl.BlockSpec((tm, k), lambda i: (i, 0)),
                  pl.BlockSpec((None, k, D), lambda i: (layer, 0, 0), pipeline_mode=pl.Buffered(1)),
                  pl.BlockSpec((tm, D), lambda i: (i, 0)),
                  pl.BlockSpec((1, D), lambda i: (0, 0))],
        out_specs=[pl.BlockSpec((tm, D), lambda i: (i, 0)),
                   pl.BlockSpec((tm, D), lambda i: (i, 0))],
        out_shape=[jax.ShapeDtypeStruct((m, D), F32),
                   jax.ShapeDtypeStruct((m, D), BF16)],
        scratch_shapes=[pltpu.VMEM((k, D), BF16)],
        compiler_params=_params(("arbitrary",), 56),
        name=name,
    )(x, w, res, g.reshape(1, D))


def _swiglu_kernel(x_ref, wg_ref, wu_ref, o_ref, wgb_ref, wub_ref):
    @pl.when(pl.program_id(1) == 0)
    def _():
        wgb_ref[...] = wg_ref[...].astype(BF16)
        wub_ref[...] = wu_ref[...].astype(BF16)

    x = x_ref[...]
    g = jnp.dot(x, wgb_ref[...], preferred_element_type=F32)
    u = jnp.dot(x, wub_ref[...], preferred_element_type=F32)
    o_ref[...] = (_silu(g) * u).astype(o_ref.dtype)


def swiglu_up(x, wg, wu, layer, tn=512, tm=TM_DENSE):
    m, k = x.shape
    f = wg.shape[-1]
    wspec = pl.BlockSpec((None, k, tn), lambda j, i: (layer, 0, j))
    return pl.pallas_call(
        _swiglu_kernel,
        grid=(f // tn, m // tm),
        in_specs=[pl.BlockSpec((tm, k), lambda j, i: (i, 0)), wspec, wspec],
        out_specs=pl.BlockSpec((tm, tn), lambda j, i: (i, j)),
        out_shape=jax.ShapeDtypeStruct((m, f), BF16),
        scratch_shapes=[pltpu.VMEM((k, tn), BF16), pltpu.VMEM((k, tn), BF16)],
        compiler_params=_params(("arbitrary", "arbitrary"), 56),
        name="ffn_up",
    )(x, wg, wu)


CONV_TM = 512
CONV_TC = 512
CONV_TILES_PER_SEQ = SEQ // CONV_TM
CONV_PROMPT_TILES = M_PROMPT // CONV_TM


def _conv_kernel(x_ref, xs_ref, wh_ref, wb_ref, wc_ref, cw_ref, b0_ref, b1_ref,
                 bz_ref, stp_ref, s0_ref, s1_ref, wbf_ref, carry_ref):
    i = pl.program_id(1)

    @pl.when(i == 0)
    def _():
        wbf_ref[0] = wh_ref[...].astype(BF16)
        wbf_ref[1] = wb_ref[...].astype(BF16)
        wbf_ref[2] = wc_ref[...].astype(BF16)

    def project(x):
        hh = jnp.dot(x, wbf_ref[0], preferred_element_type=F32)
        bb = jnp.dot(x, wbf_ref[1], preferred_element_type=F32)
        cc = jnp.dot(x, wbf_ref[2], preferred_element_type=F32)
        return bb, cc * hh

    w = cw_ref[...]

    @pl.when(i < CONV_PROMPT_TILES)
    def _():
        @pl.when(i % CONV_TILES_PER_SEQ == 0)
        def _():
            carry_ref[...] = jnp.zeros_like(carry_ref)

        bb, u = project(x_ref[...])
        c2 = carry_ref[0:1, :]
        c1 = carry_ref[1:2, :]
        row = lax.broadcasted_iota(jnp.int32, u.shape, 0)
        u1 = jnp.where(row == 0, c1, pltpu.roll(u, 1, 0))
        u2 = jnp.where(row == 0, c2, jnp.where(row == 1, c1, pltpu.roll(u, 2, 0)))
        z = w[0:1, :] * u2
        z = z + w[1:2, :] * u1
        z = z + w[2:3, :] * u
        bz_ref[...] = (bb * z).astype(bz_ref.dtype)
        tail = u[CONV_TM - 2:CONV_TM, :]
        carry_ref[0:2, :] = tail
        stp_ref[...] = tail

    @pl.when(i == CONV_PROMPT_TILES)
    def _():
        bb, u = project(xs_ref[...])
        b1 = b1_ref[...]
        z = w[0:1, :] * b0_ref[...]
        z = z + w[1:2, :] * b1
        z = z + w[2:3, :] * u
        bz_ref[0:M_SAMPLE, :] = (bb * z).astype(bz_ref.dtype)
        s0_ref[...] = b1
        s1_ref[...] = u


def conv_mixer(h, w_in, conv_w, buf):
    tm, tc = CONV_TM, CONV_TC
    nj = D // tc
    last_tile = CONV_PROMPT_TILES - 1
    last_seq = N_PROMPT_SEQ - 1
    wspec = lambda part: pl.BlockSpec((None, D, tc), lambda j, i: (0, 0, part * nj + j))
    return pl.pallas_call(
        _conv_kernel,
        grid=(nj, CONV_PROMPT_TILES + 1),
        in_specs=[pl.BlockSpec((tm, D), lambda j, i: (jnp.minimum(i, last_tile), 0)),
                  pl.BlockSpec((M_SAMPLE, D), lambda j, i: (M_PROMPT // M_SAMPLE, 0)),
                  wspec(0), wspec(1), wspec(2),
                  pl.BlockSpec((3, tc), lambda j, i: (0, j)),
                  pl.BlockSpec((M_SAMPLE, tc), lambda j, i: (0, j)),
                  pl.BlockSpec((M_SAMPLE, tc), lambda j, i: (0, nj + j))],
        out_specs=[pl.BlockSpec((tm, tc), lambda j, i: (i, j)),
                   pl.BlockSpec((None, 2, tc),
                                lambda j, i: (jnp.minimum(i // CONV_TILES_PER_SEQ, last_seq), 0, j)),
                   pl.BlockSpec((M_SAMPLE, tc), lambda j, i: (0, j)),
                   pl.BlockSpec((M_SAMPLE, tc), lambda j, i: (0, j))],
        out_shape=[jax.ShapeDtypeStruct((M_TOK, D), BF16),
                   jax.ShapeDtypeStruct((N_PROMPT_SEQ, 2, D), F32),
                   jax.ShapeDtypeStruct((M_SAMPLE, D), F32),
                   jax.ShapeDtypeStruct((M_SAMPLE, D), F32)],
        scratch_shapes=[pltpu.VMEM((3, D, tc), BF16), pltpu.VMEM((8, tc), F32)],
        compiler_params=_params(("arbitrary", "arbitrary"), 56),
        name="conv_mixer",
    )(h, h, w_in, w_in, w_in, conv_w, buf, buf)


def _log_sigmoid(x):
    return jnp.minimum(x, 0.0) - jnp.log1p(jnp.exp(-jnp.abs(x)))


def _norm_gate_kernel(x_ref, gn_ref, wa_ref, wg_ref, bg_ref, h_ref, g_ref):
    h = _rmsnorm(x_ref[...], gn_ref[...]).astype(BF16)
    h_ref[...] = h
    a = jnp.dot(h, wa_ref[...].astype(BF16), preferred_element_type=F32)
    z = jnp.dot(a.astype(BF16), wg_ref[...].astype(BF16), preferred_element_type=F32) + bg_ref[...]
    g_ref[...] = _log_sigmoid(z) * (1.0 / GATE_TAU)


def norm_gate(x, gn, w_a_pad, w_gate_pad, b_gate, tm=832):
    return pl.pallas_call(
        _norm_gate_kernel,
        grid=(M_TOK // tm,),
        in_specs=[pl.BlockSpec((tm, D), lambda i: (i, 0)),
                  pl.BlockSpec((1, D), lambda i: (0, 0)),
                  pl.BlockSpec((D, LANES), lambda i: (0, 0)),
                  pl.BlockSpec((LANES, HK), lambda i: (0, 0)),
                  pl.BlockSpec((1, HK), lambda i: (0, 0))],
        out_specs=[pl.BlockSpec((tm, D), lambda i: (i, 0)),
                   pl.BlockSpec((tm, HK), lambda i: (i, 0))],
        out_shape=[jax.ShapeDtypeStruct((M_TOK, D), BF16),
                   jax.ShapeDtypeStruct((M_TOK, HK), F32)],
        compiler_params=_params(("arbitrary",), 48),
        name="norm_gate",
    )(x, gn.reshape(1, D), w_a_pad, w_gate_pad, b_gate)


def _row_to_cols(row):
    return jnp.transpose(jnp.broadcast_to(row, (LANES, row.shape[1])))


def _split3_bf16(x):
    x1 = x.astype(BF16)
    r1 = x - x1.astype(F32)
    x2 = r1.astype(BF16)
    x3 = (r1 - x2.astype(F32)).astype(BF16)
    return x1, x2, x3


def _gla_prompt_kernel(o_init, q_ref, k_ref, v_ref, r_ref, g_ref, gn_ref, o_ref, sout_ref, s_ref):
    del o_init
    c = pl.program_id(1)

    @pl.when(c == 0)
    def _():
        s_ref[...] = jnp.zeros_like(s_ref)

    row = lax.broadcasted_iota(jnp.int32, (CHUNK, CHUNK), 0)
    col = lax.broadcasted_iota(jnp.int32, (CHUNK, CHUNK), 1)
    tri = row >= col
    trib = tri.astype(BF16)

    g1, g2, g3 = _split3_bf16(g_ref[...])
    b = (jnp.dot(trib, g1, preferred_element_type=F32)
         + jnp.dot(trib, g2, preferred_element_type=F32)
         + jnp.dot(trib, g3, preferred_element_type=F32))
    b_last = b[CHUNK - 1:CHUNK, :]
    q = q_ref[...] * (DK ** -0.5)
    k = k_ref[...]
    q_dec = (q * jnp.exp(b)).astype(BF16)
    k_inv = (k * jnp.exp(-b)).astype(BF16)
    k_end = (k * jnp.exp(b_last - b)).astype(BF16)
    decay = jnp.exp(b_last)
    gn = gn_ref[...]

    for h in range(HEADS):
        ks = slice(h * DK, (h + 1) * DK)
        vs = slice(h * DV, (h + 1) * DV)
        vb = v_ref[:, vs].astype(BF16)
        s_old = s_ref[h]
        scores = lax.dot_general(q_dec[:, ks], k_inv[:, ks], (((1,), (1,)), ((), ())),
                                 preferred_element_type=F32)
        scores = jnp.where(tri, scores, 0.0).astype(BF16)
        o = (jnp.dot(scores, vb, preferred_element_type=F32)
             + jnp.dot(q_dec[:, ks], s_old.astype(BF16), preferred_element_type=F32))
        kv = lax.dot_general(k_end[:, ks], vb, (((0,), (0,)), ((), ())),
                             preferred_element_type=F32)
        dcol = _row_to_cols(decay[:, ks])
        s_ref[h] = jnp.concatenate(
            [s_old[:, j * LANES:(j + 1) * LANES] * dcol for j in range(DV // LANES)], axis=1) + kv
        on = _rmsnorm(o, gn)
        o_ref[:, vs] = (_silu(r_ref[:, vs]) * on).astype(o_ref.dtype)

    @pl.when(c == N_CHUNKS - 1)
    def _():
        sout_ref[...] = s_ref[...]


def gla_prompt(p, g, g_norm):
    rows = lambda n, c: n * N_CHUNKS + c
    return pl.pallas_call(
        _gla_prompt_kernel,
        grid=(N_PROMPT_SEQ, N_CHUNKS),
        in_specs=[pl.BlockSpec(memory_space=pl.ANY),
                  pl.BlockSpec((CHUNK, HK), lambda n, c: (rows(n, c), 0)),
                  pl.BlockSpec((CHUNK, HK), lambda n, c: (rows(n, c), 1)),
                  pl.BlockSpec((CHUNK, HV), lambda n, c: (rows(n, c), 1)),
                  pl.BlockSpec((CHUNK, HV), lambda n, c: (rows(n, c), 2)),
                  pl.BlockSpec((CHUNK, HK), lambda n, c: (rows(n, c), 0)),
                  pl.BlockSpec((1, DV), lambda n, c: (0, 0))],
        out_specs=[pl.BlockSpec((CHUNK, HV), lambda n, c: (rows(n, c), 0)),
                   pl.BlockSpec((None, HEADS, DK, DV), lambda n, c: (n, 0, 0, 0))],
        out_shape=[jax.ShapeDtypeStruct((M_TOK, HV), BF16),
                   jax.ShapeDtypeStruct((N_PROMPT_SEQ, HEADS, DK, DV), F32)],
        scratch_shapes=[pltpu.VMEM((HEADS, DK, DV), F32)],
        input_output_aliases={0: 0},
        compiler_params=_params(("arbitrary", "arbitrary"), 40),
        name="gla_prompt",
    )(jnp.zeros((M_TOK, HV), BF16), p, p, p, p, g, g_norm)


GLA_SAMPLE_SEQS = 2


def _gla_sample_kernel(gated_any, q_ref, k_ref, v_ref, r_ref, g_ref, gn_ref, s_ref,
                       gated_ref, sout_ref, qt_ref, kt_ref, et_ref, o_scr):
    del gated_any
    i = pl.program_id(0)

    @pl.when(i == 0)
    def _():
        q = q_ref[...] * (DK ** -0.5)
        k = k_ref[...]
        e = jnp.exp(g_ref[...])
        for h in range(HEADS):
            ks = slice(h * DK, (h + 1) * DK)
            qt_ref[h] = jnp.transpose(q[:, ks])
            kt_ref[h] = jnp.transpose(k[:, ks])
            et_ref[h] = jnp.transpose(e[:, ks])

    lane = lax.broadcasted_iota(jnp.int32, (DK, M_SAMPLE), 1)
    for s in range(GLA_SAMPLE_SEQS):
        n = i * GLA_SAMPLE_SEQS + s
        pick = lane == n

        def column(t):
            return jnp.sum(jnp.where(pick, t, 0.0), axis=1, keepdims=True)

        for h in range(HEADS):
            vs = slice(h * DV, (h + 1) * DV)
            v_row = v_ref[pl.ds(n, 1), vs]
            s_new = s_ref[s, h] * column(et_ref[h]) + column(kt_ref[h]) * v_row
            sout_ref[s, h] = s_new
            o_scr[pl.ds(n, 1), vs] = jnp.sum(column(qt_ref[h]) * s_new, axis=0, keepdims=True)

    @pl.when(i == pl.num_programs(0) - 1)
    def _():
        gn = gn_ref[...]
        for h in range(HEADS):
            vs = slice(h * DV, (h + 1) * DV)
            on = _rmsnorm(o_scr[:, vs], gn)
            gated_ref[:, vs] = (_silu(r_ref[:, vs]) * on).astype(gated_ref.dtype)


def gla_sample(gated, p, g, g_norm, state):
    rb = M_PROMPT // M_SAMPLE
    bs = GLA_SAMPLE_SEQS
    return pl.pallas_call(
        _gla_sample_kernel,
        grid=(M_SAMPLE // bs,),
        in_specs=[pl.BlockSpec(memory_space=pl.ANY),
                  pl.BlockSpec((M_SAMPLE, HK), lambda i: (rb, 0)),
                  pl.BlockSpec((M_SAMPLE, HK), lambda i: (rb, 1)),
                  pl.BlockSpec((M_SAMPLE, HV), lambda i: (rb, 1)),
                  pl.BlockSpec((M_SAMPLE, HV), lambda i: (rb, 2)),
                  pl.BlockSpec((M_SAMPLE, HK), lambda i: (rb, 0)),
                  pl.BlockSpec((1, DV), lambda i: (0, 0)),
                  pl.BlockSpec((bs, HEADS, DK, DV), lambda i: (i, 0, 0, 0))],
        out_specs=[pl.BlockSpec((M_SAMPLE, HV), lambda i: (rb, 0)),
                   pl.BlockSpec((bs, HEADS, DK, DV), lambda i: (i, 0, 0, 0))],
        out_shape=[jax.ShapeDtypeStruct((M_TOK, HV), BF16),
                   jax.ShapeDtypeStruct((M_SAMPLE, HEADS, DK, DV), F32)],
        scratch_shapes=[pltpu.VMEM((HEADS, DK, M_SAMPLE), F32),
                        pltpu.VMEM((HEADS, DK, M_SAMPLE), F32),
                        pltpu.VMEM((HEADS, DK, M_SAMPLE), F32),
                        pltpu.VMEM((M_SAMPLE, HV), F32)],
        input_output_aliases={0: 0},
        compiler_params=_params(("arbitrary",), 40),
        name="gla_sample",
    )(gated, p, p, p, p, g, g_norm, state)


def _router_kernel(x_ref, g_ref, wr_ref, route_ref, idx_ref):
    h = _rmsnorm(x_ref[...], g_ref[...])
    lane = lax.broadcasted_iota(jnp.int32, route_ref.shape, 1)
    lane_f = lane.astype(F32)
    neg = jnp.float32(-jnp.inf)
    wr = wr_ref[...]
    logits = jnp.full(route_ref.shape, neg, F32)
    for e in range(N_EXPERTS):
        logit_e = jnp.sum(h * wr[e:e + 1, :], axis=1, keepdims=True)
        logits = jnp.where(lane == e, logit_e, logits)
    m1 = jnp.max(logits, axis=1, keepdims=True)
    i1 = jnp.min(jnp.where(logits == m1, lane_f, float(LANES)), axis=1, keepdims=True)
    rest = jnp.where(lane_f == i1, neg, logits)
    m2 = jnp.max(rest, axis=1, keepdims=True)
    i2 = jnp.min(jnp.where(rest == m2, lane_f, float(LANES)), axis=1, keepdims=True)
    e2 = jnp.exp(m2 - m1)
    den = 1.0 + e2
    w1 = 1.0 / den
    w2 = e2 / den
    route_ref[...] = jnp.where(lane == 0, w1, jnp.where(lane == 1, w2, 0.0))
    idx_ref[...] = jnp.where(lane == 0, i1, jnp.where(lane == 1, i2, 0.0)).astype(jnp.int32)


def router(x, g, w_router_t, tm=416):
    return pl.pallas_call(
        _router_kernel,
        grid=(M_TOK // tm,),
        in_specs=[pl.BlockSpec((tm, D), lambda i: (i, 0)),
                  pl.BlockSpec((1, D), lambda i: (0, 0)),
                  pl.BlockSpec((N_EXPERTS, D), lambda i: (0, 0))],
        out_specs=[pl.BlockSpec((tm, LANES), lambda i: (i, 0)),
                   pl.BlockSpec((tm, LANES), lambda i: (i, 0))],
        out_shape=[jax.ShapeDtypeStruct((M_TOK, LANES), F32),
                   jax.ShapeDtypeStruct((M_TOK, LANES), jnp.int32)],
        compiler_params=_params(("arbitrary",), 40),
        name="router",
    )(x, g.reshape(1, D), w_router_t)


def _grouped_kernel(*refs, n_w, n_col, total_tiles, gather):
    if gather:
        tstart, ntiles, dest1, dest2, src_hbm, gn_ref = refs[:6]
        w_hbm = refs[6:6 + n_w]
        o_hbm, x_hbm = refs[6 + n_w:8 + n_w]
        wbf, stage, xbuf, obuf, xsem, osem, wsem, gbuf, gsem, xwsem, tok = refs[8 + n_w:]
    else:
        tstart, ntiles, x_hbm = refs[:3]
        w_hbm = refs[3:3 + n_w]
        o_hbm = refs[3 + n_w]
        wbf, stage, xbuf, obuf, xsem, osem, wsem = refs[4 + n_w:]
    s = pl.program_id(0)
    n_items = pl.num_programs(0)
    e = s // n_col
    j = lax.rem(s, n_col)
    p = lax.rem(s, 2)
    tm = xbuf.shape[1]
    tn = obuf.shape[2]
    ck = stage.shape[2]
    n_chunks = wbf.shape[2] // ck
    nt = ntiles[e]
    t0 = tstart[e]
    col = pl.multiple_of(j * tn, tn)
    has_next = s + 1 < n_items

    def w_copies(item, c, q):
        row = pl.multiple_of(c * ck, ck)
        wcol = pl.multiple_of(lax.rem(item, n_col) * tn, tn)
        return [pltpu.make_async_copy(w.at[0, item // n_col, pl.ds(row, ck), pl.ds(wcol, tn)],
                                      stage.at[q, i], wsem.at[q, i])
                for i, w in enumerate(w_hbm)]

    def w_start(item, c, q):
        for cp in w_copies(item, c, q):
            cp.start(priority=1)

    def w_chunk(item, c, dst):
        q = lax.rem(c, 2)

        @pl.when(c + 1 < n_chunks)
        def _():
            w_start(item, c + 1, 1 - q)

        row = pl.multiple_of(c * ck, ck)
        for i, cp in enumerate(w_copies(item, c, q)):
            cp.wait()
            wbf[dst, i, pl.ds(row, ck), :] = stage[q, i].astype(BF16)

    @pl.when(s == 0)
    def _():
        w_start(0, 0, 0)

        if gather:
            def clear(r, carry):
                tok[r] = 0
                return carry

            lax.fori_loop(0, tok.shape[0], clear, 0, unroll=8)

            def invert(t, carry):
                tok[dest1[t]] = t
                tok[dest2[t]] = t
                return carry

            lax.fori_loop(0, dest1.shape[0], invert, 0, unroll=8)

        def first(c, carry):
            w_chunk(0, c, 0)
            return carry

        lax.fori_loop(0, n_chunks, first, 0)

    @pl.when(has_next)
    def _():
        w_start(s + 1, 0, 0)

    def x_copy(tile, slot):
        row = pl.multiple_of(tile * tm, tm)
        return pltpu.make_async_copy(x_hbm.at[pl.ds(row, tm)], xbuf.at[slot], xsem.at[slot])

    def o_copy(tile, slot):
        row = pl.multiple_of(tile * tm, tm)
        return pltpu.make_async_copy(obuf.at[slot], o_hbm.at[pl.ds(row, tm), pl.ds(col, tn)],
                                     osem.at[slot])

    nx = xbuf.shape[0]

    def when_fetching(fn):
        if gather:
            pl.when(j > 0)(fn)
        else:
            fn()

    def g_issue(t, gslot):
        base = (t0 + t) * tm

        def rows(r, carry):
            pltpu.make_async_copy(src_hbm.at[pl.ds(tok[base + r], 1)],
                                  gbuf.at[gslot, pl.ds(r, 1)], gsem.at[gslot]).start()
            return carry

        lax.fori_loop(0, tm, rows, 0, unroll=8)

    def xw_copy(tile, xslot):
        row = pl.multiple_of(tile * tm, tm)
        return pltpu.make_async_copy(xbuf.at[xslot], x_hbm.at[pl.ds(row, tm)], xwsem.at[xslot])

    @when_fetching
    def _():
        for d in range(nx - 1):
            @pl.when(d < nt)
            def _():
                x_copy(t0 + d, d).start()

    if gather:
        ng = gbuf.shape[0]
        for d in range(ng - 1):
            @pl.when((j == 0) & (d < nt))
            def _():
                g_issue(d, d)

    def body(t, chunks_done):
        slot = lax.rem(t, 2)
        xslot = lax.rem(t, nx)

        @when_fetching
        def _():
            x_copy(t0 + t, xslot).wait()
            ahead = t + (nx - 1)

            @pl.when(ahead < nt)
            def _():
                x_copy(t0 + ahead, lax.rem(ahead, nx)).start()

        if gather:
            @pl.when(j == 0)
            def _():
                gslot = lax.rem(t, ng)
                g_ahead = t + (ng - 1)

                @pl.when(g_ahead < nt)
                def _():
                    g_issue(g_ahead, lax.rem(g_ahead, ng))

                pltpu.make_async_copy(src_hbm.at[pl.ds(0, tm)], gbuf.at[gslot], gsem.at[gslot]).wait()

                @pl.when(t >= nx)
                def _():
                    xw_copy(t0 + t - nx, xslot).wait()

                xbuf[xslot] = _rmsnorm(gbuf[gslot], gn_ref[...]).astype(xbuf.dtype)
                xw_copy(t0 + t, xslot).start()

        @pl.when(t >= 2)
        def _():
            o_copy(t0 + t - 2, slot).wait()

        x = xbuf[xslot]
        if n_w == 2:
            g = jnp.dot(x, wbf[p, 0], preferred_element_type=F32)
            u = jnp.dot(x, wbf[p, 1], preferred_element_type=F32)
            obuf[slot] = (_silu(g) * u).astype(obuf.dtype)
        else:
            obuf[slot] = jnp.dot(x, wbf[p, 0], preferred_element_type=F32).astype(obuf.dtype)
        o_copy(t0 + t, slot).start()

        stream = has_next & (chunks_done < n_chunks)

        @pl.when(stream)
        def _():
            w_chunk(s + 1, chunks_done, 1 - p)

        return chunks_done + stream.astype(jnp.int32)

    chunks_done = lax.fori_loop(0, nt, body, jnp.int32(0))

    @pl.when(has_next)
    def _():
        def rest(c, carry):
            w_chunk(s + 1, c, 1 - p)
            return carry

        lax.fori_loop(chunks_done, n_chunks, rest, 0)

    @pl.when(nt >= 2)
    def _():
        o_copy(t0 + nt - 2, lax.rem(nt, 2)).wait()

    @pl.when(nt >= 1)
    def _():
        o_copy(t0 + nt - 1, lax.rem(nt + 1, 2)).wait()

    if gather:
        @pl.when(j == 0)
        def _():
            for d in range(nx):
                @pl.when(nt > d)
                def _():
                    tile = nt - 1 - d
                    xw_copy(t0 + tile, lax.rem(tile, nx)).wait()

    @pl.when(e == N_EXPERTS - 1)
    def _():
        obuf[0] = jnp.zeros(obuf.shape[1:], obuf.dtype)

        def zero_tile(tile, carry):
            cp = o_copy(tile, 0)
            cp.start()
            cp.wait()
            return carry

        lax.fori_loop(t0 + nt, total_tiles, zero_tile, 0)

        if gather:
            @pl.when(j == 0)
            def _():
                xbuf[0] = jnp.zeros(xbuf.shape[1:], xbuf.dtype)

                def zero_x(tile, carry):
                    cp = xw_copy(tile, 0)
                    cp.start()
                    cp.wait()
                    return carry

                lax.fori_loop(t0 + nt, total_tiles, zero_x, 0)


def grouped_matmul(x, ws, tstart, ntiles, tm, tn, ck, out_dtype, vmem_mb, name, gather_from=None):
    n_x = 3
    n = ws[0].shape[-1]
    n_w = len(ws)
    n_col = n // tn
    any_spec = pl.BlockSpec(memory_space=pl.ANY)
    gather = gather_from is not None
    if gather:
        dest1, dest2, src, g = gather_from
        k = src.shape[1]
        prefetch = (tstart, ntiles, dest1, dest2)
        inputs = (src, g.reshape(1, k)) + tuple(ws)
        in_specs = [any_spec, pl.BlockSpec((1, k), lambda s, *_: (0, 0))] + [any_spec] * n_w
        out_specs = [any_spec, any_spec]
        out_shape = [jax.ShapeDtypeStruct((R_PAD, n), out_dtype),
                     jax.ShapeDtypeStruct((R_PAD, k), BF16)]
        extra_scratch = [pltpu.VMEM((3, tm, k), F32), pltpu.SemaphoreType.DMA((3,)),
                         pltpu.SemaphoreType.DMA((n_x,)), pltpu.SMEM((R_PAD,), jnp.int32)]
    else:
        k = x.shape[1]
        prefetch = (tstart, ntiles)
        inputs = (x,) + tuple(ws)
        in_specs = [any_spec] * (1 + n_w)
        out_specs = any_spec
        out_shape = jax.ShapeDtypeStruct((R_PAD, n), out_dtype)
        extra_scratch = []
    return pl.pallas_call(
        functools.partial(_grouped_kernel, n_w=n_w, n_col=n_col, total_tiles=R_PAD // tm,
                          gather=gather),
        grid_spec=pltpu.PrefetchScalarGridSpec(
            num_scalar_prefetch=len(prefetch),
            grid=(N_EXPERTS * n_col,),
            in_specs=in_specs,
            out_specs=out_specs,
            scratch_shapes=[pltpu.VMEM((2, n_w, k, tn), BF16), pltpu.VMEM((2, n_w, ck, tn), F32),
                            pltpu.VMEM((n_x, tm, k), BF16), pltpu.VMEM((2, tm, tn), out_dtype),
                            pltpu.SemaphoreType.DMA((n_x,)), pltpu.SemaphoreType.DMA((2,)),
                            pltpu.SemaphoreType.DMA((2, n_w))] + extra_scratch),
        out_shape=out_shape,
        compiler_params=_params(("arbitrary",), vmem_mb),
        name=name,
    )(*prefetch, *inputs)


COMBINE_ROWS = 128
COMBINE_PROMPT_STEPS = M_PROMPT // COMBINE_ROWS


def _combine_kernel(p1_ref, p2_ref, x_ref, route_ref, g_ref, y_hbm, op_ref, os_ref, b1, b2, sem):
    i = pl.program_id(0)
    tm = b1.shape[1]
    slot = lax.rem(i, 2)

    def issue(tile, dst_slot):
        base = tile * tm

        def body(r, carry):
            pltpu.make_async_copy(y_hbm.at[pl.ds(p1_ref[base + r], 1)], b1.at[dst_slot, pl.ds(r, 1)],
                                  sem.at[0, dst_slot]).start(priority=0)
            pltpu.make_async_copy(y_hbm.at[pl.ds(p2_ref[base + r], 1)], b2.at[dst_slot, pl.ds(r, 1)],
                                  sem.at[1, dst_slot]).start(priority=1)
            return carry

        lax.fori_loop(0, tm, body, 0, unroll=8)

    @pl.when(i == 0)
    def _():
        issue(0, 0)

    @pl.when(i + 1 < pl.num_programs(0))
    def _():
        issue(i + 1, 1 - slot)

    pltpu.make_async_copy(y_hbm.at[pl.ds(0, tm)], b1.at[slot], sem.at[0, slot]).wait()
    pltpu.make_async_copy(y_hbm.at[pl.ds(0, tm)], b2.at[slot], sem.at[1, slot]).wait()
    route = route_ref[...]
    w1 = route[:, 0:1]
    w2 = route[:, 1:2]
    x = x_ref[...] + (w1 * b1[slot] + w2 * b2[slot])
    out = _rmsnorm(x, g_ref[...])

    @pl.when(i < COMBINE_PROMPT_STEPS)
    def _():
        op_ref[...] = out

    @pl.when(i == COMBINE_PROMPT_STEPS)
    def _():
        os_ref[...] = out


def moe_combine(pos1, pos2, x, route, g_final, y):
    tm = COMBINE_ROWS
    last_prompt = COMBINE_PROMPT_STEPS - 1
    return pl.pallas_call(
        _combine_kernel,
        grid_spec=pltpu.PrefetchScalarGridSpec(
            num_scalar_prefetch=2,
            grid=(M_TOK // tm,),
            in_specs=[pl.BlockSpec((tm, D), lambda i, p1, p2: (i, 0)),
                      pl.BlockSpec((tm, LANES), lambda i, p1, p2: (i, 0)),
                      pl.BlockSpec((1, D), lambda i, p1, p2: (0, 0)),
                      pl.BlockSpec(memory_space=pl.ANY)],
            out_specs=[pl.BlockSpec((tm, D), lambda i, p1, p2: (jnp.minimum(i, last_prompt), 0)),
                       pl.BlockSpec((M_SAMPLE, D), lambda i, p1, p2: (0, 0))],
            scratch_shapes=[pltpu.VMEM((2, tm, D), F32), pltpu.VMEM((2, tm, D), F32),
                            pltpu.SemaphoreType.DMA((2, 2))]),
        out_shape=[jax.ShapeDtypeStruct((M_PROMPT, D), F32),
                   jax.ShapeDtypeStruct((M_SAMPLE, D), F32)],
        compiler_params=_params(("arbitrary",), 40),
        name="moe_combine",
    )(pos1, pos2, x, route, g_final.reshape(1, D), y)


def _group_tables(idx):
    e_flat = jnp.concatenate([idx[:, 0], idx[:, 1]])
    onehot = (e_flat[:, None] == jnp.arange(N_EXPERTS, dtype=jnp.int32)[None, :]).astype(jnp.int32)
    csum = jnp.cumsum(onehot, axis=0)
    counts = csum[-1]
    rank = jnp.sum(csum * onehot, axis=1) - 1
    ntiles = (counts + GROUP_ROWS - 1) // GROUP_ROWS
    tile_end = jnp.cumsum(ntiles)
    tstart = tile_end - ntiles
    dest = jnp.sum(onehot * (tstart * GROUP_ROWS)[None, :], axis=1) + rank
    return dest[:M_TOK], dest[M_TOK:], tstart, ntiles


def kernel(x_prompt, x_sample, state_conv, state_gla, norm_mix, norm_ffn, norm_final,
           conv_w_in, conv_w, conv_w_out, gla_w_in, gla_w_gate, gla_b_gate, gla_norm, gla_w_out,
           ffn_w_gate, ffn_w_up, ffn_w_down, moe_w_router, moe_w_gate, moe_w_up, moe_w_down):
    x0, h = stack_norm(x_prompt.reshape(M_PROMPT, D), x_sample.reshape(M_SAMPLE, D), norm_mix[0])
    bz, conv_prompt_state, s0, s1 = conv_mixer(
        h, conv_w_in, conv_w[0], state_conv[0].reshape(M_SAMPLE, 2 * D))
    conv_sample_state = jnp.stack([s0, s1], axis=1).reshape(1, M_SAMPLE, 2, D)
    x1, h = linear_res_norm(bz, conv_w_out, 0, x0, norm_ffn[0], name="conv_out")

    a = swiglu_up(h, ffn_w_gate, ffn_w_up, 0)
    x2 = linear(a, ffn_w_down, 0, D, tn=512, tm=640, res=x1, name="ffn_down")

    w_gate_pad = jnp.pad(gla_w_gate[0], ((0, LANES - GATE_RANK), (0, 0)))
    w_a_pad = jnp.pad(gla_w_in[0, :, 2 * HK + 2 * HV:], ((0, 0), (0, LANES - GATE_RANK)))
    h, g = norm_gate(x2, norm_mix[1], w_a_pad, w_gate_pad, gla_b_gate[0].reshape(1, HK))
    p = linear(h, jnp.swapaxes(gla_w_in, 1, 2), 0, 2 * HK + 2 * HV, tn=1024, tm=TM_DENSE,
               name="gla_in", w_is_nk=True)
    gn = gla_norm[0].reshape(1, DV)
    gated, gla_prompt_state = gla_prompt(p, g, gn)
    gated, gla_sample_state = gla_sample(gated, p, g, gn, state_gla[0])
    x3 = linear(gated, gla_w_out, 0, D, tn=1024, tm=TM_DENSE, res=x2, name="gla_out")

    route, idx = router(x3, norm_ffn[1], jnp.transpose(moe_w_router[0]))
    pos1, pos2, tstart, ntiles = _group_tables(idx)
    act, _ = grouped_matmul(None, (moe_w_gate, moe_w_up), tstart, ntiles, tm=GROUP_ROWS, tn=1792,
                            ck=256, out_dtype=BF16, vmem_mb=56, name="moe_up",
                            gather_from=(pos1, pos2, x3, norm_ffn[1]))
    y = grouped_matmul(act, (moe_w_down,), tstart, ntiles, tm=GROUP_ROWS, tn=1024, ck=896,
                       out_dtype=F32, vmem_mb=56, name="moe_down")
    y_prompt, y_sample = moe_combine(pos1, pos2, x3, route, norm_final, y)

    y_prompt = y_prompt.reshape(N_PROMPT_SEQ, SEQ, D)
    y_sample = y_sample.reshape(M_SAMPLE, 1, D)
    return (y_prompt, y_sample,
            conv_prompt_state.reshape(1, N_PROMPT_SEQ, 2, D), conv_sample_state,
            gla_prompt_state.reshape(1, N_PROMPT_SEQ, HEADS, DK, DV),
            gla_sample_state.reshape(1, M_SAMPLE, HEADS, DK, DV))
```
